```python
import math
import jax
import jax.numpy as jnp
from jax import lax
import numpy as np

D_MODEL = 1024
BATCH = 1
SEQ = 16384
DEPTH = 2

GRID_W = 64
CTX_LEN = 256
BRANCH_W = 512
NA_HEADS = 8
NA_DH = BRANCH_W // NA_HEADS
NA_WIN_R = 8
NA_WIN_C = 16
DA_HEADS = 4
DA_DH = 64
DA_VD = 2 * DA_DH
DA_QK = DA_HEADS * 2 * DA_DH
GM_GROUPS = 4
GM_CHUNK = 128
GM_GW = BRANCH_W // GM_GROUPS
N_BRANCH = 3
N_EXPERTS = 16
EC_CAPACITY = 2
EXPERT_FF = 1024
Q_BLOCK = 128
ROPE_THETA = 10000.0
ROPE_AXIS = DA_DH // 2
N_MOD = 6
EPS = 1e-6
KV_SIZES = (BRANCH_W, BRANCH_W, DA_QK, DA_HEADS * DA_VD)
Q_SIZES = (BRANCH_W, DA_QK, BRANCH_W, BRANCH_W, N_BRANCH * D_MODEL)
KV_COLS = sum(KV_SIZES)
PROJ_COLS = KV_COLS + sum(Q_SIZES)

kernel_name = "hybrid_na_diff_gmlp_ec_block"


def _split(z, sizes):
    offs = np.cumsum(sizes)[:-1].tolist()
    return jnp.split(z, offs, axis=-1)


def rmsnorm(x, g):
    xf = x.astype(jnp.float32)
    y = xf * lax.rsqrt(jnp.mean(xf * xf, axis=-1, keepdims=True) + EPS)
    return y.astype(x.dtype) * g


def layernorm(x, g, b):
    xf = x.astype(jnp.float32)
    mu = jnp.mean(xf, axis=-1, keepdims=True)
    var = jnp.mean(jnp.square(xf - mu), axis=-1, keepdims=True)
    return ((xf - mu) * lax.rsqrt(var + EPS)).astype(x.dtype) * g + b


def modulate(h, shift, scale):
    return h * (1 + scale) + shift


def rope_tables(n):
    t = jnp.arange(n)
    inv = ROPE_THETA ** (-jnp.arange(0, ROPE_AXIS, 2, dtype=jnp.float32) / ROPE_AXIS)
    ang_r = (t // GRID_W).astype(jnp.float32)[:, None] * inv
    ang_c = (t % GRID_W).astype(jnp.float32)[:, None] * inv
    return (jnp.cos(ang_r), jnp.sin(ang_r), jnp.cos(ang_c), jnp.sin(ang_c))


def _rot(x, cos, sin):
    x1, x2 = jnp.split(x, 2, axis=-1)
    return jnp.concatenate([x1 * cos - x2 * sin, x1 * sin + x2 * cos], axis=-1)


def rope_2d(x, tabs):
    cr, sr, cc, sc = [t.astype(x.dtype)[None, :, None, None, :] for t in tabs]
    xr, xcol = jnp.split(x, 2, axis=-1)
    return jnp.concatenate([_rot(xr, cr, sr), _rot(xcol, cc, sc)], axis=-1)


def dense_attn(q, k, v):
    B, n, H, dh = q.shape
    s = jnp.einsum('bqhd,bkhd->bhqk', q, k) * (dh ** -0.5)
    p = jax.nn.softmax(s.astype(jnp.float32), axis=-1).astype(v.dtype)
    return jnp.einsum('bhqk,bkhd->bqhd', p, v).reshape(B, n, H * dh)


def neighbourhood_attn(q, k, v, kc, vc, rpb):
    B, L, H, dh = q.shape
    Lc = kc.shape[1]
    rows = L // GRID_W
    wr = min(NA_WIN_R, rows)
    scale = dh ** -0.5
    kg = k.reshape(B, rows, GRID_W, H, dh)
    vg = v.reshape(B, rows, GRID_W, H, dh)
    qg = jnp.moveaxis(q.reshape(B, rows, GRID_W, H, dh), 1, 0)
    col = jnp.arange(GRID_W)
    col_start = jnp.clip(col - NA_WIN_C // 2, 0, GRID_W - NA_WIN_C)
    col_idx = col_start[:, None] + jnp.arange(NA_WIN_C)[None, :]
    dc = col_idx - col[:, None] + (NA_WIN_C - 1)

    def row_block(args):
        r, qr = args
        r0 = jnp.clip(r - wr // 2, 0, rows - wr)
        kw = lax.dynamic_slice_in_dim(kg, r0, wr, axis=1)[:, :, col_idx]
        vw = lax.dynamic_slice_in_dim(vg, r0, wr, axis=1)[:, :, col_idx]
        dr = r0 + jnp.arange(wr) - r + (NA_WIN_R - 1)
        bias = jnp.transpose(rpb[:, dr[:, None, None], dc[None]], (0, 2, 1, 3))
        s_loc = jnp.einsum('bqhd,brqjhd->bhqrj', qr, kw) * scale + bias[None]
        s_ctx = jnp.einsum('bqhd,bchd->bhqc', qr, kc) * scale
        s = jnp.concatenate([s_ctx, s_loc.reshape(B, H, GRID_W, wr * NA_WIN_C)], axis=-1)
        p = jax.nn.softmax(s.astype(jnp.float32), axis=-1).astype(v.dtype)
        p_loc = p[..., Lc:].reshape(B, H, GRID_W, wr, NA_WIN_C)
        return (jnp.einsum('bhqc,bchd->bqhd', p[..., :Lc], vc)
                + jnp.einsum('bhqrj,brqjhd->bqhd', p_loc, vw))

    out = lax.map(row_block, (jnp.arange(rows), qg))
    return jnp.moveaxis(out, 0, 1).reshape(B, L, H * dh)


def diff_attend(q, k, v, lam):
    s = jnp.einsum('bqhcd,bkhcd->bhcqk', q, k) * (q.shape[-1] ** -0.5)
    p = jax.nn.softmax(s.astype(jnp.float32), axis=-1)
    a = (p[:, :, 0] - lam * p[:, :, 1]).astype(v.dtype)
    return jnp.einsum('bhqk,bkhd->bqhd', a, v)


def diff_attn_blocks(q, k, v, lam):
    B, L, H, _, dh = q.shape
    nb = L // Q_BLOCK
    qblk = jnp.moveaxis(q.reshape(B, nb, Q_BLOCK, H, 2, dh), 1, 0)
    out = lax.map(lambda qb: diff_attend(qb, k, v, lam), qblk)
    return jnp.moveaxis(out, 0, 1).reshape(B, L, H, v.shape[-1])


def diff_head_norm(o, g, lam_init):
    B, n = o.shape[:2]
    return (rmsnorm(o, g) * (1.0 - lam_init)).reshape(B, n, -1)


def gmlp_gate(u, v, ln_g, ln_b, w_s, b_s):
    B, n, _ = u.shape
    vn = layernorm(v, ln_g, ln_b).reshape(B, n // GM_CHUNK, GM_CHUNK, GM_GROUPS, GM_GW)
    mixed = jnp.einsum('gpq,bnqgc->bnpgc', w_s, vn) + jnp.swapaxes(b_s, 0, 1)[:, :, None]
    return u * mixed.reshape(B, n, BRANCH_W)


def branch_merge(ya, yb, yc, gates, w_branch, w_out):
    ga, gb, gc = jnp.split(jax.nn.sigmoid(gates), N_BRANCH, axis=-1)
    m = ga * (ya @ w_branch[0]) + gb * (yb @ w_branch[1]) + gc * (yc @ w_branch[2])
    return m @ w_out


def ec_moe(h, w_r, w1, w3, w2):
    B, n, D = h.shape
    cap = max(1, EC_CAPACITY * n // N_EXPERTS)
    aff = jax.nn.softmax((h @ w_r).astype(jnp.float32), axis=-1)
    gate, idx = lax.top_k(jnp.swapaxes(aff, 1, 2), cap)
    xin = jax.vmap(lambda hb, ib: hb[ib])(h, idx)
    hid = jax.nn.silu(jnp.einsum('becd,edf->becf', xin, w1)) * jnp.einsum('becd,edf->becf', xin, w3)
    ye = jnp.einsum('becf,efd->becd', hid, w2) * gate[..., None].astype(h.dtype)
    return jax.vmap(lambda yb, ib: jnp.zeros((n, D), h.dtype).at[ib.reshape(-1)].add(yb.reshape(-1, D)))(ye, idx)


def _na_heads(t):
    return t.reshape(t.shape[0], t.shape[1], NA_HEADS, NA_DH)


def _da_qk(t):
    return t.reshape(t.shape[0], t.shape[1], DA_HEADS, 2, DA_DH)


def _da_v(t):
    return t.reshape(t.shape[0], t.shape[1], DA_HEADS, DA_VD)


def setup_inputs(seed: int = 0) -> dict:
    key = jax.random.key(seed)
    ks = jax.random.split(key, 32)
    f32 = jnp.float32

    def nrm(k, shape, scale):
        return jax.random.normal(k, shape, f32) * scale

    D = D_MODEL
    return {
        "x": nrm(ks[0], (BATCH, SEQ, D), 1.0),
        "c": nrm(ks[1], (BATCH, D), 1.0),
        "ctx": nrm(ks[2], (BATCH, CTX_LEN, D), 1.0),
        "c_ctx": nrm(ks[3], (D,), 1.0),
        "w_ada": nrm(ks[4], (DEPTH, D, N_MOD * D), 0.5 * D ** -0.5),
        "b_ada": nrm(ks[5], (DEPTH, N_MOD * D), 0.02),
        "g_norm1": 1.0 + nrm(ks[6], (DEPTH, D), 0.05),
        "g_norm2": 1.0 + nrm(ks[7], (DEPTH, D), 0.05),
        "w_in": nrm(ks[8], (DEPTH, D, PROJ_COLS), D ** -0.5),
        "na_rpb": nrm(ks[9], (DEPTH, NA_HEADS, 2 * NA_WIN_R - 1, 2 * NA_WIN_C - 1), 0.1),
        "da_lam_q1": nrm(ks[10], (DEPTH, DA_DH), 0.1),
        "da_lam_k1": nrm(ks[11], (DEPTH, DA_DH), 0.1),
        "da_lam_q2": nrm(ks[12], (DEPTH, DA_DH), 0.1),
        "da_lam_k2": nrm(ks[13], (DEPTH, DA_DH), 0.1),
        "da_subln_g": 1.0 + nrm(ks[14], (DEPTH, DA_VD), 0.05),
        "gm_ln_g": 1.0 + nrm(ks[15], (DEPTH, BRANCH_W), 0.05),
        "gm_ln_b": nrm(ks[16], (DEPTH, BRANCH_W), 0.02),
        "gm_w_s": nrm(ks[17], (DEPTH, GM_GROUPS, GM_CHUNK, GM_CHUNK), GM_CHUNK ** -0.5),
        "gm_b_s": 1.0 + nrm(ks[18], (DEPTH, GM_GROUPS, GM_CHUNK), 0.1),
        "w_branch": nrm(ks[19], (DEPTH, N_BRANCH, BRANCH_W, D), BRANCH_W ** -0.5),
        "w_out": nrm(ks[20], (DEPTH, D, D), D ** -0.5),
        "w_router": nrm(ks[21], (DEPTH, D, N_EXPERTS), D ** -0.5),
        "w_e1": nrm(ks[22], (DEPTH, N_EXPERTS, D, EXPERT_FF), D ** -0.5),
        "w_e3": nrm(ks[23], (DEPTH, N_EXPERTS, D, EXPERT_FF), D ** -0.5),
        "w_e2": nrm(ks[24], (DEPTH, N_EXPERTS, EXPERT_FF, D), EXPERT_FF ** -0.5),
        "g_final": 1.0 + nrm(ks[25], (D,), 0.05),
    }


def reference(x, c, ctx, c_ctx, w_ada, b_ada, g_norm1, g_norm2, w_in, na_rpb,
              da_lam_q1, da_lam_k1, da_lam_q2, da_lam_k2, da_subln_g,
              gm_ln_g, gm_ln_b, gm_w_s, gm_b_s, w_branch, w_out,
              w_router, w_e1, w_e3, w_e2, g_final):
    L = x.shape[1]
    tabs = rope_tables(L)
    xc = ctx
    for i in range(DEPTH):
        last = i == DEPTH - 1
        mod = jax.nn.silu(c) @ w_ada[i] + b_ada[i]
        sh1, sc1, gt1, sh2, sc2, gt2 = jnp.split(mod[:, None, :], N_MOD, axis=-1)
        modc = jax.nn.silu(c_ctx) @ w_ada[i] + b_ada[i]
        sh1c, sc1c, gt1c, sh2c, sc2c, gt2c = jnp.split(modc, N_MOD, axis=-1)
        lam_init = 0.8 - 0.6 * math.exp(-0.3 * i)
        lam = (jnp.exp(jnp.sum(da_lam_q1[i].astype(jnp.float32) * da_lam_k1[i].astype(jnp.float32)))
               - jnp.exp(jnp.sum(da_lam_q2[i].astype(jnp.float32) * da_lam_k2[i].astype(jnp.float32)))
               + lam_init)

        h = modulate(rmsnorm(x, g_norm1[i]), sh1, sc1)
        hc = modulate(rmsnorm(xc, g_norm1[i]), sh1c, sc1c)
        ka, va, kb, vb, qa, qb, u, v, gates = _split(h @ w_in[i], KV_SIZES + Q_SIZES)
        if last:
            kac, vac, kbc, vbc = _split(hc @ w_in[i][:, :KV_COLS], KV_SIZES)
        else:
            kac, vac, kbc, vbc, qac, qbc, uc, vcc, gatesc = _split(hc @ w_in[i], KV_SIZES + Q_SIZES)

        ya = neighbourhood_attn(_na_heads(qa), _na_heads(ka), _na_heads(va),
                                _na_heads(kac), _na_heads(vac), na_rpb[i])
        kb_all = jnp.concatenate([_da_qk(kbc), rope_2d(_da_qk(kb), tabs)], axis=1)
        vb_all = jnp.concatenate([_da_v(vbc), _da_v(vb)], axis=1)
        yb = diff_head_norm(diff_attn_blocks(rope_2d(_da_qk(qb), tabs), kb_all, vb_all, lam),
                            da_subln_g[i], lam_init)
        yc = gmlp_gate(jax.nn.gelu(u), jax.nn.gelu(v), gm_ln_g[i], gm_ln_b[i], gm_w_s[i], gm_b_s[i])
        x = x + gt1 * branch_merge(ya, yb, yc, gates, w_branch[i], w_out[i])
        if not last:
            yac = dense_attn(_na_heads(qac), _na_heads(kac), _na_heads(vac))
            ybc = diff_head_norm(diff_attend(_da_qk(qbc), _da_qk(kbc), _da_v(vbc), lam),
                                 da_subln_g[i], lam_init)
            ycc = gmlp_gate(jax.nn.gelu(uc), jax.nn.gelu(vcc), gm_ln_g[i], gm_ln_b[i], gm_w_s[i], gm_b_s[i])
            xc = xc + gt1c * branch_merge(yac, ybc, ycc, gatesc, w_branch[i], w_out[i])

        h2 = modulate(rmsnorm(x, g_norm2[i]), sh2, sc2)
        x = x + gt2 * ec_moe(h2, w_router[i], w_e1[i], w_e3[i], w_e2[i])
        if not last:
            h2c = modulate(rmsnorm(xc, g_norm2[i]), sh2c, sc2c)
            xc = xc + gt2c * ec_moe(h2c, w_router[i], w_e1[i], w_e3[i], w_e2[i])
    return rmsnorm(x, g_final)
```

```python
import functools
import math

import numpy as np
import jax
import jax.numpy as jnp
from jax import lax
from jax.experimental import pallas as pl
from jax.experimental.pallas import tpu as pltpu

F32 = jnp.float32
BF16 = jnp.bfloat16
I32 = jnp.int32

D = 1024
GRID_W = 64
BW = 512
N_COLBLK = 14
NA_HEADS = 8
NA_WIN_R = 8
NA_WIN_C = 16
NA_ROWS = 4
NA_KROWS = 12
DA_HEADS = 4
DA_DH = 64
DA_VD = 128
GM_GROUPS = 4
GM_CHUNK = 128
N_EXPERTS = 16
EC_CAPACITY = 2
ROPE_THETA = 10000.0
ROPE_AXIS = DA_DH // 2
N_MOD = 6
EPS = 1e-6
LANES = 128
SUBLANES = 8
BF16_ROWS = 16
NEG_BIG = -1e30
VMEM_LIMIT = 56 * 1024 * 1024

CB_KA, CB_VA, CB_KB, CB_VB, CB_QA, CB_QB, CB_U, CB_V = range(8)

_NT = (((1,), (1,)), ((), ()))
_TN = (((0,), (0,)), ((), ()))


def _cparams(sem):
    return pltpu.CompilerParams(dimension_semantics=sem, vmem_limit_bytes=VMEM_LIMIT)


def _gelu(x):
    return 0.5 * x * (1.0 + jnp.tanh(math.sqrt(2.0 / math.pi) * (x + 0.044715 * (x * x * x))))


def _ada_kernel(c_ref, w_ref, b_ref, o_ref):
    c = c_ref[...]
    s = c * jax.nn.sigmoid(c)
    o_ref[0] = jnp.dot(s, w_ref[0], preferred_element_type=F32,
                       precision=lax.Precision.HIGHEST) + b_ref[0]


def _ada(cc, w_ada, b_ada):
    depth = w_ada.shape[0]
    tn = 512
    return pl.pallas_call(
        _ada_kernel,
        grid=(depth, N_MOD * D // tn),
        in_specs=[pl.BlockSpec((SUBLANES, D), lambda i, j: (0, 0)),
                  pl.BlockSpec((1, D, tn), lambda i, j: (i, 0, j)),
                  pl.BlockSpec((1, 1, tn), lambda i, j: (i, 0, j))],
        out_specs=pl.BlockSpec((1, SUBLANES, tn), lambda i, j: (i, 0, j)),
        out_shape=jax.ShapeDtypeStruct((depth, SUBLANES, N_MOD * D), F32),
        compiler_params=_cparams(("arbitrary", "arbitrary")),
        name="ada_mod",
    )(cc, w_ada, b_ada.reshape(depth, 1, N_MOD * D))


def _proj_kernel(x_ref, g_ref, mod_ref, w_ref, c_ref, sp_ref, sm_ref, p_ref, qt_ref, vt_ref, h_scr, *, mrow):
    j = pl.program_id(1)

    @pl.when(j == 0)
    def _():
        x = x_ref[...]
        ms = jnp.mean(x * x, axis=-1, keepdims=True)
        xn = x * lax.rsqrt(ms + EPS) * g_ref[...]
        sh = mod_ref[mrow:mrow + 1, 0:D]
        sc = mod_ref[mrow:mrow + 1, D:2 * D]
        h_scr[...] = (xn * (1.0 + sc) + sh).astype(BF16)

    z = jnp.dot(h_scr[...], w_ref[...], preferred_element_type=F32)

    def rope(z):
        reps = BW // LANES
        c = jnp.concatenate([c_ref[...]] * reps, axis=1)
        sp = jnp.concatenate([sp_ref[...]] * reps, axis=1)
        sm = jnp.concatenate([sm_ref[...]] * reps, axis=1)
        half = ROPE_AXIS // 2
        return z * c + pltpu.roll(z, half, 1) * sp + pltpu.roll(z, BW - half, 1) * sm

    @pl.when(j == CB_KB)
    def _():
        p_ref[...] = rope(z).astype(BF16)

    @pl.when(j == CB_QB)
    def _():
        zr = rope(z)
        p_ref[...] = zr.astype(BF16)
        qt_ref[...] = zr.T.astype(BF16)

    @pl.when(j == CB_VB)
    def _():
        p_ref[...] = z.astype(BF16)
        vt_ref[...] = z.T.astype(BF16)

    @pl.when((j != CB_KB) & (j != CB_QB) & (j != CB_VB))
    def _():
        p_ref[...] = z.astype(BF16)


def _proj(x, g, mod, w, tabs, mrow, tm):
    n = x.shape[0]
    return pl.pallas_call(
        functools.partial(_proj_kernel, mrow=mrow),
        grid=(n // tm, N_COLBLK),
        in_specs=[pl.BlockSpec((tm, D), lambda i, j: (i, 0)),
                  pl.BlockSpec((1, D), lambda i, j: (0, 0)),
                  pl.BlockSpec((SUBLANES, N_MOD * D), lambda i, j: (0, 0)),
                  pl.BlockSpec((D, BW), lambda i, j: (0, j)),
                  pl.BlockSpec((tm, LANES), lambda i, j: (i, 0)),
                  pl.BlockSpec((tm, LANES), lambda i, j: (i, 0)),
                  pl.BlockSpec((tm, LANES), lambda i, j: (i, 0))],
        out_specs=[pl.BlockSpec((tm, BW), lambda i, j: (i, j)),
                   pl.BlockSpec((BW, tm), lambda i, j: (0, i)),
                   pl.BlockSpec((BW, tm), lambda i, j: (0, i))],
        out_shape=[jax.ShapeDtypeStruct((n, N_COLBLK * BW), BF16),
                   jax.ShapeDtypeStruct((BW, n), BF16),
                   jax.ShapeDtypeStruct((BW, n), BF16)],
        scratch_shapes=[pltpu.VMEM((tm, D), BF16)],
        compiler_params=_cparams(("arbitrary", "arbitrary")),
        name="proj",
    )(x, g, mod, w, *tabs)


def _na_kernel(q_ref, k0_ref, k1_ref, k2_ref, v0_ref, v1_ref, v2_ref, kc_ref, vc_ref, bias_ref, o_ref):
    nq = q_ref.shape[0]
    lane = lax.broadcasted_iota(I32, (nq, LANES), 1)
    k_refs = (kc_ref, k0_ref, k1_ref, k2_ref)
    v_refs = (vc_ref, v0_ref, v1_ref, v2_ref)
    for g in range(NA_HEADS // 2):
        sl = slice(g * LANES, (g + 1) * LANES)
        qp = q_ref[:, sl] * 0.125
        ks = [r[:, sl] for r in k_refs]
        vs = [r[:, sl] for r in v_refs]
        outs = []
        for sub in range(2):
            h = 2 * g + sub
            keep = (lane < 64) if sub == 0 else (lane >= 64)
            qz = jnp.where(keep, qp, jnp.zeros_like(qp))
            ss = [lax.dot_general(qz, ks[0], _NT, preferred_element_type=F32)]
            for t in range(3):
                s = lax.dot_general(qz, ks[1 + t], _NT, preferred_element_type=F32)
                ss.append(s + bias_ref[0, h, :, t * nq:(t + 1) * nq])
            m = ss[0].max(axis=-1, keepdims=True)
            for s in ss[1:]:
                m = jnp.maximum(m, s.max(axis=-1, keepdims=True))
            l = jnp.zeros_like(m)
            o = jnp.zeros((nq, LANES), F32)
            for s, v in zip(ss, vs):
                p = jnp.exp(s - m)
                l = l + p.sum(axis=-1, keepdims=True)
                o = o + jnp.dot(p.astype(BF16), v, preferred_element_type=F32)
            outs.append(o / l)
        o_ref[:, sl] = jnp.where(lane < 64, outs[0], outs[1]).astype(BF16)


def _na_bias(rpb, rows):
    nb = rows // NA_ROWS
    variants = []
    a = np.arange(NA_ROWS)[:, None, None, None]
    c = np.arange(GRID_W)[None, :, None, None]
    i = np.arange(NA_KROWS)[None, None, :, None]
    kc = np.arange(GRID_W)[None, None, None, :]
    for b in (0, 1, nb - 1):
        kb0 = min(max(b - 1, 0), nb - 3)
        r = NA_ROWS * b + a
        kr = NA_ROWS * kb0 + i
        r0 = np.clip(r - NA_WIN_R // 2, 0, rows - NA_WIN_R)
        c0 = np.clip(c - NA_WIN_C // 2, 0, GRID_W - NA_WIN_C)
        valid = (kr >= r0) & (kr < r0 + NA_WIN_R) & (kc >= c0) & (kc < c0 + NA_WIN_C)
        dr = np.clip(kr - r + (NA_WIN_R - 1), 0, 2 * NA_WIN_R - 2) + 0 * c + 0 * kc
        dc = np.clip(kc - c + (NA_WIN_C - 1), 0, 2 * NA_WIN_C - 2) + 0 * a + 0 * i
        nq, nk = NA_ROWS * GRID_W, NA_KROWS * GRID_W
        valid = np.broadcast_to(valid, dr.shape).reshape(nq, nk)
        bias = rpb[:, dr.reshape(nq, nk), dc.reshape(nq, nk)]
        variants.append(jnp.where(valid[None], bias, NEG_BIG))
    return jnp.stack(variants).astype(F32)


def _na(p_lat, p_ctx, bias):
    nq = NA_ROWS * GRID_W
    n_lat, n_ctx = p_lat.shape[0], p_ctx.shape[0]
    nb = n_lat // nq

    def kmap(t, col):
        return lambda b: (jnp.clip(b - 1, 0, nb - 3) + t, col)

    def bmap(b):
        return (jnp.where(b == 0, 0, jnp.where(b == nb - 1, 2, 1)), 0, 0, 0)

    blk = lambda f: pl.BlockSpec((nq, BW), f)
    return pl.pallas_call(
        _na_kernel,
        grid=(nb,),
        in_specs=[blk(lambda b: (b, CB_QA)),
                  blk(kmap(0, CB_KA)), blk(kmap(1, CB_KA)), blk(kmap(2, CB_KA)),
                  blk(kmap(0, CB_VA)), blk(kmap(1, CB_VA)), blk(kmap(2, CB_VA)),
                  pl.BlockSpec((n_ctx, BW), lambda b: (0, CB_KA)),
                  pl.BlockSpec((n_ctx, BW), lambda b: (0, CB_VA)),
                  pl.BlockSpec((1, NA_HEADS, nq, NA_KROWS * GRID_W), bmap)],
        out_specs=pl.BlockSpec((nq, BW), lambda b: (b, 0)),
        out_shape=jax.ShapeDtypeStruct((n_lat, BW), BF16),
        compiler_params=_cparams(("arbitrary",)),
        name="na_attn",
    )(p_lat, p_lat, p_lat, p_lat, p_lat, p_lat, p_lat, p_ctx, p_ctx, bias)


def _ctx_dense_kernel(q_ref, k_ref, v_ref, o_ref):
    nq = q_ref.shape[0]
    lane = lax.broadcasted_iota(I32, (nq, LANES), 1)
    for g in range(NA_HEADS // 2):
        sl = slice(g * LANES, (g + 1) * LANES)
        qp = q_ref[:, sl] * 0.125
        kp = k_ref[:, sl]
        vp = v_ref[:, sl]
        outs = []
        for sub in range(2):
            keep = (lane < 64) if sub == 0 else (lane >= 64)
            qz = jnp.where(keep, qp, jnp.zeros_like(qp))
            s = lax.dot_general(qz, kp, _NT, preferred_element_type=F32)
            m = s.max(axis=-1, keepdims=True)
            p = jnp.exp(s - m)
            l = p.sum(axis=-1, keepdims=True)
            outs.append(jnp.dot(p.astype(BF16), vp, preferred_element_type=F32) / l)
        o_ref[:, sl] = jnp.where(lane < 64, outs[0], outs[1]).astype(BF16)


def _ctx_dense(p_ctx):
    n_ctx = p_ctx.shape[0]
    blk = lambda col: pl.BlockSpec((n_ctx, BW), lambda i: (0, col))
    return pl.pallas_call(
        _ctx_dense_kernel,
        grid=(1,),
        in_specs=[blk(CB_QA), blk(CB_KA), blk(CB_VA)],
        out_specs=pl.BlockSpec((n_ctx, BW), lambda i: (0, 0)),
        out_shape=jax.ShapeDtypeStruct((n_ctx, BW), BF16),
        compiler_params=_cparams(("arbitrary",)),
        name="ctx_dense_attn",
    )(p_ctx, p_ctx, p_ctx)


def _diff_kernel(*refs, tq, nk, lam_init, with_lat):
    if with_lat:
        (qt_ref, kc_ref, vtc_ref, k_ref, vt_ref, lq1_ref, lk1_ref, lq2_ref, lk2_ref, g_ref, o_ref,
         qz_scr, m_scr, l_scr, acc_scr) = refs
    else:
        (qt_ref, kc_ref, vtc_ref, lq1_ref, lk1_ref, lq2_ref, lk2_ref, g_ref, o_ref,
         qz_scr, m_scr, l_scr, acc_scr) = refs
    j = pl.program_id(1)

    def attend(h, kh, vth):
        s = jnp.dot(kh, qz_scr[h], preferred_element_type=F32)
        m_old = m_scr[h]
        m_new = jnp.maximum(m_old, s.max(axis=0, keepdims=True))
        alpha = jnp.exp(m_old - m_new)
        p = jnp.exp(s - m_new)
        l_scr[h] = alpha * l_scr[h] + p.sum(axis=0, keepdims=True)
        acc_scr[h] = alpha * acc_scr[h] + jnp.dot(vth, p.astype(BF16), preferred_element_type=F32)
        m_scr[h] = m_new

    @pl.when(j == 0)
    def _():
        row = lax.broadcasted_iota(I32, (DA_VD, tq), 0)
        for h in range(DA_HEADS):
            qh = qt_ref[h * DA_VD:(h + 1) * DA_VD, :] * 0.125
            zero = jnp.zeros_like(qh)
            qz_scr[h] = jnp.concatenate([jnp.where(row < DA_DH, qh, zero),
                                         jnp.where(row >= DA_DH, qh, zero)], axis=1)
        m_scr[...] = jnp.full(m_scr.shape, -jnp.inf, F32)
        l_scr[...] = jnp.zeros(l_scr.shape, F32)
        acc_scr[...] = jnp.zeros(acc_scr.shape, F32)
        for h in range(DA_HEADS):
            hs = slice(h * DA_VD, (h + 1) * DA_VD)
            attend(h, kc_ref[:, hs], vtc_ref[hs, :])

    if with_lat:
        for h in range(DA_HEADS):
            hs = slice(h * DA_VD, (h + 1) * DA_VD)
            attend(h, k_ref[:, hs], vt_ref[hs, :])

    @pl.when(j == nk - 1)
    def _():
        lam = (jnp.exp(jnp.sum(lq1_ref[...] * lk1_ref[...], keepdims=True))
               - jnp.exp(jnp.sum(lq2_ref[...] * lk2_ref[...], keepdims=True)) + lam_init)
        for h in range(DA_HEADS):
            o = acc_scr[h] / l_scr[h]
            od = o[:, :tq] - lam * o[:, tq:]
            ms = jnp.mean(od * od, axis=0, keepdims=True)
            y = od * lax.rsqrt(ms + EPS) * g_ref[...] * (1.0 - lam_init)
            o_ref[:, h * DA_VD:(h + 1) * DA_VD] = y.T.astype(BF16)


def _diff(qt, p_ctx, vt_ctx, lat, lam_vecs, g_col, lam_init, tq, tk):
    n_q, n_ctx = qt.shape[1], p_ctx.shape[0]
    with_lat = lat is not None
    nk = lat[0].shape[0] // tk if with_lat else 1
    vec = pl.BlockSpec((1, DA_DH), lambda i, j: (0, 0))
    in_specs = [pl.BlockSpec((BW, tq), lambda i, j: (0, i)),
                pl.BlockSpec((n_ctx, BW), lambda i, j: (0, CB_KB)),
                pl.BlockSpec((BW, n_ctx), lambda i, j: (0, 0))]
    args = [qt, p_ctx, vt_ctx]
    if with_lat:
        in_specs += [pl.BlockSpec((tk, BW), lambda i, j: (j, CB_KB)),
                     pl.BlockSpec((BW, tk), lambda i, j: (0, j))]
        args += list(lat)
    in_specs += [vec, vec, vec, vec, pl.BlockSpec((DA_VD, 1), lambda i, j: (0, 0))]
    return pl.pallas_call(
        functools.partial(_diff_kernel, tq=tq, nk=nk, lam_init=lam_init, with_lat=with_lat),
        grid=(n_q // tq, nk),
        in_specs=in_specs,
        out_specs=pl.BlockSpec((tq, BW), lambda i, j: (i, 0)),
        out_shape=jax.ShapeDtypeStruct((n_q, BW), BF16),
        scratch_shapes=[pltpu.VMEM((DA_HEADS, DA_VD, 2 * tq), BF16),
                        pltpu.VMEM((DA_HEADS, 1, 2 * tq), F32),
                        pltpu.VMEM((DA_HEADS, 1, 2 * tq), F32),
                        pltpu.VMEM((DA_HEADS, DA_VD, 2 * tq), F32)],
        compiler_params=_cparams(("arbitrary", "arbitrary")),
        name="diff_attn",
    )(*args, *lam_vecs, g_col)


def _merge_kernel(x_ref, ya_ref, yb_ref, u_ref, v_ref, ga_ref, gb_ref, gc_ref, mod_ref, wb_ref, wo_ref,
                  lng_ref, lnb_ref, ws_ref, bst_ref, g2_ref, wrt_ref,
                  x1_ref, h2_ref, aff_ref, yc_scr, *, mrow, tm):
    ug = _gelu(u_ref[...].astype(F32))
    vg = _gelu(v_ref[...].astype(F32))
    mu = jnp.mean(vg, axis=-1, keepdims=True)
    var = jnp.mean(jnp.square(vg - mu), axis=-1, keepdims=True)
    vn = ((vg - mu) * lax.rsqrt(var + EPS) * lng_ref[...] + lnb_ref[...]).astype(BF16)
    for ci in range(tm // GM_CHUNK):
        rs = slice(ci * GM_CHUNK, (ci + 1) * GM_CHUNK)
        for g in range(GM_GROUPS):
            cs = slice(g * LANES, (g + 1) * LANES)
            mixed = jnp.dot(ws_ref[g], vn[rs, cs], preferred_element_type=F32) + bst_ref[:, g:g + 1]
            yc_scr[rs, cs] = (ug[rs, cs] * mixed).astype(BF16)

    sig = jax.nn.sigmoid
    m = sig(ga_ref[...].astype(F32)) * jnp.dot(ya_ref[...], wb_ref[0], preferred_element_type=F32)
    m = m + sig(gb_ref[...].astype(F32)) * jnp.dot(yb_ref[...], wb_ref[1], preferred_element_type=F32)
    m = m + sig(gc_ref[...].astype(F32)) * jnp.dot(yc_scr[...], wb_ref[2], preferred_element_type=F32)
    y = jnp.dot(m.astype(BF16), wo_ref[...], preferred_element_type=F32)
    gt1 = mod_ref[mrow:mrow + 1, 2 * D:3 * D]
    x1 = x_ref[...] + gt1 * y
    x1_ref[...] = x1

    ms = jnp.mean(x1 * x1, axis=-1, keepdims=True)
    sh2 = mod_ref[mrow:mrow + 1, 3 * D:4 * D]
    sc2 = mod_ref[mrow:mrow + 1, 4 * D:5 * D]
    h2 = x1 * lax.rsqrt(ms + EPS) * g2_ref[...] * (1.0 + sc2) + sh2
    for s in range(D // LANES):
        h2_ref[pl.ds(s, tm, stride=D // LANES), :] = h2[:, s * LANES:(s + 1) * LANES]

    hh = h2.astype(BF16)
    hl = (h2 - hh.astype(F32)).astype(BF16)
    w = wrt_ref[...]
    wh = w.astype(BF16)
    wl = (w - wh.astype(F32)).astype(BF16)
    lg = (lax.dot_general(wh, hh, _NT, preferred_element_type=F32)
          + lax.dot_general(wh, hl, _NT, preferred_element_type=F32)
          + lax.dot_general(wl, hh, _NT, preferred_element_type=F32))
    e = jnp.exp(lg - lg.max(axis=0, keepdims=True))
    aff_ref[...] = e / e.sum(axis=0, keepdims=True)


def _merge(x, ya, yb, p_all, mod, mrow, wb, wo, lng, lnb, ws, bst, g2, wrt, tm):
    n = x.shape[0]
    const = lambda shape: pl.BlockSpec(shape, lambda i: (0,) * len(shape))
    return pl.pallas_call(
        functools.partial(_merge_kernel, mrow=mrow, tm=tm),
        grid=(n // tm,),
        in_specs=[pl.BlockSpec((tm, D), lambda i: (i, 0)),
                  pl.BlockSpec((tm, BW), lambda i: (i, 0)),
                  pl.BlockSpec((tm, BW), lambda i: (i, 0)),
                  pl.BlockSpec((tm, BW), lambda i: (i,CB_U)),
                  pl.BlockSpec((tm, BW), lambda i: (i,CB_V)),
                  pl.BlockSpec((tm, D), lambda i: (i,4)),
                  pl.BlockSpec((tm, D), lambda i: (i,5)),
                  pl.BlockSpec((tm, D), lambda i: (i,6)),
                  const((SUBLANES, N_MOD * D)),
                  const((3, BW, D)), const((D, D)),
                  const((1, BW)), const((1, BW)),
                  const((GM_GROUPS, GM_CHUNK, GM_CHUNK)), const((GM_CHUNK, GM_GROUPS)),
                  const((1, D)), const((N_EXPERTS, D))],
        out_specs=[pl.BlockSpec((tm, D), lambda i: (i, 0)),
                   pl.BlockSpec((tm * (D // LANES), LANES), lambda i: (i, 0)),
                   pl.BlockSpec((N_EXPERTS, tm), lambda i: (0, i))],
        out_shape=[jax.ShapeDtypeStruct((n, D), F32),
                   jax.ShapeDtypeStruct((n * (D // LANES), LANES), F32),
                   jax.ShapeDtypeStruct((N_EXPERTS, n), F32)],
        scratch_shapes=[pltpu.VMEM((tm, BW), BF16)],
        compiler_params=_cparams(("arbitrary",)),
        name="merge_prenorm",
    )(x, ya, yb, p_all, p_all, p_all, p_all, p_all, mod, wb, wo, lng, lnb, ws, bst, g2, wrt)


def _route_kernel(a_ref, idx_ref, gate_ref, slot_ref, rank_ref, *, cap, nrow):
    a = a_ref[0]

    def bisect(k, prefix):
        cand = prefix | jnp.left_shift(jnp.int32(1), 30 - k)
        cand_f = pltpu.bitcast(jnp.full((nrow, LANES), cand, I32), F32)
        cnt = jnp.sum(jnp.where(a >= cand_f, 1.0, 0.0))
        return jnp.where(cnt >= cap, cand, prefix)

    thr_bits = lax.fori_loop(0, 31, bisect, jnp.int32(0))
    thr = pltpu.bitcast(jnp.full((nrow, LANES), thr_bits, I32), F32)
    gt = a > thr
    eq = a == thr

    r_i = lax.broadcasted_iota(I32, (LANES, LANES), 0)
    c_i = lax.broadcasted_iota(I32, (LANES, LANES), 1)
    upper = (r_i <= c_i).astype(BF16)
    rr = lax.broadcasted_iota(I32, (nrow, nrow), 0)
    rc = lax.broadcasted_iota(I32, (nrow, nrow), 1)
    lower_strict = (rc < rr).astype(BF16)
    upper_strict = (rr < rc).astype(BF16)

    def prefix(mask):
        xf = jnp.where(mask, 1.0, 0.0)
        incl = jnp.dot(xf.astype(BF16), upper, preferred_element_type=F32)
        tot = jnp.broadcast_to(incl[:, LANES - 1:LANES], (nrow, LANES))
        base = jnp.dot(lower_strict, tot.astype(BF16), preferred_element_type=F32)
        return xf, incl, base + incl - xf

    _, _, eq_rank = prefix(eq)
    need = cap - jnp.sum(jnp.where(gt, 1.0, 0.0))
    sel = gt | (eq & (eq_rank < need))
    xf, incl, rank = prefix(sel)
    rank_ref[0] = rank.astype(I32)
    slot_ref[0] = jnp.where(sel, rank, -1.0).astype(I32)

    ones = jnp.ones((SUBLANES, LANES), BF16)
    tot_l = lax.dot_general(ones, xf.astype(BF16), _NT, preferred_element_type=F32)
    off_l = jnp.dot(tot_l.astype(BF16), upper_strict, preferred_element_type=F32)[0:1]
    tot_l = tot_l[0:1]

    jf = lax.broadcasted_iota(I32, (cap, nrow), 0).astype(F32)
    oh_row = (off_l <= jf) & (jf < off_l + tot_l)
    ohb = jnp.where(oh_row, 1.0, 0.0).astype(BF16)
    row_id = lax.broadcasted_iota(I32, (cap, nrow), 1).astype(F32)
    off_j = jnp.sum(jnp.where(oh_row, off_l, 0.0), axis=1, keepdims=True)
    row_j = jnp.sum(jnp.where(oh_row, row_id, 0.0), axis=1, keepdims=True)
    key = jnp.where(sel, incl, 0.0).astype(BF16)
    g = jnp.dot(ohb, key, preferred_element_type=F32)
    target = lax.broadcasted_iota(I32, (cap, 1), 0).astype(F32) - off_j + 1.0
    oh_lane = g == target
    lane_id = lax.broadcasted_iota(I32, (cap, LANES), 1).astype(F32)
    lane_j = jnp.sum(jnp.where(oh_lane, lane_id, 0.0), axis=1, keepdims=True)
    idx_ref[0] = (row_j * LANES + lane_j).astype(I32)

    a1 = a.astype(BF16)
    r1 = a - a1.astype(F32)
    a2 = r1.astype(BF16)
    a3 = (r1 - a2.astype(F32)).astype(BF16)
    arow = (jnp.dot(ohb, a1, preferred_element_type=F32) + jnp.dot(ohb, a2, preferred_element_type=F32)
            + jnp.dot(ohb, a3, preferred_element_type=F32))
    gate_ref[0] = jnp.sum(jnp.where(oh_lane, arow, 0.0), axis=1, keepdims=True)


def _route(aff3, cap):
    nrow = aff3.shape[1]
    return pl.pallas_call(
        functools.partial(_route_kernel, cap=cap, nrow=nrow),
        grid=(N_EXPERTS,),
        in_specs=[pl.BlockSpec((1, nrow, LANES), lambda e: (e, 0, 0))],
        out_specs=[pl.BlockSpec((1, cap, 1), lambda e: (e, 0, 0)),
                   pl.BlockSpec((1, cap, 1), lambda e: (e, 0, 0)),
                   pl.BlockSpec((1, nrow, LANES), lambda e: (e, 0, 0)),
                   pl.BlockSpec((1, nrow, LANES), lambda e: (e, 0, 0))],
        out_shape=[jax.ShapeDtypeStruct((N_EXPERTS, cap, 1), I32),
                   jax.ShapeDtypeStruct((N_EXPERTS, cap, 1), F32),
                   jax.ShapeDtypeStruct((N_EXPERTS, nrow, LANES), I32),
                   jax.ShapeDtypeStruct((N_EXPERTS, nrow, LANES), I32)],
        compiler_params=_cparams(("arbitrary",)),
        name="ec_route",
    )(aff3)


def _ffn_kernel(idx_ref, h2_hbm, gate_ref, w1_ref, w3_ref, w2_ref, ye_ref, buf, sem, *, tc, nc):
    sub = D // LANES
    step = pl.program_id(0) * nc + pl.program_id(1)
    nsteps = N_EXPERTS * nc

    def row_copy(tok, r, slot):
        return pltpu.make_async_copy(h2_hbm.at[pl.ds(pl.multiple_of(tok * sub, sub), sub), :],
                                     buf.at[slot, pl.ds(pl.multiple_of(r * sub, sub), sub), :],
                                     sem.at[slot])

    def issue(s, slot):
        e = s // nc
        base = (s % nc) * tc

        def body(r, carry):
            row_copy(idx_ref[e, base + r], r, slot).start()
            return carry

        lax.fori_loop(0, tc, body, 0)

    @pl.when(step == 0)
    def _():
        issue(step, 0)

    @pl.when(step + 1 < nsteps)
    def _():
        issue(step + 1, (step + 1) % 2)

    slot = step % 2
    pltpu.make_async_copy(h2_hbm.at[pl.ds(0, tc * sub), :], buf.at[slot], sem.at[slot]).wait()
    xin = jnp.concatenate([buf[slot, pl.ds(s, tc, stride=sub), :] for s in range(sub)], axis=1).astype(BF16)
    a = jnp.dot(xin, w1_ref[0], preferred_element_type=F32)
    b = jnp.dot(xin, w3_ref[0], preferred_element_type=F32)
    hid = (a * jax.nn.sigmoid(a) * b).astype(BF16)
    y = jnp.dot(hid, w2_ref[0], preferred_element_type=F32)
    ye_ref[0] = (y * gate_ref[0]).astype(BF16)


def _ffn(idx, gate, h2, w1, w3, w2, tc):
    cap = idx.shape[1]
    nc = cap // tc
    sub = D // LANES
    grid_spec = pltpu.PrefetchScalarGridSpec(
        num_scalar_prefetch=1,
        grid=(N_EXPERTS, nc),
        in_specs=[pl.BlockSpec(memory_space=pl.ANY),
                  pl.BlockSpec((1, tc, 1), lambda e, c, idx: (e, c, 0)),
                  pl.BlockSpec((1, D, D), lambda e, c, idx: (e, 0, 0)),
                  pl.BlockSpec((1, D, D), lambda e, c, idx: (e, 0, 0)),
                  pl.BlockSpec((1, D, D), lambda e, c, idx: (e, 0, 0))],
        out_specs=pl.BlockSpec((1, tc, D), lambda e, c, idx: (e, c, 0)),
        scratch_shapes=[pltpu.VMEM((2, tc * sub, LANES), F32),
                        pltpu.SemaphoreType.DMA((2,))])
    return pl.pallas_call(
        functools.partial(_ffn_kernel, tc=tc, nc=nc),
        grid_spec=grid_spec,
        out_shape=jax.ShapeDtypeStruct((N_EXPERTS, cap, D), BF16),
        compiler_params=_cparams(("arbitrary", "arbitrary")),
        name="expert_ffn",
    )(idx, h2, gate, w1, w3, w2)


def _combine_kernel(rs_ref, x_ref, slot_ref, mod_ref, gf_ref, ye_hbm, o_ref, win, sem,
                    *, win_rows, cap, nt, tt, mrow, final):
    i = pl.program_id(0)

    def start_of(e, t):
        s0 = rs_ref[e, t]
        s_al = lax.shift_left(lax.shift_right_logical(s0, 4), 4)
        return pl.multiple_of(jnp.minimum(s_al, cap - win_rows), BF16_ROWS)

    def copy(e, t, sl):
        return pltpu.make_async_copy(ye_hbm.at[e, pl.ds(start_of(e, t), win_rows), :],
                                     win.at[sl, e], sem.at[sl])

    @pl.when(i == 0)
    def _():
        for e in range(N_EXPERTS):
            copy(e, i, 0).start()

    @pl.when(i + 1 < nt)
    def _():
        for e in range(N_EXPERTS):
            copy(e, i + 1, (i + 1) % 2).start()

    sl = i % 2
    for e in range(N_EXPERTS):
        copy(e, i, sl).wait()

    acc = jnp.zeros((tt, D), F32)
    wi = lax.broadcasted_iota(I32, (win_rows, tt), 0)
    for e in range(N_EXPERTS):
        rel = slot_ref[e:e + 1, :] - start_of(e, i)
        oh_t = jnp.where(wi == rel, 1.0, 0.0).astype(BF16)
        acc = acc + lax.dot_general(oh_t, win[sl, e], _TN, preferred_element_type=F32)
    gt2 = mod_ref[mrow:mrow + 1, 5 * D:6 * D]
    y = x_ref[...] + gt2 * acc
    if final:
        ms = jnp.mean(y * y, axis=-1, keepdims=True)
        y = y * lax.rsqrt(ms + EPS) * gf_ref[...]
    o_ref[...] = y


def _combine(rstart, x1, slot, mod, mrow, g_final, ye, tt, final):
    n = x1.shape[0]
    cap = ye.shape[1]
    nt = n // tt
    win_rows = min(cap, tt + BF16_ROWS)
    grid_spec = pltpu.PrefetchScalarGridSpec(
        num_scalar_prefetch=1,
        grid=(nt,),
        in_specs=[pl.BlockSpec((tt, D), lambda i, rs: (i, 0)),
                  pl.BlockSpec((N_EXPERTS, tt), lambda i, rs: (0, i)),
                  pl.BlockSpec((SUBLANES, N_MOD * D), lambda i, rs: (0, 0)),
                  pl.BlockSpec((1, D), lambda i, rs: (0, 0)),
                  pl.BlockSpec(memory_space=pl.ANY)],
        out_specs=pl.BlockSpec((tt, D), lambda i, rs: (i, 0)),
        scratch_shapes=[pltpu.VMEM((2, N_EXPERTS, win_rows, D), BF16),
                        pltpu.SemaphoreType.DMA((2,))])
    return pl.pallas_call(
        functools.partial(_combine_kernel, win_rows=win_rows, cap=cap, nt=nt, tt=tt, mrow=mrow, final=final),
        grid_spec=grid_spec,
        out_shape=jax.ShapeDtypeStruct((n, D), F32),
        compiler_params=_cparams(("arbitrary",)),
        name="moe_combine",
    )(rstart, x1, slot, mod, g_final, ye)


def _rope_tables(n_lat):
    t = jnp.arange(n_lat)
    inv = ROPE_THETA ** (-jnp.arange(0, ROPE_AXIS, 2, dtype=F32) / ROPE_AXIS)
    ang_r = (t // GRID_W).astype(F32)[:, None] * inv
    ang_c = (t % GRID_W).astype(F32)[:, None] * inv
    cr, sr, cc, sc = jnp.cos(ang_r), jnp.sin(ang_r), jnp.cos(ang_c), jnp.sin(ang_c)
    zero = jnp.zeros_like(sr)
    rep = lambda a: jnp.concatenate([a] * (LANES // DA_DH), axis=1)
    return (rep(jnp.concatenate([cr, cr, cc, cc], axis=1)),
            rep(jnp.concatenate([zero, sr, zero, sc], axis=1)),
            rep(jnp.concatenate([-sr, zero, -sc, zero], axis=1)))


def _largest_tile(n, unit, limit):
    best = unit
    for k in range(1, n // unit + 1):
        if n % (k * unit) == 0 and k * unit <= limit:
            best = k * unit
    return best


def _moe(x1, h2, aff_t, mod, mrow, w1, w3, w2, g_final, final):
    n = x1.shape[0]
    cap = max(1, EC_CAPACITY * n // N_EXPERTS)
    tile = LANES * LANES if n <= LANES * LANES else n
    n_pad = max(n, tile)
    if n_pad > n:
        aff_t = jnp.concatenate([aff_t, jnp.full((N_EXPERTS, n_pad - n), -1.0, F32)], axis=1)
    idx, gate, slot, rank = _route(aff_t.reshape(N_EXPERTS, n_pad // LANES, LANES), cap)
    tt = min(256, n)
    slot = slot.reshape(N_EXPERTS, n_pad)[:, :n]
    rstart = rank.reshape(N_EXPERTS, n_pad)[:, 0:n:tt]
    tc = min(cap, 1024)
    ye = _ffn(idx.reshape(N_EXPERTS, cap), gate, h2, w1, w3, w2, tc)
    return _combine(rstart, x1, slot, mod, mrow, g_final, ye, tt, final)


def kernel(x, c, ctx, c_ctx, w_ada, b_ada, g_norm1, g_norm2, w_in, na_rpb, da_lam_q1, da_lam_k1, da_lam_q2,
           da_lam_k2, da_subln_g, gm_ln_g, gm_ln_b, gm_w_s, gm_b_s, w_branch, w_out, w_router, w_e1, w_e3,
           w_e2, g_final):
    depth = w_ada.shape[0]
    n_lat, n_ctx = x.shape[1], ctx.shape[1]
    rows = n_lat // GRID_W
    xs, xc = x[0], ctx[0]

    cc = jnp.concatenate([c.reshape(1, D), c_ctx.reshape(1, D), jnp.zeros((SUBLANES - 2, D), F32)], axis=0)
    mods = _ada(cc, w_ada, b_ada)

    tabs_lat = _rope_tables(n_lat)
    tabs_ctx = (jnp.ones((n_ctx, LANES), F32), jnp.zeros((n_ctx, LANES), F32), jnp.zeros((n_ctx, LANES), F32))
    tm_proj = _largest_tile(n_lat, 256, 1024)
    tk = _largest_tile(n_lat, 256, 2048)
    tq = 256

    for i in range(depth):
        last = i == depth - 1
        lam_init = 0.8 - 0.6 * math.exp(-0.3 * i)
        mod = mods[i]
        w_in_b = w_in[i].astype(BF16)
        g1 = g_norm1[i].reshape(1, D)
        g2 = g_norm2[i].reshape(1, D)
        gf = g_final.reshape(1, D)
        lam_vecs = [v[i].reshape(1, DA_DH).astype(F32) for v in (da_lam_q1, da_lam_k1, da_lam_q2, da_lam_k2)]
        g_col = da_subln_g[i].reshape(DA_VD, 1)
        merge_w = (w_branch[i].astype(BF16), w_out[i].astype(BF16), gm_ln_g[i].reshape(1, BW),
                   gm_ln_b[i].reshape(1, BW), gm_w_s[i].astype(BF16), gm_b_s[i].T, g2, w_router[i].T)
        w1, w3, w2 = w_e1[i].astype(BF16), w_e3[i].astype(BF16), w_e2[i].astype(BF16)

        p_lat, qt_lat, vt_lat = _proj(xs, g1, mod, w_in_b, tabs_lat, 0, tm_proj)
        p_ctx, qt_ctx, vt_ctx = _proj(xc, g1, mod, w_in_b, tabs_ctx, 1, n_ctx)

        ya = _na(p_lat, p_ctx, _na_bias(na_rpb[i], rows))
        yb = _diff(qt_lat, p_ctx, vt_ctx, (p_lat, vt_lat), lam_vecs, g_col, lam_init, tq, tk)
        x1, h2, aff_t = _merge(xs, ya, yb, p_lat, mod, 0, *merge_w, tm=256)
        xs = _moe(x1, h2, aff_t, mod, 0, w1, w3, w2, gf, last)

        if not last:
            yac = _ctx_dense(p_ctx)
            ybc = _diff(qt_ctx, p_ctx, vt_ctx, None, lam_vecs, g_col, lam_init, n_ctx, n_ctx)
            x1c, h2c, aff_tc = _merge(xc, yac, ybc, p_ctx, mod, 1, *merge_w, tm=n_ctx)
            xc = _moe(x1c, h2c, aff_tc, mod, 1, w1, w3, w2, gf, False)
    return xs[None]
```

```python
import functools
import math

import numpy as np
import jax
import jax.numpy as jnp
from jax import lax
from jax.experimental import pallas as pl
from jax.experimental.pallas import tpu as pltpu

F32 = jnp.float32
BF16 = jnp.bfloat16
I32 = jnp.int32

D = 1024
GRID_W = 64
BW = 512
N_COLBLK = 14
NA_HEADS = 8
NA_WIN_R = 8
NA_WIN_C = 16
NA_ROWS = 4
NA_KROWS = 12
DA_HEADS = 4
DA_DH = 64
DA_VD = 128
GM_GROUPS = 4
GM_CHUNK = 128
N_EXPERTS = 16
EC_CAPACITY = 2
ROPE_THETA = 10000.0
ROPE_AXIS = DA_DH // 2
N_MOD = 6
EPS = 1e-6
LANES = 128
SUBLANES = 8
BF16_ROWS = 16
NEG_BIG = -1e30
VMEM_LIMIT = 56 * 1024 * 1024

CB_KA, CB_VA, CB_KB, CB_VB, CB_QA, CB_QB, CB_U, CB_V = range(8)

_NT = (((1,), (1,)), ((), ()))
_TN = (((0,), (0,)), ((), ()))


def _cparams(sem, flags=None):
    return pltpu.CompilerParams(dimension_semantics=sem, vmem_limit_bytes=VMEM_LIMIT, flags=flags)


def _gelu(x):
    return 0.5 * x * (1.0 + jnp.tanh(math.sqrt(2.0 / math.pi) * (x + 0.044715 * (x * x * x))))


def _ada_kernel(c_ref, w_ref, b_ref, o_ref):
    c = c_ref[...]
    s = c * jax.nn.sigmoid(c)
    o_ref[0] = jnp.dot(s, w_ref[0], preferred_element_type=F32,
                       precision=lax.Precision.HIGHEST) + b_ref[0]


def _ada(cc, w_ada, b_ada):
    depth = w_ada.shape[0]
    tn = 512
    return pl.pallas_call(
        _ada_kernel,
        grid=(depth, N_MOD * D // tn),
        in_specs=[pl.BlockSpec((SUBLANES, D), lambda i, j: (0, 0)),
                  pl.BlockSpec((1, D, tn), lambda i, j: (i, 0, j)),
                  pl.BlockSpec((1, 1, tn), lambda i, j: (i, 0, j))],
        out_specs=pl.BlockSpec((1, SUBLANES, tn), lambda i, j: (i, 0, j)),
        out_shape=jax.ShapeDtypeStruct((depth, SUBLANES, N_MOD * D), F32),
        compiler_params=_cparams(("arbitrary", "arbitrary")),
        name="ada_mod",
    )(cc, w_ada, b_ada.reshape(depth, 1, N_MOD * D))


def _proj_kernel(x_ref, g_ref, mod_ref, w_ref, c_ref, sp_ref, sm_ref, p_ref, qt_ref, vt_ref, h_scr, *, mrow):
    j = pl.program_id(1)

    @pl.when(j == 0)
    def _():
        x = x_ref[...]
        ms = jnp.mean(x * x, axis=-1, keepdims=True)
        xn = x * lax.rsqrt(ms + EPS) * g_ref[...]
        sh = mod_ref[mrow:mrow + 1, 0:D]
        sc = mod_ref[mrow:mrow + 1, D:2 * D]
        h_scr[...] = (xn * (1.0 + sc) + sh).astype(BF16)

    z = jnp.dot(h_scr[...], w_ref[...], preferred_element_type=F32)

    def rope(z):
        reps = BW // LANES
        c = jnp.concatenate([c_ref[...]] * reps, axis=1)
        sp = jnp.concatenate([sp_ref[...]] * reps, axis=1)
        sm = jnp.concatenate([sm_ref[...]] * reps, axis=1)
        half = ROPE_AXIS // 2
        return z * c + pltpu.roll(z, half, 1) * sp + pltpu.roll(z, BW - half, 1) * sm

    @pl.when(j == CB_KB)
    def _():
        p_ref[...] = rope(z).astype(BF16)

    @pl.when(j == CB_QB)
    def _():
        zr = rope(z)
        p_ref[...] = zr.astype(BF16)
        qt_ref[...] = zr.T.astype(BF16)

    @pl.when(j == CB_VB)
    def _():
        p_ref[...] = z.astype(BF16)
        vt_ref[...] = z.T.astype(BF16)

    @pl.when((j != CB_KB) & (j != CB_QB) & (j != CB_VB))
    def _():
        p_ref[...] = z.astype(BF16)


def _proj(x, g, mod, w, tabs, mrow, tm):
    n = x.shape[0]
    return pl.pallas_call(
        functools.partial(_proj_kernel, mrow=mrow),
        grid=(n // tm, N_COLBLK),
        in_specs=[pl.BlockSpec((tm, D), lambda i, j: (i, 0)),
                  pl.BlockSpec((1, D), lambda i, j: (0, 0)),
                  pl.BlockSpec((SUBLANES, N_MOD * D), lambda i, j: (0, 0)),
                  pl.BlockSpec((D, BW), lambda i, j: (0, j)),
                  pl.BlockSpec((tm, LANES), lambda i, j: (i, 0)),
                  pl.BlockSpec((tm, LANES), lambda i, j: (i, 0)),
                  pl.BlockSpec((tm, LANES), lambda i, j: (i, 0))],
        out_specs=[pl.BlockSpec((tm, BW), lambda i, j: (i, j)),
                   pl.BlockSpec((BW, tm), lambda i, j: (0, i)),
                   pl.BlockSpec((BW, tm), lambda i, j: (0, i))],
        out_shape=[jax.ShapeDtypeStruct((n, N_COLBLK * BW), BF16),
                   jax.ShapeDtypeStruct((BW, n), BF16),
                   jax.ShapeDtypeStruct((BW, n), BF16)],
        scratch_shapes=[pltpu.VMEM((tm, D), BF16)],
        compiler_params=_cparams(("arbitrary", "arbitrary")),
        name="proj",
    )(x, g, mod, w, *tabs)


def _na_kernel(q_ref, k0_ref, k1_ref, k2_ref, v0_ref, v1_ref, v2_ref, kc_ref, vc_ref, bias_ref, o_ref):
    nq = q_ref.shape[0]
    lane = lax.broadcasted_iota(I32, (nq, LANES), 1)
    k_refs = (kc_ref, k0_ref, k1_ref, k2_ref)
    v_refs = (vc_ref, v0_ref, v1_ref, v2_ref)
    for g in range(NA_HEADS // 2):
        sl = slice(g * LANES, (g + 1) * LANES)
        qp = q_ref[:, sl] * 0.125
        ks = [r[:, sl] for r in k_refs]
        vs = [r[:, sl] for r in v_refs]
        outs = []
        for sub in range(2):
            h = 2 * g + sub
            keep = (lane < 64) if sub == 0 else (lane >= 64)
            qz = jnp.where(keep, qp, jnp.zeros_like(qp))
            ss = [lax.dot_general(qz, ks[0], _NT, preferred_element_type=F32)]
            for t in range(3):
                s = lax.dot_general(qz, ks[1 + t], _NT, preferred_element_type=F32)
                ss.append(s + bias_ref[0, h, :, t * nq:(t + 1) * nq])
            m = ss[0].max(axis=-1, keepdims=True)
            for s in ss[1:]:
                m = jnp.maximum(m, s.max(axis=-1, keepdims=True))
            l = jnp.zeros_like(m)
            o = jnp.zeros((nq, LANES), F32)
            for s, v in zip(ss, vs):
                p = jnp.exp(s - m)
                l = l + p.sum(axis=-1, keepdims=True)
                o = o + jnp.dot(p.astype(BF16), v, preferred_element_type=F32)
            outs.append(o / l)
        o_ref[:, sl] = jnp.where(lane < 64, outs[0], outs[1]).astype(BF16)


def _na_bias(rpb, rows):
    nb = rows // NA_ROWS
    c = np.arange(GRID_W)[:, None]
    kc = np.arange(GRID_W)[None, :]
    c0 = np.clip(c - NA_WIN_C // 2, 0, GRID_W - NA_WIN_C)
    col_ok = (kc >= c0) & (kc < c0 + NA_WIN_C)
    dc = kc - c + (NA_WIN_C - 1)
    sel = np.stack([(dc == d) & col_ok for d in range(2 * NA_WIN_C - 1)]).astype(np.float32)
    toep = jnp.einsum("hrd,dck->hrck", rpb.astype(F32), sel, precision=lax.Precision.HIGHEST)
    toep = toep + np.where(col_ok, 0.0, NEG_BIG).astype(np.float32)
    masked = jnp.full((NA_HEADS, GRID_W, GRID_W), NEG_BIG, F32)
    variants = []
    for b in (0, 1, nb - 1):
        kb0 = min(max(b - 1, 0), nb - 3)
        q_rows = []
        for a in range(NA_ROWS):
            r = NA_ROWS * b + a
            r0 = min(max(r - NA_WIN_R // 2, 0), rows - NA_WIN_R)
            blocks = []
            for i in range(NA_KROWS):
                kr = NA_ROWS * kb0 + i
                blocks.append(toep[:, kr - r + NA_WIN_R - 1] if r0 <= kr < r0 + NA_WIN_R else masked)
            q_rows.append(jnp.concatenate(blocks, axis=2))
        variants.append(jnp.concatenate(q_rows, axis=1))
    return jnp.stack(variants)


def _na(p_lat, p_ctx, bias):
    nq = NA_ROWS * GRID_W
    n_lat, n_ctx = p_lat.shape[0], p_ctx.shape[0]
    nb = n_lat // nq

    def kmap(t, col):
        return lambda b: (jnp.clip(b - 1, 0, nb - 3) + t, col)

    def bmap(b):
        return (jnp.where(b == 0, 0, jnp.where(b == nb - 1, 2, 1)), 0, 0, 0)

    blk = lambda f: pl.BlockSpec((nq, BW), f)
    return pl.pallas_call(
        _na_kernel,
        grid=(nb,),
        in_specs=[blk(lambda b: (b, CB_QA)),
                  blk(kmap(0, CB_KA)), blk(kmap(1, CB_KA)), blk(kmap(2, CB_KA)),
                  blk(kmap(0, CB_VA)), blk(kmap(1, CB_VA)), blk(kmap(2, CB_VA)),
                  pl.BlockSpec((n_ctx, BW), lambda b: (0, CB_KA)),
                  pl.BlockSpec((n_ctx, BW), lambda b: (0, CB_VA)),
                  pl.BlockSpec((1, NA_HEADS, nq, NA_KROWS * GRID_W), bmap)],
        out_specs=pl.BlockSpec((nq, BW), lambda b: (b, 0)),
        out_shape=jax.ShapeDtypeStruct((n_lat, BW), BF16),
        compiler_params=_cparams(("arbitrary",)),
        name="na_attn",
    )(p_lat, p_lat, p_lat, p_lat, p_lat, p_lat, p_lat, p_ctx, p_ctx, bias)


def _ctx_dense_kernel(q_ref, k_ref, v_ref, o_ref):
    nq = q_ref.shape[0]
    lane = lax.broadcasted_iota(I32, (nq, LANES), 1)
    for g in range(NA_HEADS // 2):
        sl = slice(g * LANES, (g + 1) * LANES)
        qp = q_ref[:, sl] * 0.125
        kp = k_ref[:, sl]
        vp = v_ref[:, sl]
        outs = []
        for sub in range(2):
            keep = (lane < 64) if sub == 0 else (lane >= 64)
            qz = jnp.where(keep, qp, jnp.zeros_like(qp))
            s = lax.dot_general(qz, kp, _NT, preferred_element_type=F32)
            m = s.max(axis=-1, keepdims=True)
            p = jnp.exp(s - m)
            l = p.sum(axis=-1, keepdims=True)
            outs.append(jnp.dot(p.astype(BF16), vp, preferred_element_type=F32) / l)
        o_ref[:, sl] = jnp.where(lane < 64, outs[0], outs[1]).astype(BF16)


def _ctx_dense(p_ctx):
    n_ctx = p_ctx.shape[0]
    blk = lambda col: pl.BlockSpec((n_ctx, BW), lambda i: (0, col))
    return pl.pallas_call(
        _ctx_dense_kernel,
        grid=(1,),
        in_specs=[blk(CB_QA), blk(CB_KA), blk(CB_VA)],
        out_specs=pl.BlockSpec((n_ctx, BW), lambda i: (0, 0)),
        out_shape=jax.ShapeDtypeStruct((n_ctx, BW), BF16),
        compiler_params=_cparams(("arbitrary",)),
        name="ctx_dense_attn",
    )(p_ctx, p_ctx, p_ctx)


def _diff_kernel(*refs, tq, nk, lam_init, with_lat):
    if with_lat:
        (qt_ref, kc_ref, vtc_ref, k_ref, vt_ref, lq1_ref, lk1_ref, lq2_ref, lk2_ref, g_ref, o_ref,
         qz_scr, m_scr, l_scr, acc_scr, s_scr, p_scr) = refs
    else:
        (qt_ref, kc_ref, vtc_ref, lq1_ref, lk1_ref, lq2_ref, lk2_ref, g_ref, o_ref,
         qz_scr, m_scr, l_scr, acc_scr, s_scr, p_scr) = refs
    j = pl.program_id(1)
    lanes2 = 2 * tq
    rc = max(BF16_ROWS, BF16_ROWS * 1024 // lanes2)

    def scores(h, k_ref, buf):
        nkeys = k_ref.shape[0]
        s = jnp.dot(k_ref[:, h * DA_VD:(h + 1) * DA_VD], qz_scr[h], preferred_element_type=F32)
        s_scr[buf, 0:nkeys, :] = s
        return s.reshape(nkeys // rc, rc, lanes2).max(axis=0)

    def softmax(h, buf, nkeys, mx):
        m_old = m_scr[h]
        m_new = jnp.maximum(m_old, mx.max(axis=0, keepdims=True))
        alpha = jnp.exp(m_old - m_new)
        m_b = jnp.broadcast_to(m_new, (rc, lanes2))
        lacc = jnp.zeros((rc, lanes2), F32)
        for c in range(nkeys // rc):
            p = jnp.exp(s_scr[buf, c * rc:(c + 1) * rc, :] - m_b)
            lacc = lacc + p
            p_scr[buf, c * rc:(c + 1) * rc, :] = p.astype(BF16)
        l_scr[h] = alpha * l_scr[h] + lacc.sum(axis=0, keepdims=True)
        m_scr[h] = m_new
        return alpha

    def attend_all(k_ref, vt_ref):
        nkeys = k_ref.shape[0]
        mx = scores(0, k_ref, 0)
        for h in range(DA_HEADS):
            mx_next = scores(h + 1, k_ref, (h + 1) % 2) if h + 1 < DA_HEADS else None
            alpha = softmax(h, h % 2, nkeys, mx)
            mx = mx_next
            acc_scr[h] = alpha * acc_scr[h] + jnp.dot(vt_ref[h * DA_VD:(h + 1) * DA_VD, :],
                                                      p_scr[h % 2, 0:nkeys, :],
                                                      preferred_element_type=F32)

    @pl.when(j == 0)
    def _():
        row = lax.broadcasted_iota(I32, (DA_VD, tq), 0)
        for h in range(DA_HEADS):
            qh = qt_ref[h * DA_VD:(h + 1) * DA_VD, :] * 0.125
            zero = jnp.zeros_like(qh)
            qz_scr[h] = jnp.concatenate([jnp.where(row < DA_DH, qh, zero),
                                         jnp.where(row >= DA_DH, qh, zero)], axis=1)
        m_scr[...] = jnp.full(m_scr.shape, -jnp.inf, F32)
        l_scr[...] = jnp.zeros(l_scr.shape, F32)
        acc_scr[...] = jnp.zeros(acc_scr.shape, F32)
        attend_all(kc_ref, vtc_ref)

    if with_lat:
        attend_all(k_ref, vt_ref)

    @pl.when(j == nk - 1)
    def _():
        lam = (jnp.exp(jnp.sum(lq1_ref[...] * lk1_ref[...], keepdims=True))
               - jnp.exp(jnp.sum(lq2_ref[...] * lk2_ref[...], keepdims=True)) + lam_init)
        for h in range(DA_HEADS):
            o = acc_scr[h] / l_scr[h]
            od = o[:, :tq] - lam * o[:, tq:]
            ms = jnp.mean(od * od, axis=0, keepdims=True)
            y = od * lax.rsqrt(ms + EPS) * g_ref[...] * (1.0 - lam_init)
            o_ref[:, h * DA_VD:(h + 1) * DA_VD] = y.T.astype(BF16)


def _diff(qt, p_ctx, vt_ctx, lat, lam_vecs, g_col, lam_init, tq, tk):
    n_q, n_ctx = qt.shape[1], p_ctx.shape[0]
    with_lat = lat is not None
    nk = lat[0].shape[0] // tk if with_lat else 1
    vec = pl.BlockSpec((1, DA_DH), lambda i, j: (0, 0))
    in_specs = [pl.BlockSpec((BW, tq), lambda i, j: (0, i)),
                pl.BlockSpec((n_ctx, BW), lambda i, j: (0, CB_KB)),
                pl.BlockSpec((BW, n_ctx), lambda i, j: (0, 0))]
    args = [qt, p_ctx, vt_ctx]
    if with_lat:
        in_specs += [pl.BlockSpec((tk, BW), lambda i, j: (j, CB_KB)),
                     pl.BlockSpec((BW, tk), lambda i, j: (0, j))]
        args += list(lat)
    in_specs += [vec, vec, vec, vec, pl.BlockSpec((DA_VD, 1), lambda i, j: (0, 0))]
    return pl.pallas_call(
        functools.partial(_diff_kernel, tq=tq, nk=nk, lam_init=lam_init, with_lat=with_lat),
        grid=(n_q // tq, nk),
        in_specs=in_specs,
        out_specs=pl.BlockSpec((tq, BW), lambda i, j: (i, 0)),
        out_shape=jax.ShapeDtypeStruct((n_q, BW), BF16),
        scratch_shapes=[pltpu.VMEM((DA_HEADS, DA_VD, 2 * tq), BF16),
                        pltpu.VMEM((DA_HEADS, 1, 2 * tq), F32),
                        pltpu.VMEM((DA_HEADS, 1, 2 * tq), F32),
                        pltpu.VMEM((DA_HEADS, DA_VD, 2 * tq), F32),
                        pltpu.VMEM((2, max(tk, n_ctx), 2 * tq), F32),
                        pltpu.VMEM((2, max(tk, n_ctx), 2 * tq), BF16)],
        compiler_params=_cparams(("arbitrary", "arbitrary")),
        name="diff_attn",
    )(*args, *lam_vecs, g_col)


def _merge_kernel(x_ref, ya_ref, yb_ref, u_ref, v_ref, ga_ref, gb_ref, gc_ref, mod_ref, wb_ref, wo_ref,
                  lng_ref, lnb_ref, ws_ref, bst_ref, g2_ref, wrt_ref,
                  x1_ref, h2_ref, aff_ref, yc_scr, *, mrow, tm):
    ug = _gelu(u_ref[...].astype(F32))
    vg = _gelu(v_ref[...].astype(F32))
    mu = jnp.mean(vg, axis=-1, keepdims=True)
    var = jnp.mean(jnp.square(vg - mu), axis=-1, keepdims=True)
    vn = ((vg - mu) * lax.rsqrt(var + EPS) * lng_ref[...] + lnb_ref[...]).astype(BF16)
    for ci in range(tm // GM_CHUNK):
        rs = slice(ci * GM_CHUNK, (ci + 1) * GM_CHUNK)
        for g in range(GM_GROUPS):
            cs = slice(g * LANES, (g + 1) * LANES)
            mixed = jnp.dot(ws_ref[g], vn[rs, cs], preferred_element_type=F32) + bst_ref[:, g:g + 1]
            yc_scr[rs, cs] = (ug[rs, cs] * mixed).astype(BF16)

    sig = jax.nn.sigmoid
    m = sig(ga_ref[...].astype(F32)) * jnp.dot(ya_ref[...], wb_ref[0], preferred_element_type=F32)
    m = m + sig(gb_ref[...].astype(F32)) * jnp.dot(yb_ref[...], wb_ref[1], preferred_element_type=F32)
    m = m + sig(gc_ref[...].astype(F32)) * jnp.dot(yc_scr[...], wb_ref[2], preferred_element_type=F32)
    y = jnp.dot(m.astype(BF16), wo_ref[...], preferred_element_type=F32)
    gt1 = mod_ref[mrow:mrow + 1, 2 * D:3 * D]
    x1 = x_ref[...] + gt1 * y
    x1_ref[...] = x1

    ms = jnp.mean(x1 * x1, axis=-1, keepdims=True)
    sh2 = mod_ref[mrow:mrow + 1, 3 * D:4 * D]
    sc2 = mod_ref[mrow:mrow + 1, 4 * D:5 * D]
    h2 = x1 * lax.rsqrt(ms + EPS) * g2_ref[...] * (1.0 + sc2) + sh2
    for s in range(D // LANES):
        h2_ref[pl.ds(s, tm, stride=D // LANES), :] = h2[:, s * LANES:(s + 1) * LANES]

    hh = h2.astype(BF16)
    hl = (h2 - hh.astype(F32)).astype(BF16)
    w = wrt_ref[...]
    wh = w.astype(BF16)
    wl = (w - wh.astype(F32)).astype(BF16)
    lg = (lax.dot_general(wh, hh, _NT, preferred_element_type=F32)
          + lax.dot_general(wh, hl, _NT, preferred_element_type=F32)
          + lax.dot_general(wl, hh, _NT, preferred_element_type=F32))
    e = jnp.exp(lg - lg.max(axis=0, keepdims=True))
    aff_ref[...] = e / e.sum(axis=0, keepdims=True)


def _merge(x, ya, yb, p_all, mod, mrow, wb, wo, lng, lnb, ws, bst, g2, wrt, tm):
    n = x.shape[0]
    const = lambda shape: pl.BlockSpec(shape, lambda i: (0,) * len(shape))
    return pl.pallas_call(
        functools.partial(_merge_kernel, mrow=mrow, tm=tm),
        grid=(n // tm,),
        in_specs=[pl.BlockSpec((tm, D), lambda i: (i, 0)),
                  pl.BlockSpec((tm, BW), lambda i: (i, 0)),
                  pl.BlockSpec((tm, BW), lambda i: (i, 0)),
                  pl.BlockSpec((tm, BW), lambda i: (i,CB_U)),
                  pl.BlockSpec((tm, BW), lambda i: (i,CB_V)),
                  pl.BlockSpec((tm, D), lambda i: (i,4)),
                  pl.BlockSpec((tm, D), lambda i: (i,5)),
                  pl.BlockSpec((tm, D), lambda i: (i,6)),
                  const((SUBLANES, N_MOD * D)),
                  const((3, BW, D)), const((D, D)),
                  const((1, BW)), const((1, BW)),
                  const((GM_GROUPS, GM_CHUNK, GM_CHUNK)), const((GM_CHUNK, GM_GROUPS)),
                  const((1, D)), const((N_EXPERTS, D))],
        out_specs=[pl.BlockSpec((tm, D), lambda i: (i, 0)),
                   pl.BlockSpec((tm * (D // LANES), LANES), lambda i: (i, 0)),
                   pl.BlockSpec((N_EXPERTS, tm), lambda i: (0, i))],
        out_shape=[jax.ShapeDtypeStruct((n, D), F32),
                   jax.ShapeDtypeStruct((n * (D // LANES), LANES), F32),
                   jax.ShapeDtypeStruct((N_EXPERTS, n), F32)],
        scratch_shapes=[pltpu.VMEM((tm, BW), BF16)],
        compiler_params=_cparams(("arbitrary",)),
        name="merge_prenorm",
    )(x, ya, yb, p_all, p_all, p_all, p_all, p_all, mod, wb, wo, lng, lnb, ws, bst, g2, wrt)


def _route_kernel(a_ref, idx_ref, gate_ref, slot_ref, rank_ref, *, cap, nrow):
    a = a_ref[0]

    def bisect(k, prefix):
        cand = prefix | jnp.left_shift(jnp.int32(1), 30 - k)
        cand_f = pltpu.bitcast(jnp.full((nrow, LANES), cand, I32), F32)
        cnt = jnp.sum(jnp.where(a >= cand_f, 1.0, 0.0))
        return jnp.where(cnt >= cap, cand, prefix)

    thr_bits = lax.fori_loop(0, 31, bisect, jnp.int32(0))
    thr = pltpu.bitcast(jnp.full((nrow, LANES), thr_bits, I32), F32)
    gt = a > thr
    eq = a == thr

    r_i = lax.broadcasted_iota(I32, (LANES, LANES), 0)
    c_i = lax.broadcasted_iota(I32, (LANES, LANES), 1)
    upper = (r_i <= c_i).astype(BF16)
    rr = lax.broadcasted_iota(I32, (nrow, nrow), 0)
    rc = lax.broadcasted_iota(I32, (nrow, nrow), 1)
    lower_strict = (rc < rr).astype(BF16)
    upper_strict = (rr < rc).astype(BF16)

    def prefix(mask):
        xf = jnp.where(mask, 1.0, 0.0)
        incl = jnp.dot(xf.astype(BF16), upper, preferred_element_type=F32)
        tot = jnp.broadcast_to(incl[:, LANES - 1:LANES], (nrow, LANES))
        base = jnp.dot(lower_strict, tot.astype(BF16), preferred_element_type=F32)
        return xf, incl, base + incl - xf

    _, _, eq_rank = prefix(eq)
    need = cap - jnp.sum(jnp.where(gt, 1.0, 0.0))
    sel = gt | (eq & (eq_rank < need))
    xf, incl, rank = prefix(sel)
    rank_ref[0] = rank.astype(I32)
    slot_ref[0] = jnp.where(sel, rank, -1.0).astype(I32)

    ones = jnp.ones((SUBLANES, LANES), BF16)
    tot_l = lax.dot_general(ones, xf.astype(BF16), _NT, preferred_element_type=F32)
    off_l = jnp.dot(tot_l.astype(BF16), upper_strict, preferred_element_type=F32)[0:1]
    tot_l = tot_l[0:1]

    jf = lax.broadcasted_iota(I32, (cap, nrow), 0).astype(F32)
    oh_row = (off_l <= jf) & (jf < off_l + tot_l)
    ohb = jnp.where(oh_row, 1.0, 0.0).astype(BF16)
    row_id = lax.broadcasted_iota(I32, (cap, nrow), 1).astype(F32)
    off_j = jnp.sum(jnp.where(oh_row, off_l, 0.0), axis=1, keepdims=True)
    row_j = jnp.sum(jnp.where(oh_row, row_id, 0.0), axis=1, keepdims=True)
    key = jnp.where(sel, incl, 0.0).astype(BF16)
    g = jnp.dot(ohb, key, preferred_element_type=F32)
    target = lax.broadcasted_iota(I32, (cap, 1), 0).astype(F32) - off_j + 1.0
    oh_lane = g == target
    lane_id = lax.broadcasted_iota(I32, (cap, LANES), 1).astype(F32)
    lane_j = jnp.sum(jnp.where(oh_lane, lane_id, 0.0), axis=1, keepdims=True)
    idx_ref[0] = (row_j * LANES + lane_j).astype(I32)

    a1 = a.astype(BF16)
    r1 = a - a1.astype(F32)
    a2 = r1.astype(BF16)
    a3 = (r1 - a2.astype(F32)).astype(BF16)
    arow = (jnp.dot(ohb, a1, preferred_element_type=F32) + jnp.dot(ohb, a2, preferred_element_type=F32)
            + jnp.dot(ohb, a3, preferred_element_type=F32))
    gate_ref[0] = jnp.sum(jnp.where(oh_lane, arow, 0.0), axis=1, keepdims=True)


def _route(aff3, cap):
    nrow = aff3.shape[1]
    return pl.pallas_call(
        functools.partial(_route_kernel, cap=cap, nrow=nrow),
        grid=(N_EXPERTS,),
        in_specs=[pl.BlockSpec((1, nrow, LANES), lambda e: (e, 0, 0))],
        out_specs=[pl.BlockSpec((1, cap, 1), lambda e: (e, 0, 0)),
                   pl.BlockSpec((1, cap, 1), lambda e: (e, 0, 0)),
                   pl.BlockSpec((1, nrow, LANES), lambda e: (e, 0, 0)),
                   pl.BlockSpec((1, nrow, LANES), lambda e: (e, 0, 0))],
        out_shape=[jax.ShapeDtypeStruct((N_EXPERTS, cap, 1), I32),
                   jax.ShapeDtypeStruct((N_EXPERTS, cap, 1), F32),
                   jax.ShapeDtypeStruct((N_EXPERTS, nrow, LANES), I32),
                   jax.ShapeDtypeStruct((N_EXPERTS, nrow, LANES), I32)],
        compiler_params=_cparams(("arbitrary",)),
        name="ec_route",
    )(aff3)


def _ffn_kernel(idx_ref, h2_hbm, gate_ref, w1_ref, w3_ref, w2_ref, ye_ref, buf, sem, wb_scr, *, tc, nc):
    sub = D // LANES

    @pl.when(pl.program_id(1) == 0)
    def _():
        wb_scr[0] = w1_ref[0].astype(BF16)
        wb_scr[1] = w3_ref[0].astype(BF16)
        wb_scr[2] = w2_ref[0].astype(BF16)

    step = pl.program_id(0) * nc + pl.program_id(1)
    nsteps = N_EXPERTS * nc

    def row_copy(tok, r, slot):
        return pltpu.make_async_copy(h2_hbm.at[pl.ds(pl.multiple_of(tok * sub, sub), sub), :],
                                     buf.at[slot, pl.ds(pl.multiple_of(r * sub, sub), sub), :],
                                     sem.at[slot])

    def issue(s, slot):
        e = s // nc
        base = (s % nc) * tc

        def body(r, carry):
            row_copy(idx_ref[e, base + r], r, slot).start()
            return carry

        lax.fori_loop(0, tc, body, 0)

    @pl.when(step == 0)
    def _():
        issue(step, 0)

    @pl.when(step + 1 < nsteps)
    def _():
        issue(step + 1, (step + 1) % 2)

    slot = step % 2
    pltpu.make_async_copy(h2_hbm.at[pl.ds(0, tc * sub), :], buf.at[slot], sem.at[slot]).wait()
    xin = jnp.concatenate([buf[slot, pl.ds(s, tc, stride=sub), :] for s in range(sub)], axis=1).astype(BF16)
    a = jnp.dot(xin, wb_scr[0], preferred_element_type=F32)
    b = jnp.dot(xin, wb_scr[1], preferred_element_type=F32)
    hid = (a * jax.nn.sigmoid(a) * b).astype(BF16)
    y = jnp.dot(hid, wb_scr[2], preferred_element_type=F32)
    ye_ref[0] = (y * gate_ref[0]).astype(BF16)


def _ffn(idx, gate, h2, w1, w3, w2, tc):
    cap = idx.shape[1]
    nc = cap // tc
    sub = D // LANES
    grid_spec = pltpu.PrefetchScalarGridSpec(
        num_scalar_prefetch=1,
        grid=(N_EXPERTS, nc),
        in_specs=[pl.BlockSpec(memory_space=pl.ANY),
                  pl.BlockSpec((1, tc, 1), lambda e, c, idx: (e, c, 0)),
                  pl.BlockSpec((1, D, D), lambda e, c, idx: (e, 0, 0)),
                  pl.BlockSpec((1, D, D), lambda e, c, idx: (e, 0, 0)),
                  pl.BlockSpec((1, D, D), lambda e, c, idx: (e, 0, 0))],
        out_specs=pl.BlockSpec((1, tc, D), lambda e, c, idx: (e, c, 0)),
        scratch_shapes=[pltpu.VMEM((2, tc * sub, LANES), F32),
                        pltpu.SemaphoreType.DMA((2,)),
                        pltpu.VMEM((3, D, D), BF16)])
    return pl.pallas_call(
        functools.partial(_ffn_kernel, tc=tc, nc=nc),
        grid_spec=grid_spec,
        out_shape=jax.ShapeDtypeStruct((N_EXPERTS, cap, D), BF16),
        compiler_params=_cparams(("arbitrary", "arbitrary")),
        name="expert_ffn",
    )(idx, h2, gate, w1, w3, w2)


def _combine_kernel(rs_ref, x_ref, slot_ref, mod_ref, gf_ref, ye_hbm, o_ref, win, sem,
                    *, win_rows, cap, nt, tt, mrow, final):
    i = pl.program_id(0)

    def start_of(e, t):
        s0 = rs_ref[e, t]
        s_al = (s0 // BF16_ROWS) * BF16_ROWS
        return pl.multiple_of(jnp.minimum(s_al, cap - win_rows), BF16_ROWS)

    def copy(e, t, sl):
        return pltpu.make_async_copy(ye_hbm.at[e, pl.ds(start_of(e, t), win_rows), :],
                                     win.at[sl, e], sem.at[sl])

    @pl.when(i == 0)
    def _():
        for e in range(N_EXPERTS):
            copy(e, i, 0).start()

    @pl.when(i + 1 < nt)
    def _():
        for e in range(N_EXPERTS):
            copy(e, i + 1, (i + 1) % 2).start()

    sl = i % 2
    for e in range(N_EXPERTS):
        copy(e, i, sl).wait()

    acc = jnp.zeros((tt, D), F32)
    wi = lax.broadcasted_iota(I32, (win_rows, tt), 0)
    for e in range(N_EXPERTS):
        rel = slot_ref[e:e + 1, :] - start_of(e, i)
        oh_t = jnp.where(wi == rel, 1.0, 0.0).astype(BF16)
        acc = acc + lax.dot_general(oh_t, win[sl, e], _TN, preferred_element_type=F32)
    gt2 = mod_ref[mrow:mrow + 1, 5 * D:6 * D]
    y = x_ref[...] + gt2 * acc
    if final:
        ms = jnp.mean(y * y, axis=-1, keepdims=True)
        y = y * lax.rsqrt(ms + EPS) * gf_ref[...]
    o_ref[...] = y


def _combine(rstart, x1, slot, mod, mrow, g_final, ye, tt, final):
    n = x1.shape[0]
    cap = ye.shape[1]
    nt = n // tt
    win_rows = min(cap, tt + BF16_ROWS)
    grid_spec = pltpu.PrefetchScalarGridSpec(
        num_scalar_prefetch=1,
        grid=(nt,),
        in_specs=[pl.BlockSpec((tt, D), lambda i, rs: (i, 0)),
                  pl.BlockSpec((N_EXPERTS, tt), lambda i, rs: (0, i)),
                  pl.BlockSpec((SUBLANES, N_MOD * D), lambda i, rs: (0, 0)),
                  pl.BlockSpec((1, D), lambda i, rs: (0, 0)),
                  pl.BlockSpec(memory_space=pl.ANY)],
        out_specs=pl.BlockSpec((tt, D), lambda i, rs: (i, 0)),
        scratch_shapes=[pltpu.VMEM((2, N_EXPERTS, win_rows, D), BF16),
                        pltpu.SemaphoreType.DMA((2,))])
    return pl.pallas_call(
        functools.partial(_combine_kernel, win_rows=win_rows, cap=cap, nt=nt, tt=tt, mrow=mrow, final=final),
        grid_spec=grid_spec,
        out_shape=jax.ShapeDtypeStruct((n, D), F32),
        compiler_params=_cparams(("arbitrary",)),
        name="moe_combine",
    )(rstart, x1, slot, mod, g_final, ye)


def _rope_tables(n_lat):
    t = jnp.arange(n_lat)
    inv = ROPE_THETA ** (-jnp.arange(0, ROPE_AXIS, 2, dtype=F32) / ROPE_AXIS)
    ang_r = (t // GRID_W).astype(F32)[:, None] * inv
    ang_c = (t % GRID_W).astype(F32)[:, None] * inv
    cr, sr, cc, sc = jnp.cos(ang_r), jnp.sin(ang_r), jnp.cos(ang_c), jnp.sin(ang_c)
    zero = jnp.zeros_like(sr)
    rep = lambda a: jnp.concatenate([a] * (LANES // DA_DH), axis=1)
    return (rep(jnp.concatenate([cr, cr, cc, cc], axis=1)),
            rep(jnp.concatenate([zero, sr, zero, sc], axis=1)),
            rep(jnp.concatenate([-sr, zero, -sc, zero], axis=1)))


def _largest_tile(n, unit, limit):
    best = unit
    for k in range(1, n // unit + 1):
        if n % (k * unit) == 0 and k * unit <= limit:
            best = k * unit
    return best


def _moe(x1, h2, aff_t, mod, mrow, w1, w3, w2, g_final, final):
    n = x1.shape[0]
    cap = max(1, EC_CAPACITY * n // N_EXPERTS)
    tile = LANES * LANES if n <= LANES * LANES else n
    n_pad = max(n, tile)
    if n_pad > n:
        aff_t = jnp.concatenate([aff_t, jnp.full((N_EXPERTS, n_pad - n), -1.0, F32)], axis=1)
    idx, gate, slot, rank = _route(aff_t.reshape(N_EXPERTS, n_pad // LANES, LANES), cap)
    tt = min(256, n)
    slot = slot.reshape(N_EXPERTS, n_pad)[:, :n]
    rstart = rank.reshape(N_EXPERTS, n_pad)[:, 0:n:tt]
    tc = min(cap, 512)
    ye = _ffn(idx.reshape(N_EXPERTS, cap), gate, h2, w1, w3, w2, tc)
    return _combine(rstart, x1, slot, mod, mrow, g_final, ye, tt, final)


def kernel(x, c, ctx, c_ctx, w_ada, b_ada, g_norm1, g_norm2, w_in, na_rpb, da_lam_q1, da_lam_k1, da_lam_q2,
           da_lam_k2, da_subln_g, gm_ln_g, gm_ln_b, gm_w_s, gm_b_s, w_branch, w_out, w_router, w_e1, w_e3,
           w_e2, g_final):
    depth = w_ada.shape[0]
    n_lat, n_ctx = x.shape[1], ctx.shape[1]
    rows = n_lat // GRID_W
    xs, xc = x[0], ctx[0]

    cc = jnp.concatenate([c.reshape(1, D), c_ctx.reshape(1, D), jnp.zeros((SUBLANES - 2, D), F32)], axis=0)
    mods = _ada(cc, w_ada, b_ada)

    tabs_lat = _rope_tables(n_lat)
    tabs_ctx = (jnp.ones((n_ctx, LANES), F32), jnp.zeros((n_ctx, LANES), F32), jnp.zeros((n_ctx, LANES), F32))
    tm_proj = _largest_tile(n_lat, 256, 1024)
    tk = _largest_tile(n_lat, 256, 1024)
    tq = 512

    for i in range(depth):
        last = i == depth - 1
        lam_init = 0.8 - 0.6 * math.exp(-0.3 * i)
        mod = mods[i]
        w_in_b = w_in[i].astype(BF16)
        g1 = g_norm1[i].reshape(1, D)
        g2 = g_norm2[i].reshape(1, D)
        gf = g_final.reshape(1, D)
        lam_vecs = [v[i].reshape(1, DA_DH).astype(F32) for v in (da_lam_q1, da_lam_k1, da_lam_q2, da_lam_k2)]
        g_col = da_subln_g[i].reshape(DA_VD, 1)
        merge_w = (w_branch[i].astype(BF16), w_out[i].astype(BF16), gm_ln_g[i].reshape(1, BW),
                   gm_ln_b[i].reshape(1, BW), gm_w_s[i].astype(BF16), gm_b_s[i].T, g2, w_router[i].T)
        w1, w3, w2 = w_e1[i], w_e3[i], w_e2[i]

        p_lat, qt_lat, vt_lat = _proj(xs, g1, mod, w_in_b, tabs_lat, 0, tm_proj)
        p_ctx, qt_ctx, vt_ctx = _proj(xc, g1, mod, w_in_b, tabs_ctx, 1, n_ctx)

        ya = _na(p_lat, p_ctx, _na_bias(na_rpb[i], rows))
        yb = _diff(qt_lat, p_ctx, vt_ctx, (p_lat, vt_lat), lam_vecs, g_col, lam_init, tq, tk)
        x1, h2, aff_t = _merge(xs, ya, yb, p_lat, mod, 0, *merge_w, tm=256)
        xs = _moe(x1, h2, aff_t, mod, 0, w1, w3, w2, gf, last)

        if not last:
            yac = _ctx_dense(p_ctx)
            ybc = _diff(qt_ctx, p_ctx, vt_ctx, None, lam_vecs, g_col, lam_init, n_ctx, n_ctx)
            x1c, h2c, aff_tc = _merge(xc, yac, ybc, p_ctx, mod, 1, *merge_w, tm=n_ctx)
            xc = _moe(x1c, h2c, aff_tc, mod, 1, w1, w3, w2, gf, False)
    return xs[None]
```

```python
import functools
import math

import numpy as np
import jax
import jax.numpy as jnp
from jax import lax
from jax.experimental import pallas as pl
from jax.experimental.pallas import tpu as pltpu

F32 = jnp.float32
BF16 = jnp.bfloat16
I32 = jnp.int32

D = 1024
GRID_W = 64
BW = 512
N_COLBLK = 14
NA_HEADS = 8
NA_WIN_R = 8
NA_WIN_C = 16
NA_ROWS = 4
NA_KROWS = 12
DA_HEADS = 4
DA_DH = 64
DA_VD = 128
GM_GROUPS = 4
GM_CHUNK = 128
N_EXPERTS = 16
EC_CAPACITY = 2
ROPE_THETA = 10000.0
ROPE_AXIS = DA_DH // 2
N_MOD = 6
EPS = 1e-6
LANES = 128
SUBLANES = 8
BF16_ROWS = 16
NEG_BIG = -1e30
VT_ROWS = BW + BF16_ROWS
Q_SCALE_LOG2 = DA_DH ** -0.5 * math.log2(math.e)
VMEM_LIMIT = 56 * 1024 * 1024

CB_KA, CB_VA, CB_KB, CB_VB, CB_QA, CB_QB, CB_U, CB_V = range(8)

_NT = (((1,), (1,)), ((), ()))
_TN = (((0,), (0,)), ((), ()))


def _cparams(sem, flags=None):
    return pltpu.CompilerParams(dimension_semantics=sem, vmem_limit_bytes=VMEM_LIMIT, flags=flags)


def _gelu(x):
    return 0.5 * x * (1.0 + jnp.tanh(math.sqrt(2.0 / math.pi) * (x + 0.044715 * (x * x * x))))


def _ada_kernel(c_ref, w_ref, b_ref, o_ref):
    c = c_ref[...]
    s = c * jax.nn.sigmoid(c)
    o_ref[0] = jnp.dot(s, w_ref[0], preferred_element_type=F32,
                       precision=lax.Precision.HIGHEST) + b_ref[0]


def _ada(cc, w_ada, b_ada):
    depth = w_ada.shape[0]
    tn = 512
    return pl.pallas_call(
        _ada_kernel,
        grid=(depth, N_MOD * D // tn),
        in_specs=[pl.BlockSpec((SUBLANES, D), lambda i, j: (0, 0)),
                  pl.BlockSpec((1, D, tn), lambda i, j: (i, 0, j)),
                  pl.BlockSpec((1, 1, tn), lambda i, j: (i, 0, j))],
        out_specs=pl.BlockSpec((1, SUBLANES, tn), lambda i, j: (i, 0, j)),
        out_shape=jax.ShapeDtypeStruct((depth, SUBLANES, N_MOD * D), F32),
        compiler_params=_cparams(("arbitrary", "arbitrary")),
        name="ada_mod",
    )(cc, w_ada, b_ada.reshape(depth, 1, N_MOD * D))


def _proj_kernel(x_ref, g_ref, mod_ref, w_ref, c_ref, sp_ref, sm_ref, p_ref, qt_ref, vt_ref, h_scr, *, mrow):
    j = pl.program_id(1)

    @pl.when(j == 0)
    def _():
        x = x_ref[...]
        ms = jnp.mean(x * x, axis=-1, keepdims=True)
        xn = x * lax.rsqrt(ms + EPS) * g_ref[...]
        sh = mod_ref[mrow:mrow + 1, 0:D]
        sc = mod_ref[mrow:mrow + 1, D:2 * D]
        h_scr[...] = (xn * (1.0 + sc) + sh).astype(BF16)

    z = jnp.dot(h_scr[...], w_ref[...], preferred_element_type=F32)

    def rope(z):
        reps = BW // LANES
        c = jnp.concatenate([c_ref[...]] * reps, axis=1)
        sp = jnp.concatenate([sp_ref[...]] * reps, axis=1)
        sm = jnp.concatenate([sm_ref[...]] * reps, axis=1)
        half = ROPE_AXIS // 2
        return z * c + pltpu.roll(z, half, 1) * sp + pltpu.roll(z, BW - half, 1) * sm

    @pl.when(j == CB_KB)
    def _():
        p_ref[...] = rope(z).astype(BF16)

    @pl.when(j == CB_QB)
    def _():
        zr = rope(z)
        p_ref[...] = zr.astype(BF16)
        qt_ref[...] = (zr * Q_SCALE_LOG2).T.astype(BF16)

    @pl.when(j == CB_VB)
    def _():
        p_ref[...] = z.astype(BF16)
        vt_ref[0:BW, :] = z.T.astype(BF16)
        vt_ref[BW:VT_ROWS, :] = jnp.ones((VT_ROWS - BW, z.shape[0]), BF16)

    @pl.when((j != CB_KB) & (j != CB_QB) & (j != CB_VB))
    def _():
        p_ref[...] = z.astype(BF16)


def _proj(x, g, mod, w, tabs, mrow, tm):
    n = x.shape[0]
    return pl.pallas_call(
        functools.partial(_proj_kernel, mrow=mrow),
        grid=(n // tm, N_COLBLK),
        in_specs=[pl.BlockSpec((tm, D), lambda i, j: (i, 0)),
                  pl.BlockSpec((1, D), lambda i, j: (0, 0)),
                  pl.BlockSpec((SUBLANES, N_MOD * D), lambda i, j: (0, 0)),
                  pl.BlockSpec((D, BW), lambda i, j: (0, j)),
                  pl.BlockSpec((tm, LANES), lambda i, j: (i, 0)),
                  pl.BlockSpec((tm, LANES), lambda i, j: (i, 0)),
                  pl.BlockSpec((tm, LANES), lambda i, j: (i, 0))],
        out_specs=[pl.BlockSpec((tm, BW), lambda i, j: (i, j)),
                   pl.BlockSpec((BW, tm), lambda i, j: (0, i)),
                   pl.BlockSpec((VT_ROWS, tm), lambda i, j: (0, i))],
        out_shape=[jax.ShapeDtypeStruct((n, N_COLBLK * BW), BF16),
                   jax.ShapeDtypeStruct((BW, n), BF16),
                   jax.ShapeDtypeStruct((VT_ROWS, n), BF16)],
        scratch_shapes=[pltpu.VMEM((tm, D), BF16)],
        compiler_params=_cparams(("arbitrary", "arbitrary")),
        name="proj",
    )(x, g, mod, w, *tabs)


def _na_kernel(q_ref, k0_ref, k1_ref, k2_ref, v0_ref, v1_ref, v2_ref, kc_ref, vc_ref, bias_ref, o_ref):
    nq = q_ref.shape[0]
    lane = lax.broadcasted_iota(I32, (nq, LANES), 1)
    k_refs = (kc_ref, k0_ref, k1_ref, k2_ref)
    v_refs = (vc_ref, v0_ref, v1_ref, v2_ref)
    for g in range(NA_HEADS // 2):
        sl = slice(g * LANES, (g + 1) * LANES)
        qp = q_ref[:, sl] * 0.125
        ks = [r[:, sl] for r in k_refs]
        vs = [r[:, sl] for r in v_refs]
        outs = []
        for sub in range(2):
            h = 2 * g + sub
            keep = (lane < 64) if sub == 0 else (lane >= 64)
            qz = jnp.where(keep, qp, jnp.zeros_like(qp))
            ss = [lax.dot_general(qz, ks[0], _NT, preferred_element_type=F32)]
            for t in range(3):
                s = lax.dot_general(qz, ks[1 + t], _NT, preferred_element_type=F32)
                ss.append(s + bias_ref[0, h, :, t * nq:(t + 1) * nq])
            m = ss[0].max(axis=-1, keepdims=True)
            for s in ss[1:]:
                m = jnp.maximum(m, s.max(axis=-1, keepdims=True))
            l = jnp.zeros_like(m)
            o = jnp.zeros((nq, LANES), F32)
            for s, v in zip(ss, vs):
                p = jnp.exp(s - m)
                l = l + p.sum(axis=-1, keepdims=True)
                o = o + jnp.dot(p.astype(BF16), v, preferred_element_type=F32)
            outs.append(o / l)
        o_ref[:, sl] = jnp.where(lane < 64, outs[0], outs[1]).astype(BF16)


def _na_bias(rpb, rows):
    nb = rows // NA_ROWS
    c = np.arange(GRID_W)[:, None]
    kc = np.arange(GRID_W)[None, :]
    c0 = np.clip(c - NA_WIN_C // 2, 0, GRID_W - NA_WIN_C)
    col_ok = (kc >= c0) & (kc < c0 + NA_WIN_C)
    dc = kc - c + (NA_WIN_C - 1)
    sel = np.stack([(dc == d) & col_ok for d in range(2 * NA_WIN_C - 1)]).astype(np.float32)
    toep = jnp.einsum("hrd,dck->hrck", rpb.astype(F32), sel, precision=lax.Precision.HIGHEST)
    toep = toep + np.where(col_ok, 0.0, NEG_BIG).astype(np.float32)
    masked = jnp.full((NA_HEADS, GRID_W, GRID_W), NEG_BIG, F32)
    variants = []
    for b in (0, 1, nb - 1):
        kb0 = min(max(b - 1, 0), nb - 3)
        q_rows = []
        for a in range(NA_ROWS):
            r = NA_ROWS * b + a
            r0 = min(max(r - NA_WIN_R // 2, 0), rows - NA_WIN_R)
            blocks = []
            for i in range(NA_KROWS):
                kr = NA_ROWS * kb0 + i
                blocks.append(toep[:, kr - r + NA_WIN_R - 1] if r0 <= kr < r0 + NA_WIN_R else masked)
            q_rows.append(jnp.concatenate(blocks, axis=2))
        variants.append(jnp.concatenate(q_rows, axis=1))
    return jnp.stack(variants)


def _na(p_lat, p_ctx, bias):
    nq = NA_ROWS * GRID_W
    n_lat, n_ctx = p_lat.shape[0], p_ctx.shape[0]
    nb = n_lat // nq

    def kmap(t, col):
        return lambda b: (jnp.clip(b - 1, 0, nb - 3) + t, col)

    def bmap(b):
        return (jnp.where(b == 0, 0, jnp.where(b == nb - 1, 2, 1)), 0, 0, 0)

    blk = lambda f: pl.BlockSpec((nq, BW), f)
    return pl.pallas_call(
        _na_kernel,
        grid=(nb,),
        in_specs=[blk(lambda b: (b, CB_QA)),
                  blk(kmap(0, CB_KA)), blk(kmap(1, CB_KA)), blk(kmap(2, CB_KA)),
                  blk(kmap(0, CB_VA)), blk(kmap(1, CB_VA)), blk(kmap(2, CB_VA)),
                  pl.BlockSpec((n_ctx, BW), lambda b: (0, CB_KA)),
                  pl.BlockSpec((n_ctx, BW), lambda b: (0, CB_VA)),
                  pl.BlockSpec((1, NA_HEADS, nq, NA_KROWS * GRID_W), bmap)],
        out_specs=pl.BlockSpec((nq, BW), lambda b: (b, 0)),
        out_shape=jax.ShapeDtypeStruct((n_lat, BW), BF16),
        compiler_params=_cparams(("arbitrary",)),
        name="na_attn",
    )(p_lat, p_lat, p_lat, p_lat, p_lat, p_lat, p_lat, p_ctx, p_ctx, bias)


def _ctx_dense_kernel(q_ref, k_ref, v_ref, o_ref):
    nq = q_ref.shape[0]
    lane = lax.broadcasted_iota(I32, (nq, LANES), 1)
    for g in range(NA_HEADS // 2):
        sl = slice(g * LANES, (g + 1) * LANES)
        qp = q_ref[:, sl] * 0.125
        kp = k_ref[:, sl]
        vp = v_ref[:, sl]
        outs = []
        for sub in range(2):
            keep = (lane < 64) if sub == 0 else (lane >= 64)
            qz = jnp.where(keep, qp, jnp.zeros_like(qp))
            s = lax.dot_general(qz, kp, _NT, preferred_element_type=F32)
            m = s.max(axis=-1, keepdims=True)
            p = jnp.exp(s - m)
            l = p.sum(axis=-1, keepdims=True)
            outs.append(jnp.dot(p.astype(BF16), vp, preferred_element_type=F32) / l)
        o_ref[:, sl] = jnp.where(lane < 64, outs[0], outs[1]).astype(BF16)


def _ctx_dense(p_ctx):
    n_ctx = p_ctx.shape[0]
    blk = lambda col: pl.BlockSpec((n_ctx, BW), lambda i: (0, col))
    return pl.pallas_call(
        _ctx_dense_kernel,
        grid=(1,),
        in_specs=[blk(CB_QA), blk(CB_KA), blk(CB_VA)],
        out_specs=pl.BlockSpec((n_ctx, BW), lambda i: (0, 0)),
        out_shape=jax.ShapeDtypeStruct((n_ctx, BW), BF16),
        compiler_params=_cparams(("arbitrary",)),
        name="ctx_dense_attn",
    )(p_ctx, p_ctx, p_ctx)


def _diff_kernel(*refs, tq, nk, lam_init, with_lat):
    if with_lat:
        (qt_ref, kc_ref, vtc_ref, k_ref, vt_ref, lq1_ref, lk1_ref, lq2_ref, lk2_ref, g_ref, o_ref,
         qz_scr, m_scr, acc_scr, s_scr, p_scr) = refs
    else:
        (qt_ref, kc_ref, vtc_ref, lq1_ref, lk1_ref, lq2_ref, lk2_ref, g_ref, o_ref,
         qz_scr, m_scr, acc_scr, s_scr, p_scr) = refs
    j = pl.program_id(1)
    lanes2 = 2 * tq
    rc = max(BF16_ROWS, BF16_ROWS * 1024 // lanes2)
    ck = 256

    def scores_slice(h, k_ref, buf, i, mx):
        rows = slice(i * ck, (i + 1) * ck)
        s = jnp.dot(k_ref[rows, h * DA_VD:(h + 1) * DA_VD], qz_scr[h], preferred_element_type=F32)
        s_scr[buf, rows, :] = s
        part = s.reshape(ck // rc, rc, lanes2).max(axis=0)
        return part if mx is None else jnp.maximum(mx, part)

    def exp_slice(buf, i, m_b):
        for c in range(i * ck // rc, (i + 1) * ck // rc):
            rows = slice(c * rc, (c + 1) * rc)
            p_scr[buf, rows, :] = jnp.exp2(s_scr[buf, rows, :] - m_b).astype(BF16)

    def pv_slice(h, vt_ref, buf, i0, i1, alpha):
        keys = slice(i0 * ck, i1 * ck)
        vt_ext = jnp.concatenate([vt_ref[h * DA_VD:(h + 1) * DA_VD, keys], vt_ref[BW:VT_ROWS, keys]], axis=0)
        part = jnp.dot(vt_ext, p_scr[buf, keys, :], preferred_element_type=F32)
        acc_scr[h] = (acc_scr[h] if alpha is None else alpha * acc_scr[h]) + part

    def attend_all(k_ref, vt_ref):
        ns = k_ref.shape[0] // ck
        cuts = sorted({0, ns // 2, ns})
        mx = None
        for i in range(ns):
            mx = scores_slice(0, k_ref, 0, i, mx)
        alpha_prev = None
        for h in range(DA_HEADS + 1):
            buf = h % 2
            if h < DA_HEADS:
                m_old = m_scr[h]
                m_new = jnp.maximum(m_old, mx.max(axis=0, keepdims=True))
                alpha = jnp.exp2(m_old - m_new)
                m_scr[h] = m_new
                m_b = jnp.broadcast_to(m_new, (rc, lanes2))
            mx = None
            for i in range(ns):
                if h + 1 < DA_HEADS:
                    mx = scores_slice(h + 1, k_ref, 1 - buf, i, mx)
                if h >= 1 and (i + 1) in cuts:
                    i0 = cuts[cuts.index(i + 1) - 1]
                    pv_slice(h - 1, vt_ref, 1 - buf, i0, i + 1, alpha_prev if i0 == 0 else None)
                if h < DA_HEADS:
                    exp_slice(buf, i, m_b)
            alpha_prev = alpha

    @pl.when(j == 0)
    def _():
        row = lax.broadcasted_iota(I32, (DA_VD, tq), 0)
        for h in range(DA_HEADS):
            qh = qt_ref[h * DA_VD:(h + 1) * DA_VD, :]
            zero = jnp.zeros_like(qh)
            qz_scr[h] = jnp.concatenate([jnp.where(row < DA_DH, qh, zero),
                                         jnp.where(row >= DA_DH, qh, zero)], axis=1)
        m_scr[...] = jnp.full(m_scr.shape, -jnp.inf, F32)
        acc_scr[...] = jnp.zeros(acc_scr.shape, F32)
        attend_all(kc_ref, vtc_ref)

    if with_lat:
        attend_all(k_ref, vt_ref)

    @pl.when(j == nk - 1)
    def _():
        lam = (jnp.exp(jnp.sum(lq1_ref[...] * lk1_ref[...], keepdims=True))
               - jnp.exp(jnp.sum(lq2_ref[...] * lk2_ref[...], keepdims=True)) + lam_init)
        for h in range(DA_HEADS):
            o = acc_scr[h, 0:DA_VD, :] / acc_scr[h, DA_VD:DA_VD + 1, :]
            od = o[:, :tq] - lam * o[:, tq:]
            ms = jnp.mean(od * od, axis=0, keepdims=True)
            y = od * lax.rsqrt(ms + EPS) * g_ref[...] * (1.0 - lam_init)
            o_ref[:, h * DA_VD:(h + 1) * DA_VD] = y.T.astype(BF16)


def _diff(qt, p_ctx, vt_ctx, lat, lam_vecs, g_col, lam_init, tq, tk):
    n_q, n_ctx = qt.shape[1], p_ctx.shape[0]
    with_lat = lat is not None
    nk = lat[0].shape[0] // tk if with_lat else 1
    vec = pl.BlockSpec((1, DA_DH), lambda i, j: (0, 0))
    in_specs = [pl.BlockSpec((BW, tq), lambda i, j: (0, i)),
                pl.BlockSpec((n_ctx, BW), lambda i, j: (0, CB_KB)),
                pl.BlockSpec((VT_ROWS, n_ctx), lambda i, j: (0, 0))]
    args = [qt, p_ctx, vt_ctx]
    if with_lat:
        in_specs += [pl.BlockSpec((tk, BW), lambda i, j: (j, CB_KB)),
                     pl.BlockSpec((VT_ROWS, tk), lambda i, j: (0, j))]
        args += list(lat)
    in_specs += [vec, vec, vec, vec, pl.BlockSpec((DA_VD, 1), lambda i, j: (0, 0))]
    return pl.pallas_call(
        functools.partial(_diff_kernel, tq=tq, nk=nk, lam_init=lam_init, with_lat=with_lat),
        grid=(n_q // tq, nk),
        in_specs=in_specs,
        out_specs=pl.BlockSpec((tq, BW), lambda i, j: (i, 0)),
        out_shape=jax.ShapeDtypeStruct((n_q, BW), BF16),
        scratch_shapes=[pltpu.VMEM((DA_HEADS, DA_VD, 2 * tq), BF16),
                        pltpu.VMEM((DA_HEADS, 1, 2 * tq), F32),
                        pltpu.VMEM((DA_HEADS, DA_VD + BF16_ROWS, 2 * tq), F32),
                        pltpu.VMEM((2, max(tk, n_ctx), 2 * tq), F32),
                        pltpu.VMEM((2, max(tk, n_ctx), 2 * tq), BF16)],
        compiler_params=_cparams(("arbitrary", "arbitrary")),
        name="diff_attn",
    )(*args, *lam_vecs, g_col)


def _merge_kernel(x_ref, ya_ref, yb_ref, u_ref, v_ref, ga_ref, gb_ref, gc_ref, mod_ref, wb_ref, wo_ref,
                  lng_ref, lnb_ref, ws_ref, bst_ref, g2_ref, wrt_ref,
                  x1_ref, h2_ref, aff_ref, yc_scr, *, mrow, tm):
    ug = _gelu(u_ref[...].astype(F32))
    vg = _gelu(v_ref[...].astype(F32))
    mu = jnp.mean(vg, axis=-1, keepdims=True)
    var = jnp.mean(jnp.square(vg - mu), axis=-1, keepdims=True)
    vn = ((vg - mu) * lax.rsqrt(var + EPS) * lng_ref[...] + lnb_ref[...]).astype(BF16)
    for ci in range(tm // GM_CHUNK):
        rs = slice(ci * GM_CHUNK, (ci + 1) * GM_CHUNK)
        for g in range(GM_GROUPS):
            cs = slice(g * LANES, (g + 1) * LANES)
            mixed = jnp.dot(ws_ref[g], vn[rs, cs], preferred_element_type=F32) + bst_ref[:, g:g + 1]
            yc_scr[rs, cs] = (ug[rs, cs] * mixed).astype(BF16)

    sig = jax.nn.sigmoid
    m = sig(ga_ref[...].astype(F32)) * jnp.dot(ya_ref[...], wb_ref[0], preferred_element_type=F32)
    m = m + sig(gb_ref[...].astype(F32)) * jnp.dot(yb_ref[...], wb_ref[1], preferred_element_type=F32)
    m = m + sig(gc_ref[...].astype(F32)) * jnp.dot(yc_scr[...], wb_ref[2], preferred_element_type=F32)
    y = jnp.dot(m.astype(BF16), wo_ref[...], preferred_element_type=F32)
    gt1 = mod_ref[mrow:mrow + 1, 2 * D:3 * D]
    x1 = x_ref[...] + gt1 * y
    x1_ref[...] = x1

    ms = jnp.mean(x1 * x1, axis=-1, keepdims=True)
    sh2 = mod_ref[mrow:mrow + 1, 3 * D:4 * D]
    sc2 = mod_ref[mrow:mrow + 1, 4 * D:5 * D]
    h2 = x1 * lax.rsqrt(ms + EPS) * g2_ref[...] * (1.0 + sc2) + sh2
    for s in range(D // LANES):
        h2_ref[pl.ds(s, tm, stride=D // LANES), :] = h2[:, s * LANES:(s + 1) * LANES]

    hh = h2.astype(BF16)
    hl = (h2 - hh.astype(F32)).astype(BF16)
    w = wrt_ref[...]
    wh = w.astype(BF16)
    wl = (w - wh.astype(F32)).astype(BF16)
    lg = (lax.dot_general(wh, hh, _NT, preferred_element_type=F32)
          + lax.dot_general(wh, hl, _NT, preferred_element_type=F32)
          + lax.dot_general(wl, hh, _NT, preferred_element_type=F32))
    e = jnp.exp(lg - lg.max(axis=0, keepdims=True))
    aff_ref[...] = e / e.sum(axis=0, keepdims=True)


def _merge(x, ya, yb, p_all, mod, mrow, wb, wo, lng, lnb, ws, bst, g2, wrt, tm):
    n = x.shape[0]
    const = lambda shape: pl.BlockSpec(shape, lambda i: (0,) * len(shape))
    return pl.pallas_call(
        functools.partial(_merge_kernel, mrow=mrow, tm=tm),
        grid=(n // tm,),
        in_specs=[pl.BlockSpec((tm, D), lambda i: (i, 0)),
                  pl.BlockSpec((tm, BW), lambda i: (i, 0)),
                  pl.BlockSpec((tm, BW), lambda i: (i, 0)),
                  pl.BlockSpec((tm, BW), lambda i: (i,CB_U)),
                  pl.BlockSpec((tm, BW), lambda i: (i,CB_V)),
                  pl.BlockSpec((tm, D), lambda i: (i,4)),
                  pl.BlockSpec((tm, D), lambda i: (i,5)),
                  pl.BlockSpec((tm, D), lambda i: (i,6)),
                  const((SUBLANES, N_MOD * D)),
                  const((3, BW, D)), const((D, D)),
                  const((1, BW)), const((1, BW)),
                  const((GM_GROUPS, GM_CHUNK, GM_CHUNK)), const((GM_CHUNK, GM_GROUPS)),
                  const((1, D)), const((N_EXPERTS, D))],
        out_specs=[pl.BlockSpec((tm, D), lambda i: (i, 0)),
                   pl.BlockSpec((tm * (D // LANES), LANES), lambda i: (i, 0)),
                   pl.BlockSpec((N_EXPERTS, tm), lambda i: (0, i))],
        out_shape=[jax.ShapeDtypeStruct((n, D), F32),
                   jax.ShapeDtypeStruct((n * (D // LANES), LANES), F32),
                   jax.ShapeDtypeStruct((N_EXPERTS, n), F32)],
        scratch_shapes=[pltpu.VMEM((tm, BW), BF16)],
        compiler_params=_cparams(("arbitrary",)),
        name="merge_prenorm",
    )(x, ya, yb, p_all, p_all, p_all, p_all, p_all, mod, wb, wo, lng, lnb, ws, bst, g2, wrt)


def _route_kernel(a_ref, idx_ref, gate_ref, slot_ref, rank_ref, *, cap, nrow):
    a = a_ref[0]

    def bisect(k, prefix):
        cand = prefix | jnp.left_shift(jnp.int32(1), 30 - k)
        cand_f = pltpu.bitcast(jnp.full((nrow, LANES), cand, I32), F32)
        cnt = jnp.sum(jnp.where(a >= cand_f, 1.0, 0.0))
        return jnp.where(cnt >= cap, cand, prefix)

    thr_bits = lax.fori_loop(0, 31, bisect, jnp.int32(0))
    thr = pltpu.bitcast(jnp.full((nrow, LANES), thr_bits, I32), F32)
    gt = a > thr
    eq = a == thr

    r_i = lax.broadcasted_iota(I32, (LANES, LANES), 0)
    c_i = lax.broadcasted_iota(I32, (LANES, LANES), 1)
    upper = (r_i <= c_i).astype(BF16)
    rr = lax.broadcasted_iota(I32, (nrow, nrow), 0)
    rc = lax.broadcasted_iota(I32, (nrow, nrow), 1)
    lower_strict = (rc < rr).astype(BF16)
    upper_strict = (rr < rc).astype(BF16)

    def prefix(mask):
        xf = jnp.where(mask, 1.0, 0.0)
        incl = jnp.dot(xf.astype(BF16), upper, preferred_element_type=F32)
        tot = jnp.broadcast_to(incl[:, LANES - 1:LANES], (nrow, LANES))
        base = jnp.dot(lower_strict, tot.astype(BF16), preferred_element_type=F32)
        return xf, incl, base + incl - xf

    _, _, eq_rank = prefix(eq)
    need = cap - jnp.sum(jnp.where(gt, 1.0, 0.0))
    sel = gt | (eq & (eq_rank < need))
    xf, incl, rank = prefix(sel)
    rank_ref[0] = rank.astype(I32)
    slot_ref[0] = jnp.where(sel, rank, -1.0).astype(I32)

    ones = jnp.ones((SUBLANES, LANES), BF16)
    tot_l = lax.dot_general(ones, xf.astype(BF16), _NT, preferred_element_type=F32)
    off_l = jnp.dot(tot_l.astype(BF16), upper_strict, preferred_element_type=F32)[0:1]
    tot_l = tot_l[0:1]

    jf = lax.broadcasted_iota(I32, (cap, nrow), 0).astype(F32)
    oh_row = (off_l <= jf) & (jf < off_l + tot_l)
    ohb = jnp.where(oh_row, 1.0, 0.0).astype(BF16)
    row_id = lax.broadcasted_iota(I32, (cap, nrow), 1).astype(F32)
    off_j = jnp.sum(jnp.where(oh_row, off_l, 0.0), axis=1, keepdims=True)
    row_j = jnp.sum(jnp.where(oh_row, row_id, 0.0), axis=1, keepdims=True)
    key = jnp.where(sel, incl, 0.0).astype(BF16)
    g = jnp.dot(ohb, key, preferred_element_type=F32)
    target = lax.broadcasted_iota(I32, (cap, 1), 0).astype(F32) - off_j + 1.0
    oh_lane = g == target
    lane_id = lax.broadcasted_iota(I32, (cap, LANES), 1).astype(F32)
    lane_j = jnp.sum(jnp.where(oh_lane, lane_id, 0.0), axis=1, keepdims=True)
    idx_ref[0] = (row_j * LANES + lane_j).astype(I32)

    a1 = a.astype(BF16)
    r1 = a - a1.astype(F32)
    a2 = r1.astype(BF16)
    a3 = (r1 - a2.astype(F32)).astype(BF16)
    arow = (jnp.dot(ohb, a1, preferred_element_type=F32) + jnp.dot(ohb, a2, preferred_element_type=F32)
            + jnp.dot(ohb, a3, preferred_element_type=F32))
    gate_ref[0] = jnp.sum(jnp.where(oh_lane, arow, 0.0), axis=1, keepdims=True)


def _route(aff3, cap):
    nrow = aff3.shape[1]
    return pl.pallas_call(
        functools.partial(_route_kernel, cap=cap, nrow=nrow),
        grid=(N_EXPERTS,),
        in_specs=[pl.BlockSpec((1, nrow, LANES), lambda e: (e, 0, 0))],
        out_specs=[pl.BlockSpec((1, cap, 1), lambda e: (e, 0, 0)),
                   pl.BlockSpec((1, cap, 1), lambda e: (e, 0, 0)),
                   pl.BlockSpec((1, nrow, LANES), lambda e: (e, 0, 0)),
                   pl.BlockSpec((1, nrow, LANES), lambda e: (e, 0, 0))],
        out_shape=[jax.ShapeDtypeStruct((N_EXPERTS, cap, 1), I32),
                   jax.ShapeDtypeStruct((N_EXPERTS, cap, 1), F32),
                   jax.ShapeDtypeStruct((N_EXPERTS, nrow, LANES), I32),
                   jax.ShapeDtypeStruct((N_EXPERTS, nrow, LANES), I32)],
        compiler_params=_cparams(("arbitrary",)),
        name="ec_route",
    )(aff3)


def _ffn_kernel(idx_ref, h2_hbm, gate_ref, w1_ref, w3_ref, w2_ref, ye_ref, buf, sem, wb_scr, *, tc, nc):
    sub = D // LANES

    @pl.when(pl.program_id(1) == 0)
    def _():
        wb_scr[0] = w1_ref[0, 0].astype(BF16)
        wb_scr[1] = w3_ref[0, 0].astype(BF16)
        wb_scr[2] = w2_ref[0, 0].astype(BF16)

    step = pl.program_id(0) * nc + pl.program_id(1)
    nsteps = N_EXPERTS * nc

    def row_copy(tok, r, slot):
        return pltpu.make_async_copy(h2_hbm.at[pl.ds(pl.multiple_of(tok * sub, sub), sub), :],
                                     buf.at[slot, pl.ds(pl.multiple_of(r * sub, sub), sub), :],
                                     sem.at[slot])

    def wait_slot(slot):
        pltpu.make_async_copy(h2_hbm.at[pl.ds(0, tc * sub), :], buf.at[slot], sem.at[slot]).wait()

    @pl.when(step == 0)
    def _():
        def body(r, carry):
            row_copy(idx_ref[0, r], r, 0).start()
            return carry

        lax.fori_loop(0, tc, body, 0, unroll=16)

    nxt = jnp.minimum(step + 1, nsteps - 1)
    nxt_e = nxt // nc
    nxt_base = (nxt % nc) * tc
    nslot = (step + 1) % 2
    parts = 4 if tc % 4 == 0 else 1

    def issue_part(q):
        for r in range(q * tc // parts, (q + 1) * tc // parts):
            row_copy(idx_ref[nxt_e, nxt_base + r], r, nslot).start()

    slot = step % 2
    wait_slot(slot)
    xin = jnp.concatenate([buf[slot, pl.ds(s, tc, stride=sub), :] for s in range(sub)], axis=1).astype(BF16)
    issue_part(0)
    a = jnp.dot(xin, wb_scr[0], preferred_element_type=F32)
    if parts == 4:
        issue_part(1)
    b = jnp.dot(xin, wb_scr[1], preferred_element_type=F32)
    hid = (a * jax.nn.sigmoid(a) * b).astype(BF16)
    if parts == 4:
        issue_part(2)
    y = jnp.dot(hid, wb_scr[2], preferred_element_type=F32)
    if parts == 4:
        issue_part(3)
    ye_ref[0] = (y * gate_ref[0]).astype(BF16)

    @pl.when(step == nsteps - 1)
    def _():
        wait_slot(nslot)


def _ffn(idx, gate, h2, w1, w3, w2, layer, tc):
    cap = idx.shape[1]
    nc = cap // tc
    sub = D // LANES
    wspec = pl.BlockSpec((1, 1, D, D), lambda e, c, idx: (layer, e, 0, 0))
    grid_spec = pltpu.PrefetchScalarGridSpec(
        num_scalar_prefetch=1,
        grid=(N_EXPERTS, nc),
        in_specs=[pl.BlockSpec(memory_space=pl.ANY),
                  pl.BlockSpec((1, tc, 1), lambda e, c, idx: (e, c, 0)),
                  wspec, wspec, wspec],
        out_specs=pl.BlockSpec((1, tc, D), lambda e, c, idx: (e, c, 0)),
        scratch_shapes=[pltpu.VMEM((2, tc * sub, LANES), F32),
                        pltpu.SemaphoreType.DMA((2,)),
                        pltpu.VMEM((3, D, D), BF16)])
    return pl.pallas_call(
        functools.partial(_ffn_kernel, tc=tc, nc=nc),
        grid_spec=grid_spec,
        out_shape=jax.ShapeDtypeStruct((N_EXPERTS, cap, D), BF16),
        compiler_params=_cparams(("arbitrary", "arbitrary")),
        name="expert_ffn",
    )(idx, h2, gate, w1, w3, w2)


def _combine_kernel(rs_ref, x_ref, slot_ref, mod_ref, gf_ref, ye_hbm, o_ref, win, sem,
                    *, win_rows, cap, nt, tt, mrow, final):
    i = pl.program_id(0)

    def start_of(e, t):
        s0 = rs_ref[e, t]
        s_al = (s0 // BF16_ROWS) * BF16_ROWS
        return pl.multiple_of(jnp.minimum(s_al, cap - win_rows), BF16_ROWS)

    def copy(e, t, sl):
        return pltpu.make_async_copy(ye_hbm.at[e, pl.ds(start_of(e, t), win_rows), :],
                                     win.at[sl, e], sem.at[sl])

    @pl.when(i == 0)
    def _():
        for e in range(N_EXPERTS):
            copy(e, i, 0).start()

    @pl.when(i + 1 < nt)
    def _():
        for e in range(N_EXPERTS):
            copy(e, i + 1, (i + 1) % 2).start()

    sl = i % 2
    for e in range(N_EXPERTS):
        copy(e, i, sl).wait()

    acc = jnp.zeros((tt, D), F32)
    wi = lax.broadcasted_iota(I32, (win_rows, tt), 0)
    for e in range(N_EXPERTS):
        rel = slot_ref[e:e + 1, :] - start_of(e, i)
        oh_t = jnp.where(wi == rel, 1.0, 0.0).astype(BF16)
        acc = acc + lax.dot_general(oh_t, win[sl, e], _TN, preferred_element_type=F32)
    gt2 = mod_ref[mrow:mrow + 1, 5 * D:6 * D]
    y = x_ref[...] + gt2 * acc
    if final:
        ms = jnp.mean(y * y, axis=-1, keepdims=True)
        y = y * lax.rsqrt(ms + EPS) * gf_ref[...]
    o_ref[...] = y


def _combine(rstart, x1, slot, mod, mrow, g_final, ye, tt, final):
    n = x1.shape[0]
    cap = ye.shape[1]
    nt = n // tt
    win_rows = min(cap, tt + BF16_ROWS)
    grid_spec = pltpu.PrefetchScalarGridSpec(
        num_scalar_prefetch=1,
        grid=(nt,),
        in_specs=[pl.BlockSpec((tt, D), lambda i, rs: (i, 0)),
                  pl.BlockSpec((N_EXPERTS, tt), lambda i, rs: (0, i)),
                  pl.BlockSpec((SUBLANES, N_MOD * D), lambda i, rs: (0, 0)),
                  pl.BlockSpec((1, D), lambda i, rs: (0, 0)),
                  pl.BlockSpec(memory_space=pl.ANY)],
        out_specs=pl.BlockSpec((tt, D), lambda i, rs: (i, 0)),
        scratch_shapes=[pltpu.VMEM((2, N_EXPERTS, win_rows, D), BF16),
                        pltpu.SemaphoreType.DMA((2,))])
    return pl.pallas_call(
        functools.partial(_combine_kernel, win_rows=win_rows, cap=cap, nt=nt, tt=tt, mrow=mrow, final=final),
        grid_spec=grid_spec,
        out_shape=jax.ShapeDtypeStruct((n, D), F32),
        compiler_params=_cparams(("arbitrary",)),
        name="moe_combine",
    )(rstart, x1, slot, mod, g_final, ye)


def _rope_tables(n_lat):
    t = jnp.arange(n_lat)
    inv = ROPE_THETA ** (-jnp.arange(0, ROPE_AXIS, 2, dtype=F32) / ROPE_AXIS)
    ang_r = (t // GRID_W).astype(F32)[:, None] * inv
    ang_c = (t % GRID_W).astype(F32)[:, None] * inv
    cr, sr, cc, sc = jnp.cos(ang_r), jnp.sin(ang_r), jnp.cos(ang_c), jnp.sin(ang_c)
    zero = jnp.zeros_like(sr)
    rep = lambda a: jnp.concatenate([a] * (LANES // DA_DH), axis=1)
    return (rep(jnp.concatenate([cr, cr, cc, cc], axis=1)),
            rep(jnp.concatenate([zero, sr, zero, sc], axis=1)),
            rep(jnp.concatenate([-sr, zero, -sc, zero], axis=1)))


def _largest_tile(n, unit, limit):
    best = unit
    for k in range(1, n // unit + 1):
        if n % (k * unit) == 0 and k * unit <= limit:
            best = k * unit
    return best


def _moe(x1, h2, aff_t, mod, mrow, w1, w3, w2, layer, g_final, final):
    n = x1.shape[0]
    cap = max(1, EC_CAPACITY * n // N_EXPERTS)
    tile = LANES * LANES if n <= LANES * LANES else n
    n_pad = max(n, tile)
    if n_pad > n:
        aff_t = jnp.concatenate([aff_t, jnp.full((N_EXPERTS, n_pad - n), -1.0, F32)], axis=1)
    idx, gate, slot, rank = _route(aff_t.reshape(N_EXPERTS, n_pad // LANES, LANES), cap)
    tt = min(256, n)
    slot = slot.reshape(N_EXPERTS, n_pad)[:, :n]
    rstart = rank.reshape(N_EXPERTS, n_pad)[:, 0:n:tt]
    tc = min(cap, 512)
    ye = _ffn(idx.reshape(N_EXPERTS, cap), gate, h2, w1, w3, w2, layer, tc)
    return _combine(rstart, x1, slot, mod, mrow, g_final, ye, tt, final)


def kernel(x, c, ctx, c_ctx, w_ada, b_ada, g_norm1, g_norm2, w_in, na_rpb, da_lam_q1, da_lam_k1, da_lam_q2,
           da_lam_k2, da_subln_g, gm_ln_g, gm_ln_b, gm_w_s, gm_b_s, w_branch, w_out, w_router, w_e1, w_e3,
           w_e2, g_final):
    depth = w_ada.shape[0]
    n_lat, n_ctx = x.shape[1], ctx.shape[1]
    rows = n_lat // GRID_W
    xs, xc = x[0], ctx[0]

    cc = jnp.concatenate([c.reshape(1, D), c_ctx.reshape(1, D), jnp.zeros((SUBLANES - 2, D), F32)], axis=0)
    mods = _ada(cc, w_ada, b_ada)

    tabs_lat = _rope_tables(n_lat)
    tabs_ctx = (jnp.ones((n_ctx, LANES), F32), jnp.zeros((n_ctx, LANES), F32), jnp.zeros((n_ctx, LANES), F32))
    tm_proj = _largest_tile(n_lat, 256, 1024)
    tk = _largest_tile(n_lat, 256, 1024)
    tq = 512

    for i in range(depth):
        last = i == depth - 1
        lam_init = 0.8 - 0.6 * math.exp(-0.3 * i)
        mod = mods[i]
        w_in_b = w_in[i].astype(BF16)
        g1 = g_norm1[i].reshape(1, D)
        g2 = g_norm2[i].reshape(1, D)
        gf = g_final.reshape(1, D)
        lam_vecs = [v[i].reshape(1, DA_DH).astype(F32) for v in (da_lam_q1, da_lam_k1, da_lam_q2, da_lam_k2)]
        g_col = da_subln_g[i].reshape(DA_VD, 1)
        merge_w = (w_branch[i].astype(BF16), w_out[i].astype(BF16), gm_ln_g[i].reshape(1, BW),
                   gm_ln_b[i].reshape(1, BW), gm_w_s[i].astype(BF16), gm_b_s[i].T, g2, w_router[i].T)

        p_lat, qt_lat, vt_lat = _proj(xs, g1, mod, w_in_b, tabs_lat, 0, tm_proj)
        p_ctx, qt_ctx, vt_ctx = _proj(xc, g1, mod, w_in_b, tabs_ctx, 1, n_ctx)

        ya = _na(p_lat, p_ctx, _na_bias(na_rpb[i], rows))
        yb = _diff(qt_lat, p_ctx, vt_ctx, (p_lat, vt_lat), lam_vecs, g_col, lam_init, tq, tk)
        x1, h2, aff_t = _merge(xs, ya, yb, p_lat, mod, 0, *merge_w, tm=256)
        xs = _moe(x1, h2, aff_t, mod, 0, w_e1, w_e3, w_e2, i, gf, last)

        if not last:
            yac = _ctx_dense(p_ctx)
            ybc = _diff(qt_ctx, p_ctx, vt_ctx, None, lam_vecs, g_col, lam_init, n_ctx, n_ctx)
            x1c, h2c, aff_tc = _merge(xc, yac, ybc, p_ctx, mod, 1, *merge_w, tm=n_ctx)
            xc = _moe(x1c, h2c, aff_tc, mod, 1, w_e1, w_e3, w_e2, i, gf, False)
    return xs[None]
```

```python
import functools
import math

import numpy as np
import jax
import jax.numpy as jnp
from jax import lax
from jax.experimental import pallas as pl
from jax.experimental.pallas import tpu as pltpu

F32 = jnp.float32
BF16 = jnp.bfloat16
I32 = jnp.int32

D = 1024
GRID_W = 64
BW = 512
N_COLBLK = 14
NA_HEADS = 8
NA_WIN_R = 8
NA_WIN_C = 16
NA_ROWS = 4
NA_KROWS = 12
DA_HEADS = 4
DA_DH = 64
DA_VD = 128
GM_GROUPS = 4
GM_CHUNK = 128
N_EXPERTS = 16
EC_CAPACITY = 2
ROPE_THETA = 10000.0
ROPE_AXIS = DA_DH // 2
N_MOD = 6
EPS = 1e-6
LANES = 128
SUBLANES = 8
BF16_ROWS = 16
NEG_BIG = -1e30
VT_ROWS = BW + BF16_ROWS
Q_SCALE_LOG2 = DA_DH ** -0.5 * math.log2(math.e)
VMEM_LIMIT = 56 * 1024 * 1024

CB_KA, CB_VA, CB_KB, CB_VB, CB_QA, CB_QB, CB_U, CB_V = range(8)

_NT = (((1,), (1,)), ((), ()))
_TN = (((0,), (0,)), ((), ()))


def _cparams(sem, flags=None):
    return pltpu.CompilerParams(dimension_semantics=sem, vmem_limit_bytes=VMEM_LIMIT, flags=flags)


def _gelu(x):
    return 0.5 * x * (1.0 + jnp.tanh(math.sqrt(2.0 / math.pi) * (x + 0.044715 * (x * x * x))))


def _ada_kernel(c_ref, w_ref, b_ref, o_ref):
    c = c_ref[...]
    s = c * jax.nn.sigmoid(c)
    o_ref[0] = jnp.dot(s, w_ref[0], preferred_element_type=F32,
                       precision=lax.Precision.HIGHEST) + b_ref[0]


def _ada(cc, w_ada, b_ada):
    depth = w_ada.shape[0]
    tn = 512
    return pl.pallas_call(
        _ada_kernel,
        grid=(depth, N_MOD * D // tn),
        in_specs=[pl.BlockSpec((SUBLANES, D), lambda i, j: (0, 0)),
                  pl.BlockSpec((1, D, tn), lambda i, j: (i, 0, j)),
                  pl.BlockSpec((1, 1, tn), lambda i, j: (i, 0, j))],
        out_specs=pl.BlockSpec((1, SUBLANES, tn), lambda i, j: (i, 0, j)),
        out_shape=jax.ShapeDtypeStruct((depth, SUBLANES, N_MOD * D), F32),
        compiler_params=_cparams(("arbitrary", "arbitrary")),
        name="ada_mod",
    )(cc, w_ada, b_ada.reshape(depth, 1, N_MOD * D))


def _proj_kernel(x_ref, g_ref, mod_ref, w_ref, c_ref, sp_ref, sm_ref, p_ref, qt_ref, vt_ref, h_scr, *, mrow):
    j = pl.program_id(1)

    @pl.when(j == 0)
    def _():
        x = x_ref[...]
        ms = jnp.mean(x * x, axis=-1, keepdims=True)
        xn = x * lax.rsqrt(ms + EPS) * g_ref[...]
        sh = mod_ref[mrow:mrow + 1, 0:D]
        sc = mod_ref[mrow:mrow + 1, D:2 * D]
        h_scr[...] = (xn * (1.0 + sc) + sh).astype(BF16)

    z = jnp.dot(h_scr[...], w_ref[...], preferred_element_type=F32)

    def rope(z):
        reps = BW // LANES
        c = jnp.concatenate([c_ref[...]] * reps, axis=1)
        sp = jnp.concatenate([sp_ref[...]] * reps, axis=1)
        sm = jnp.concatenate([sm_ref[...]] * reps, axis=1)
        half = ROPE_AXIS // 2
        return z * c + pltpu.roll(z, half, 1) * sp + pltpu.roll(z, BW - half, 1) * sm

    @pl.when(j == CB_KB)
    def _():
        p_ref[...] = rope(z).astype(BF16)

    @pl.when(j == CB_QB)
    def _():
        zr = rope(z)
        p_ref[...] = zr.astype(BF16)
        qt_ref[...] = (zr * Q_SCALE_LOG2).T.astype(BF16)

    @pl.when(j == CB_VB)
    def _():
        p_ref[...] = z.astype(BF16)
        vt_ref[0:BW, :] = z.T.astype(BF16)
        vt_ref[BW:VT_ROWS, :] = jnp.ones((VT_ROWS - BW, z.shape[0]), BF16)

    @pl.when((j != CB_KB) & (j != CB_QB) & (j != CB_VB))
    def _():
        p_ref[...] = z.astype(BF16)


def _proj(x, g, mod, w, tabs, mrow, tm):
    n = x.shape[0]
    return pl.pallas_call(
        functools.partial(_proj_kernel, mrow=mrow),
        grid=(n // tm, N_COLBLK),
        in_specs=[pl.BlockSpec((tm, D), lambda i, j: (i, 0)),
                  pl.BlockSpec((1, D), lambda i, j: (0, 0)),
                  pl.BlockSpec((SUBLANES, N_MOD * D), lambda i, j: (0, 0)),
                  pl.BlockSpec((D, BW), lambda i, j: (0, j)),
                  pl.BlockSpec((tm, LANES), lambda i, j: (i, 0)),
                  pl.BlockSpec((tm, LANES), lambda i, j: (i, 0)),
                  pl.BlockSpec((tm, LANES), lambda i, j: (i, 0))],
        out_specs=[pl.BlockSpec((tm, BW), lambda i, j: (i, j)),
                   pl.BlockSpec((BW, tm), lambda i, j: (0, i)),
                   pl.BlockSpec((VT_ROWS, tm), lambda i, j: (0, i))],
        out_shape=[jax.ShapeDtypeStruct((n, N_COLBLK * BW), BF16),
                   jax.ShapeDtypeStruct((BW, n), BF16),
                   jax.ShapeDtypeStruct((VT_ROWS, n), BF16)],
        scratch_shapes=[pltpu.VMEM((tm, D), BF16)],
        compiler_params=_cparams(("arbitrary", "arbitrary")),
        name="proj",
    )(x, g, mod, w, *tabs)


def _na_kernel(q_ref, k0_ref, k1_ref, k2_ref, v0_ref, v1_ref, v2_ref, kc_ref, vc_ref, bias_ref, o_ref):
    nq = q_ref.shape[0]
    lane = lax.broadcasted_iota(I32, (nq, LANES), 1)
    k_refs = (kc_ref, k0_ref, k1_ref, k2_ref)
    v_refs = (vc_ref, v0_ref, v1_ref, v2_ref)
    for g in range(NA_HEADS // 2):
        sl = slice(g * LANES, (g + 1) * LANES)
        qp = q_ref[:, sl] * 0.125
        ks = [r[:, sl] for r in k_refs]
        vs = [r[:, sl] for r in v_refs]
        outs = []
        for sub in range(2):
            h = 2 * g + sub
            keep = (lane < 64) if sub == 0 else (lane >= 64)
            qz = jnp.where(keep, qp, jnp.zeros_like(qp))
            ss = [lax.dot_general(qz, ks[0], _NT, preferred_element_type=F32)]
            for t in range(3):
                s = lax.dot_general(qz, ks[1 + t], _NT, preferred_element_type=F32)
                ss.append(s + bias_ref[0, h, :, t * nq:(t + 1) * nq])
            m = ss[0].max(axis=-1, keepdims=True)
            for s in ss[1:]:
                m = jnp.maximum(m, s.max(axis=-1, keepdims=True))
            l = jnp.zeros_like(m)
            o = jnp.zeros((nq, LANES), F32)
            for s, v in zip(ss, vs):
                p = jnp.exp(s - m)
                l = l + p.sum(axis=-1, keepdims=True)
                o = o + jnp.dot(p.astype(BF16), v, preferred_element_type=F32)
            outs.append(o / l)
        o_ref[:, sl] = jnp.where(lane < 64, outs[0], outs[1]).astype(BF16)


def _na_bias(rpb, rows):
    nb = rows // NA_ROWS
    c = np.arange(GRID_W)[:, None]
    kc = np.arange(GRID_W)[None, :]
    c0 = np.clip(c - NA_WIN_C // 2, 0, GRID_W - NA_WIN_C)
    col_ok = (kc >= c0) & (kc < c0 + NA_WIN_C)
    dc = kc - c + (NA_WIN_C - 1)
    sel = np.stack([(dc == d) & col_ok for d in range(2 * NA_WIN_C - 1)]).astype(np.float32)
    toep = jnp.einsum("hrd,dck->hrck", rpb.astype(F32), sel, precision=lax.Precision.HIGHEST)
    toep = toep + np.where(col_ok, 0.0, NEG_BIG).astype(np.float32)
    masked = jnp.full((NA_HEADS, GRID_W, GRID_W), NEG_BIG, F32)
    variants = []
    for b in (0, 1, nb - 1):
        kb0 = min(max(b - 1, 0), nb - 3)
        q_rows = []
        for a in range(NA_ROWS):
            r = NA_ROWS * b + a
            r0 = min(max(r - NA_WIN_R // 2, 0), rows - NA_WIN_R)
            blocks = []
            for i in range(NA_KROWS):
                kr = NA_ROWS * kb0 + i
                blocks.append(toep[:, kr - r + NA_WIN_R - 1] if r0 <= kr < r0 + NA_WIN_R else masked)
            q_rows.append(jnp.concatenate(blocks, axis=2))
        variants.append(jnp.concatenate(q_rows, axis=1))
    return jnp.stack(variants)


def _na(p_lat, p_ctx, bias):
    nq = NA_ROWS * GRID_W
    n_lat, n_ctx = p_lat.shape[0], p_ctx.shape[0]
    nb = n_lat // nq

    def kmap(t, col):
        return lambda b: (jnp.clip(b - 1, 0, nb - 3) + t, col)

    def bmap(b):
        return (jnp.where(b == 0, 0, jnp.where(b == nb - 1, 2, 1)), 0, 0, 0)

    blk = lambda f: pl.BlockSpec((nq, BW), f)
    return pl.pallas_call(
        _na_kernel,
        grid=(nb,),
        in_specs=[blk(lambda b: (b, CB_QA)),
                  blk(kmap(0, CB_KA)), blk(kmap(1, CB_KA)), blk(kmap(2, CB_KA)),
                  blk(kmap(0, CB_VA)), blk(kmap(1, CB_VA)), blk(kmap(2, CB_VA)),
                  pl.BlockSpec((n_ctx, BW), lambda b: (0, CB_KA)),
                  pl.BlockSpec((n_ctx, BW), lambda b: (0, CB_VA)),
                  pl.BlockSpec((1, NA_HEADS, nq, NA_KROWS * GRID_W), bmap)],
        out_specs=pl.BlockSpec((nq, BW), lambda b: (b, 0)),
        out_shape=jax.ShapeDtypeStruct((n_lat, BW), BF16),
        compiler_params=_cparams(("arbitrary",)),
        name="na_attn",
    )(p_lat, p_lat, p_lat, p_lat, p_lat, p_lat, p_lat, p_ctx, p_ctx, bias)


def _ctx_dense_kernel(q_ref, k_ref, v_ref, o_ref):
    nq = q_ref.shape[0]
    lane = lax.broadcasted_iota(I32, (nq, LANES), 1)
    for g in range(NA_HEADS // 2):
        sl = slice(g * LANES, (g + 1) * LANES)
        qp = q_ref[:, sl] * 0.125
        kp = k_ref[:, sl]
        vp = v_ref[:, sl]
        outs = []
        for sub in range(2):
            keep = (lane < 64) if sub == 0 else (lane >= 64)
            qz = jnp.where(keep, qp, jnp.zeros_like(qp))
            s = lax.dot_general(qz, kp, _NT, preferred_element_type=F32)
            m = s.max(axis=-1, keepdims=True)
            p = jnp.exp(s - m)
            l = p.sum(axis=-1, keepdims=True)
            outs.append(jnp.dot(p.astype(BF16), vp, preferred_element_type=F32) / l)
        o_ref[:, sl] = jnp.where(lane < 64, outs[0], outs[1]).astype(BF16)


def _ctx_dense(p_ctx):
    n_ctx = p_ctx.shape[0]
    blk = lambda col: pl.BlockSpec((n_ctx, BW), lambda i: (0, col))
    return pl.pallas_call(
        _ctx_dense_kernel,
        grid=(1,),
        in_specs=[blk(CB_QA), blk(CB_KA), blk(CB_VA)],
        out_specs=pl.BlockSpec((n_ctx, BW), lambda i: (0, 0)),
        out_shape=jax.ShapeDtypeStruct((n_ctx, BW), BF16),
        compiler_params=_cparams(("arbitrary",)),
        name="ctx_dense_attn",
    )(p_ctx, p_ctx, p_ctx)


def _diff_kernel(*refs, tq, nk, lam_init, with_lat):
    if with_lat:
        (qt_ref, kc_ref, vtc_ref, k_ref, vt_ref, lq1_ref, lk1_ref, lq2_ref, lk2_ref, g_ref, o_ref,
         qz_scr, m_scr, acc_scr, s0_scr, s1_scr, p0_scr, p1_scr) = refs
    else:
        (qt_ref, kc_ref, vtc_ref, lq1_ref, lk1_ref, lq2_ref, lk2_ref, g_ref, o_ref,
         qz_scr, m_scr, acc_scr, s0_scr, s1_scr, p0_scr, p1_scr) = refs
    s_scr = (s0_scr, s1_scr)
    p_scr = (p0_scr, p1_scr)
    j = pl.program_id(1)
    lanes2 = 2 * tq
    rc = max(BF16_ROWS, BF16_ROWS * 1024 // lanes2)
    ck = 256

    def attend_all(k_ref, vt_ref):
        ns = k_ref.shape[0] // ck

        def head_cols(h):
            return slice(h * DA_VD, (h + 1) * DA_VD)

        pv_every = 1

        def run_round(h_qk, h_exp, h_pv, m_b):
            mx = None
            for i in range(ns):
                rows = slice(i * ck, (i + 1) * ck)
                if h_qk is not None:
                    s = jnp.dot(k_ref[rows, head_cols(h_qk)], qz_scr[h_qk], preferred_element_type=F32)
                    s_scr[h_qk % 2][rows, :] = s
                    part = s.reshape(ck // rc, rc, lanes2).max(axis=0)
                    mx = part if mx is None else jnp.maximum(mx, part)
                if h_exp is not None:
                    for c in range(i * ck // rc, (i + 1) * ck // rc):
                        rr = slice(c * rc, (c + 1) * rc)
                        p_scr[h_exp % 2][rr, :] = jnp.exp2(s_scr[h_exp % 2][rr, :] - m_b).astype(BF16)
                if h_pv is not None and (i + 1) % pv_every == 0:
                    keys = slice((i + 1 - pv_every) * ck, (i + 1) * ck)
                    vt_ext = jnp.concatenate([vt_ref[head_cols(h_pv), keys], vt_ref[BW:VT_ROWS, keys]], axis=0)
                    acc_scr[h_pv] += jnp.dot(vt_ext, p_scr[h_pv % 2][keys, :], preferred_element_type=F32)
            return mx

        mx = None
        for r in range(DA_HEADS + 2):
            h_qk = r if r < DA_HEADS else None
            h_exp = r - 1 if 0 <= r - 1 < DA_HEADS else None
            h_pv = r - 2 if 0 <= r - 2 < DA_HEADS else None
            m_b = None
            if h_exp is not None:
                m_old = m_scr[h_exp]
                m_new = jnp.maximum(m_old, mx.max(axis=0, keepdims=True))
                m_scr[h_exp] = m_new
                acc_scr[h_exp] = jnp.exp2(m_old - m_new) * acc_scr[h_exp]
                m_b = jnp.broadcast_to(m_new, (rc, lanes2))
            mx = run_round(h_qk, h_exp, h_pv, m_b)

    @pl.when(j == 0)
    def _():
        row = lax.broadcasted_iota(I32, (DA_VD, tq), 0)
        for h in range(DA_HEADS):
            qh = qt_ref[h * DA_VD:(h + 1) * DA_VD, :]
            zero = jnp.zeros_like(qh)
            qz_scr[h] = jnp.concatenate([jnp.where(row < DA_DH, qh, zero),
                                         jnp.where(row >= DA_DH, qh, zero)], axis=1)
        m_scr[...] = jnp.full(m_scr.shape, -jnp.inf, F32)
        acc_scr[...] = jnp.zeros(acc_scr.shape, F32)
        attend_all(kc_ref, vtc_ref)

    if with_lat:
        attend_all(k_ref, vt_ref)

    @pl.when(j == nk - 1)
    def _():
        lam = (jnp.exp(jnp.sum(lq1_ref[...] * lk1_ref[...], keepdims=True))
               - jnp.exp(jnp.sum(lq2_ref[...] * lk2_ref[...], keepdims=True)) + lam_init)
        for h in range(DA_HEADS):
            o = acc_scr[h, 0:DA_VD, :] / acc_scr[h, DA_VD:DA_VD + 1, :]
            od = o[:, :tq] - lam * o[:, tq:]
            ms = jnp.mean(od * od, axis=0, keepdims=True)
            y = od * lax.rsqrt(ms + EPS) * g_ref[...] * (1.0 - lam_init)
            o_ref[:, h * DA_VD:(h + 1) * DA_VD] = y.T.astype(BF16)


def _diff(qt, p_ctx, vt_ctx, lat, lam_vecs, g_col, lam_init, tq, tk):
    n_q, n_ctx = qt.shape[1], p_ctx.shape[0]
    with_lat = lat is not None
    nk = lat[0].shape[0] // tk if with_lat else 1
    vec = pl.BlockSpec((1, DA_DH), lambda i, j: (0, 0))
    in_specs = [pl.BlockSpec((BW, tq), lambda i, j: (0, i)),
                pl.BlockSpec((n_ctx, BW), lambda i, j: (0, CB_KB)),
                pl.BlockSpec((VT_ROWS, n_ctx), lambda i, j: (0, 0))]
    args = [qt, p_ctx, vt_ctx]
    if with_lat:
        in_specs += [pl.BlockSpec((tk, BW), lambda i, j: (j, CB_KB)),
                     pl.BlockSpec((VT_ROWS, tk), lambda i, j: (0, j))]
        args += list(lat)
    in_specs += [vec, vec, vec, vec, pl.BlockSpec((DA_VD, 1), lambda i, j: (0, 0))]
    return pl.pallas_call(
        functools.partial(_diff_kernel, tq=tq, nk=nk, lam_init=lam_init, with_lat=with_lat),
        grid=(n_q // tq, nk),
        in_specs=in_specs,
        out_specs=pl.BlockSpec((tq, BW), lambda i, j: (i, 0)),
        out_shape=jax.ShapeDtypeStruct((n_q, BW), BF16),
        scratch_shapes=[pltpu.VMEM((DA_HEADS, DA_VD, 2 * tq), BF16),
                        pltpu.VMEM((DA_HEADS, 1, 2 * tq), F32),
                        pltpu.VMEM((DA_HEADS, DA_VD + BF16_ROWS, 2 * tq), F32),
                        pltpu.VMEM((max(tk, n_ctx), 2 * tq), F32),
                        pltpu.VMEM((max(tk, n_ctx), 2 * tq), F32),
                        pltpu.VMEM((max(tk, n_ctx), 2 * tq), BF16),
                        pltpu.VMEM((max(tk, n_ctx), 2 * tq), BF16)],
        compiler_params=_cparams(("arbitrary", "arbitrary")),
        name="diff_attn",
    )(*args, *lam_vecs, g_col)


def _merge_kernel(x_ref, ya_ref, yb_ref, u_ref, v_ref, ga_ref, gb_ref, gc_ref, mod_ref, wb_ref, wo_ref,
                  lng_ref, lnb_ref, ws_ref, bst_ref, g2_ref, wrt_ref,
                  x1_ref, h2_ref, aff_ref, yc_scr, *, mrow, tm):
    ug = _gelu(u_ref[...].astype(F32))
    vg = _gelu(v_ref[...].astype(F32))
    mu = jnp.mean(vg, axis=-1, keepdims=True)
    var = jnp.mean(jnp.square(vg - mu), axis=-1, keepdims=True)
    vn = ((vg - mu) * lax.rsqrt(var + EPS) * lng_ref[...] + lnb_ref[...]).astype(BF16)
    for ci in range(tm // GM_CHUNK):
        rs = slice(ci * GM_CHUNK, (ci + 1) * GM_CHUNK)
        for g in range(GM_GROUPS):
            cs = slice(g * LANES, (g + 1) * LANES)
            mixed = jnp.dot(ws_ref[g], vn[rs, cs], preferred_element_type=F32) + bst_ref[:, g:g + 1]
            yc_scr[rs, cs] = (ug[rs, cs] * mixed).astype(BF16)

    sig = jax.nn.sigmoid
    m = sig(ga_ref[...].astype(F32)) * jnp.dot(ya_ref[...], wb_ref[0], preferred_element_type=F32)
    m = m + sig(gb_ref[...].astype(F32)) * jnp.dot(yb_ref[...], wb_ref[1], preferred_element_type=F32)
    m = m + sig(gc_ref[...].astype(F32)) * jnp.dot(yc_scr[...], wb_ref[2], preferred_element_type=F32)
    y = jnp.dot(m.astype(BF16), wo_ref[...], preferred_element_type=F32)
    gt1 = mod_ref[mrow:mrow + 1, 2 * D:3 * D]
    x1 = x_ref[...] + gt1 * y
    x1_ref[...] = x1

    ms = jnp.mean(x1 * x1, axis=-1, keepdims=True)
    sh2 = mod_ref[mrow:mrow + 1, 3 * D:4 * D]
    sc2 = mod_ref[mrow:mrow + 1, 4 * D:5 * D]
    h2 = x1 * lax.rsqrt(ms + EPS) * g2_ref[...] * (1.0 + sc2) + sh2
    for s in range(D // LANES):
        h2_ref[pl.ds(s, tm, stride=D // LANES), :] = h2[:, s * LANES:(s + 1) * LANES]

    hh = h2.astype(BF16)
    hl = (h2 - hh.astype(F32)).astype(BF16)
    w = wrt_ref[...]
    wh = w.astype(BF16)
    wl = (w - wh.astype(F32)).astype(BF16)
    lg = (lax.dot_general(wh, hh, _NT, preferred_element_type=F32)
          + lax.dot_general(wh, hl, _NT, preferred_element_type=F32)
          + lax.dot_general(wl, hh, _NT, preferred_element_type=F32))
    e = jnp.exp(lg - lg.max(axis=0, keepdims=True))
    aff_ref[...] = e / e.sum(axis=0, keepdims=True)


def _merge(x, ya, yb, p_all, mod, mrow, wb, wo, lng, lnb, ws, bst, g2, wrt, tm):
    n = x.shape[0]
    const = lambda shape: pl.BlockSpec(shape, lambda i: (0,) * len(shape))
    return pl.pallas_call(
        functools.partial(_merge_kernel, mrow=mrow, tm=tm),
        grid=(n // tm,),
        in_specs=[pl.BlockSpec((tm, D), lambda i: (i, 0)),
                  pl.BlockSpec((tm, BW), lambda i: (i, 0)),
                  pl.BlockSpec((tm, BW), lambda i: (i, 0)),
                  pl.BlockSpec((tm, BW), lambda i: (i,CB_U)),
                  pl.BlockSpec((tm, BW), lambda i: (i,CB_V)),
                  pl.BlockSpec((tm, D), lambda i: (i,4)),
                  pl.BlockSpec((tm, D), lambda i: (i,5)),
                  pl.BlockSpec((tm, D), lambda i: (i,6)),
                  const((SUBLANES, N_MOD * D)),
                  const((3, BW, D)), const((D, D)),
                  const((1, BW)), const((1, BW)),
                  const((GM_GROUPS, GM_CHUNK, GM_CHUNK)), const((GM_CHUNK, GM_GROUPS)),
                  const((1, D)), const((N_EXPERTS, D))],
        out_specs=[pl.BlockSpec((tm, D), lambda i: (i, 0)),
                   pl.BlockSpec((tm * (D // LANES), LANES), lambda i: (i, 0)),
                   pl.BlockSpec((N_EXPERTS, tm), lambda i: (0, i))],
        out_shape=[jax.ShapeDtypeStruct((n, D), F32),
                   jax.ShapeDtypeStruct((n * (D // LANES), LANES), F32),
                   jax.ShapeDtypeStruct((N_EXPERTS, n), F32)],
        scratch_shapes=[pltpu.VMEM((tm, BW), BF16)],
        compiler_params=_cparams(("arbitrary",)),
        name="merge_prenorm",
    )(x, ya, yb, p_all, p_all, p_all, p_all, p_all, mod, wb, wo, lng, lnb, ws, bst, g2, wrt)


def _route_kernel(a_ref, idx_ref, gate_ref, slot_ref, rank_ref, *, cap, nrow):
    a = a_ref[0]

    def bisect(k, prefix):
        cand = prefix | jnp.left_shift(jnp.int32(1), 30 - k)
        cand_f = pltpu.bitcast(jnp.full((nrow, LANES), cand, I32), F32)
        cnt = jnp.sum(jnp.where(a >= cand_f, 1.0, 0.0))
        return jnp.where(cnt >= cap, cand, prefix)

    thr_bits = lax.fori_loop(0, 31, bisect, jnp.int32(0))
    thr = pltpu.bitcast(jnp.full((nrow, LANES), thr_bits, I32), F32)
    gt = a > thr
    eq = a == thr

    r_i = lax.broadcasted_iota(I32, (LANES, LANES), 0)
    c_i = lax.broadcasted_iota(I32, (LANES, LANES), 1)
    upper = (r_i <= c_i).astype(BF16)
    rr = lax.broadcasted_iota(I32, (nrow, nrow), 0)
    rc = lax.broadcasted_iota(I32, (nrow, nrow), 1)
    lower_strict = (rc < rr).astype(BF16)
    upper_strict = (rr < rc).astype(BF16)

    def prefix(mask):
        xf = jnp.where(mask, 1.0, 0.0)
        incl = jnp.dot(xf.astype(BF16), upper, preferred_element_type=F32)
        tot = jnp.broadcast_to(incl[:, LANES - 1:LANES], (nrow, LANES))
        base = jnp.dot(lower_strict, tot.astype(BF16), preferred_element_type=F32)
        return xf, incl, base + incl - xf

    _, _, eq_rank = prefix(eq)
    need = cap - jnp.sum(jnp.where(gt, 1.0, 0.0))
    sel = gt | (eq & (eq_rank < need))
    xf, incl, rank = prefix(sel)
    rank_ref[0] = rank.astype(I32)
    slot_ref[0] = jnp.where(sel, rank, -1.0).astype(I32)

    ones = jnp.ones((SUBLANES, LANES), BF16)
    tot_l = lax.dot_general(ones, xf.astype(BF16), _NT, preferred_element_type=F32)
    off_l = jnp.dot(tot_l.astype(BF16), upper_strict, preferred_element_type=F32)[0:1]
    tot_l = tot_l[0:1]

    jf = lax.broadcasted_iota(I32, (cap, nrow), 0).astype(F32)
    oh_row = (off_l <= jf) & (jf < off_l + tot_l)
    ohb = jnp.where(oh_row, 1.0, 0.0).astype(BF16)
    row_id = lax.broadcasted_iota(I32, (cap, nrow), 1).astype(F32)
    off_j = jnp.sum(jnp.where(oh_row, off_l, 0.0), axis=1, keepdims=True)
    row_j = jnp.sum(jnp.where(oh_row, row_id, 0.0), axis=1, keepdims=True)
    key = jnp.where(sel, incl, 0.0).astype(BF16)
    g = jnp.dot(ohb, key, preferred_element_type=F32)
    target = lax.broadcasted_iota(I32, (cap, 1), 0).astype(F32) - off_j + 1.0
    oh_lane = g == target
    lane_id = lax.broadcasted_iota(I32, (cap, LANES), 1).astype(F32)
    lane_j = jnp.sum(jnp.where(oh_lane, lane_id, 0.0), axis=1, keepdims=True)
    idx_ref[0] = (row_j * LANES + lane_j).astype(I32)

    a1 = a.astype(BF16)
    r1 = a - a1.astype(F32)
    a2 = r1.astype(BF16)
    a3 = (r1 - a2.astype(F32)).astype(BF16)
    arow = (jnp.dot(ohb, a1, preferred_element_type=F32) + jnp.dot(ohb, a2, preferred_element_type=F32)
            + jnp.dot(ohb, a3, preferred_element_type=F32))
    gate_ref[0] = jnp.sum(jnp.where(oh_lane, arow, 0.0), axis=1, keepdims=True)


def _route(aff3, cap):
    nrow = aff3.shape[1]
    return pl.pallas_call(
        functools.partial(_route_kernel, cap=cap, nrow=nrow),
        grid=(N_EXPERTS,),
        in_specs=[pl.BlockSpec((1, nrow, LANES), lambda e: (e, 0, 0))],
        out_specs=[pl.BlockSpec((1, cap, 1), lambda e: (e, 0, 0)),
                   pl.BlockSpec((1, cap, 1), lambda e: (e, 0, 0)),
                   pl.BlockSpec((1, nrow, LANES), lambda e: (e, 0, 0)),
                   pl.BlockSpec((1, nrow, LANES), lambda e: (e, 0, 0))],
        out_shape=[jax.ShapeDtypeStruct((N_EXPERTS, cap, 1), I32),
                   jax.ShapeDtypeStruct((N_EXPERTS, cap, 1), F32),
                   jax.ShapeDtypeStruct((N_EXPERTS, nrow, LANES), I32),
                   jax.ShapeDtypeStruct((N_EXPERTS, nrow, LANES), I32)],
        compiler_params=_cparams(("arbitrary",)),
        name="ec_route",
    )(aff3)


def _ffn_kernel(idx_ref, h2_hbm, gate_ref, w1_ref, w3_ref, w2_ref, ye_ref, buf, sem, wb_scr, *, tc, nc):
    sub = D // LANES

    @pl.when(pl.program_id(1) == 0)
    def _():
        wb_scr[0] = w1_ref[0, 0].astype(BF16)
        wb_scr[1] = w3_ref[0, 0].astype(BF16)
        wb_scr[2] = w2_ref[0, 0].astype(BF16)

    step = pl.program_id(0) * nc + pl.program_id(1)
    nsteps = N_EXPERTS * nc

    def row_copy(tok, r, slot):
        return pltpu.make_async_copy(h2_hbm.at[pl.ds(pl.multiple_of(tok * sub, sub), sub), :],
                                     buf.at[slot, pl.ds(pl.multiple_of(r * sub, sub), sub), :],
                                     sem.at[slot])

    def wait_slot(slot):
        pltpu.make_async_copy(h2_hbm.at[pl.ds(0, tc * sub), :], buf.at[slot], sem.at[slot]).wait()

    @pl.when(step == 0)
    def _():
        def body(r, carry):
            row_copy(idx_ref[0, r], r, 0).start()
            return carry

        lax.fori_loop(0, tc, body, 0, unroll=16)

    nxt = jnp.minimum(step + 1, nsteps - 1)
    nxt_e = nxt // nc
    nxt_base = (nxt % nc) * tc
    nslot = (step + 1) % 2
    parts = 4 if tc % 4 == 0 else 1

    def issue_part(q):
        for r in range(q * tc // parts, (q + 1) * tc // parts):
            row_copy(idx_ref[nxt_e, nxt_base + r], r, nslot).start()

    slot = step % 2
    wait_slot(slot)
    xin = jnp.concatenate([buf[slot, pl.ds(s, tc, stride=sub), :] for s in range(sub)], axis=1).astype(BF16)
    issue_part(0)
    a = jnp.dot(xin, wb_scr[0], preferred_element_type=F32)
    if parts == 4:
        issue_part(1)
    b = jnp.dot(xin, wb_scr[1], preferred_element_type=F32)
    hid = (a * jax.nn.sigmoid(a) * b).astype(BF16)
    if parts == 4:
        issue_part(2)
    y = jnp.dot(hid, wb_scr[2], preferred_element_type=F32)
    if parts == 4:
        issue_part(3)
    ye_ref[0] = (y * gate_ref[0]).astype(BF16)

    @pl.when(step == nsteps - 1)
    def _():
        wait_slot(nslot)


def _ffn(idx, gate, h2, w1, w3, w2, layer, tc):
    cap = idx.shape[1]
    nc = cap // tc
    sub = D // LANES
    wspec = pl.BlockSpec((1, 1, D, D), lambda e, c, idx: (layer, e, 0, 0))
    grid_spec = pltpu.PrefetchScalarGridSpec(
        num_scalar_prefetch=1,
        grid=(N_EXPERTS, nc),
        in_specs=[pl.BlockSpec(memory_space=pl.ANY),
                  pl.BlockSpec((1, tc, 1), lambda e, c, idx: (e, c, 0)),
                  wspec, wspec, wspec],
        out_specs=pl.BlockSpec((1, tc, D), lambda e, c, idx: (e, c, 0)),
        scratch_shapes=[pltpu.VMEM((2, tc * sub, LANES), F32),
                        pltpu.SemaphoreType.DMA((2,)),
                        pltpu.VMEM((3, D, D), BF16)])
    return pl.pallas_call(
        functools.partial(_ffn_kernel, tc=tc, nc=nc),
        grid_spec=grid_spec,
        out_shape=jax.ShapeDtypeStruct((N_EXPERTS, cap, D), BF16),
        compiler_params=_cparams(("arbitrary", "arbitrary")),
        name="expert_ffn",
    )(idx, h2, gate, w1, w3, w2)


def _combine_kernel(rs_ref, x_ref, slot_ref, mod_ref, gf_ref, ye_hbm, o_ref, win, sem, y_scr,
                    *, win_rows, cap, nt, tt, mrow, final):
    i = pl.program_id(0)

    def start_of(e, t):
        s0 = rs_ref[e, t]
        s_al = (s0 // BF16_ROWS) * BF16_ROWS
        return pl.multiple_of(jnp.minimum(s_al, cap - win_rows), BF16_ROWS)

    def copy(e, t, sl):
        return pltpu.make_async_copy(ye_hbm.at[e, pl.ds(start_of(e, t), win_rows), :],
                                     win.at[sl, e], sem.at[sl])

    @pl.when(i == 0)
    def _():
        for e in range(N_EXPERTS):
            copy(e, i, 0).start()

    @pl.when(i + 1 < nt)
    def _():
        for e in range(N_EXPERTS):
            copy(e, i + 1, (i + 1) % 2).start()

    sl = i % 2
    for e in range(N_EXPERTS):
        copy(e, i, sl).wait()

    def window_sum(r0, r1):
        acc = jnp.zeros((tt, D), F32)
        wi = lax.broadcasted_iota(I32, (r1 - r0, tt), 0) + r0
        for e in range(N_EXPERTS):
            rel = slot_ref[e:e + 1, :] - start_of(e, i)
            oh_t = jnp.where(wi == rel, 1.0, 0.0).astype(BF16)
            acc = acc + lax.dot_general(oh_t, win[sl, e, r0:r1, :], _TN, preferred_element_type=F32)
        return acc

    main_rows = min(win_rows, tt)
    gt2 = mod_ref[mrow:mrow + 1, 5 * D:6 * D]
    y_scr[...] = x_ref[...] + gt2 * window_sum(0, main_rows)
    if win_rows > main_rows:
        need_tail = False
        for e in range(N_EXPERTS):
            end = rs_ref[e, i + 1] - start_of(e, i)
            need_tail = jnp.logical_or(need_tail, end > main_rows)

        @pl.when(need_tail)
        def _():
            y_scr[...] += gt2 * window_sum(main_rows, win_rows)

    y = y_scr[...]
    if final:
        ms = jnp.mean(y * y, axis=-1, keepdims=True)
        y = y * lax.rsqrt(ms + EPS) * gf_ref[...]
    o_ref[...] = y


def _combine(rstart, x1, slot, mod, mrow, g_final, ye, tt, final):
    n = x1.shape[0]
    cap = ye.shape[1]
    nt = n // tt
    win_rows = min(cap, tt + BF16_ROWS)
    grid_spec = pltpu.PrefetchScalarGridSpec(
        num_scalar_prefetch=1,
        grid=(nt,),
        in_specs=[pl.BlockSpec((tt, D), lambda i, rs: (i, 0)),
                  pl.BlockSpec((N_EXPERTS, tt), lambda i, rs: (0, i)),
                  pl.BlockSpec((SUBLANES, N_MOD * D), lambda i, rs: (0, 0)),
                  pl.BlockSpec((1, D), lambda i, rs: (0, 0)),
                  pl.BlockSpec(memory_space=pl.ANY)],
        out_specs=pl.BlockSpec((tt, D), lambda i, rs: (i, 0)),
        scratch_shapes=[pltpu.VMEM((2, N_EXPERTS, win_rows, D), BF16),
                        pltpu.SemaphoreType.DMA((2,)),
                        pltpu.VMEM((tt, D), F32)])
    return pl.pallas_call(
        functools.partial(_combine_kernel, win_rows=win_rows, cap=cap, nt=nt, tt=tt, mrow=mrow, final=final),
        grid_spec=grid_spec,
        out_shape=jax.ShapeDtypeStruct((n, D), F32),
        compiler_params=_cparams(("arbitrary",)),
        name="moe_combine",
    )(rstart, x1, slot, mod, g_final, ye)


def _rope_tables(n_lat):
    t = jnp.arange(n_lat)
    inv = ROPE_THETA ** (-jnp.arange(0, ROPE_AXIS, 2, dtype=F32) / ROPE_AXIS)
    ang_r = (t // GRID_W).astype(F32)[:, None] * inv
    ang_c = (t % GRID_W).astype(F32)[:, None] * inv
    cr, sr, cc, sc = jnp.cos(ang_r), jnp.sin(ang_r), jnp.cos(ang_c), jnp.sin(ang_c)
    zero = jnp.zeros_like(sr)
    rep = lambda a: jnp.concatenate([a] * (LANES // DA_DH), axis=1)
    return (rep(jnp.concatenate([cr, cr, cc, cc], axis=1)),
            rep(jnp.concatenate([zero, sr, zero, sc], axis=1)),
            rep(jnp.concatenate([-sr, zero, -sc, zero], axis=1)))


def _largest_tile(n, unit, limit):
    best = unit
    for k in range(1, n // unit + 1):
        if n % (k * unit) == 0 and k * unit <= limit:
            best = k * unit
    return best


def _moe(x1, h2, aff_t, mod, mrow, w1, w3, w2, layer, g_final, final):
    n = x1.shape[0]
    cap = max(1, EC_CAPACITY * n // N_EXPERTS)
    tile = LANES * LANES if n <= LANES * LANES else n
    n_pad = max(n, tile)
    if n_pad > n:
        aff_t = jnp.concatenate([aff_t, jnp.full((N_EXPERTS, n_pad - n), -1.0, F32)], axis=1)
    idx, gate, slot, rank = _route(aff_t.reshape(N_EXPERTS, n_pad // LANES, LANES), cap)
    tt = min(256, n)
    slot = slot.reshape(N_EXPERTS, n_pad)[:, :n]
    rstart = jnp.concatenate([rank.reshape(N_EXPERTS, n_pad)[:, 0:n:tt],
                              jnp.full((N_EXPERTS, 1), cap, I32)], axis=1)
    tc = min(cap, 512)
    ye = _ffn(idx.reshape(N_EXPERTS, cap), gate, h2, w1, w3, w2, layer, tc)
    return _combine(rstart, x1, slot, mod, mrow, g_final, ye, tt, final)


def kernel(x, c, ctx, c_ctx, w_ada, b_ada, g_norm1, g_norm2, w_in, na_rpb, da_lam_q1, da_lam_k1, da_lam_q2,
           da_lam_k2, da_subln_g, gm_ln_g, gm_ln_b, gm_w_s, gm_b_s, w_branch, w_out, w_router, w_e1, w_e3,
           w_e2, g_final):
    depth = w_ada.shape[0]
    n_lat, n_ctx = x.shape[1], ctx.shape[1]
    rows = n_lat // GRID_W
    xs, xc = x[0], ctx[0]

    cc = jnp.concatenate([c.reshape(1, D), c_ctx.reshape(1, D), jnp.zeros((SUBLANES - 2, D), F32)], axis=0)
    mods = _ada(cc, w_ada, b_ada)

    tabs_lat = _rope_tables(n_lat)
    tabs_ctx = (jnp.ones((n_ctx, LANES), F32), jnp.zeros((n_ctx, LANES), F32), jnp.zeros((n_ctx, LANES), F32))
    tm_proj = _largest_tile(n_lat, 256, 1024)
    tk = _largest_tile(n_lat, 256, 1024)
    tq = 512

    for i in range(depth):
        last = i == depth - 1
        lam_init = 0.8 - 0.6 * math.exp(-0.3 * i)
        mod = mods[i]
        w_in_b = w_in[i].astype(BF16)
        g1 = g_norm1[i].reshape(1, D)
        g2 = g_norm2[i].reshape(1, D)
        gf = g_final.reshape(1, D)
        lam_vecs = [v[i].reshape(1, DA_DH).astype(F32) for v in (da_lam_q1, da_lam_k1, da_lam_q2, da_lam_k2)]
        g_col = da_subln_g[i].reshape(DA_VD, 1)
        merge_w = (w_branch[i].astype(BF16), w_out[i].astype(BF16), gm_ln_g[i].reshape(1, BW),
                   gm_ln_b[i].reshape(1, BW), gm_w_s[i].astype(BF16), gm_b_s[i].T, g2, w_router[i].T)

        p_lat, qt_lat, vt_lat = _proj(xs, g1, mod, w_in_b, tabs_lat, 0, tm_proj)
        p_ctx, qt_ctx, vt_ctx = _proj(xc, g1, mod, w_in_b, tabs_ctx, 1, n_ctx)

        ya = _na(p_lat, p_ctx, _na_bias(na_rpb[i], rows))
        yb = _diff(qt_lat, p_ctx, vt_ctx, (p_lat, vt_lat), lam_vecs, g_col, lam_init, tq, tk)
        x1, h2, aff_t = _merge(xs, ya, yb, p_lat, mod, 0, *merge_w, tm=256)
        xs = _moe(x1, h2, aff_t, mod, 0, w_e1, w_e3, w_e2, i, gf, last)

        if not last:
            yac = _ctx_dense(p_ctx)
            ybc = _diff(qt_ctx, p_ctx, vt_ctx, None, lam_vecs, g_col, lam_init, n_ctx, n_ctx)
            x1c, h2c, aff_tc = _merge(xc, yac, ybc, p_ctx, mod, 1, *merge_w, tm=n_ctx)
            xc = _moe(x1c, h2c, aff_tc, mod, 1, w_e1, w_e3, w_e2, i, gf, False)
    return xs[None]
```

```python
import functools
import math

import numpy as np
import jax
import jax.numpy as jnp
from jax import lax
from jax.experimental import pallas as pl
from jax.experimental.pallas import tpu as pltpu

F32 = jnp.float32
BF16 = jnp.bfloat16
I32 = jnp.int32

D = 1024
GRID_W = 64
BW = 512
N_COLBLK = 14
NA_HEADS = 8
NA_WIN_R = 8
NA_WIN_C = 16
NA_ROWS = 4
NA_KROWS = 12
DA_HEADS = 4
DA_DH = 64
DA_VD = 128
GM_GROUPS = 4
GM_CHUNK = 128
N_EXPERTS = 16
EC_CAPACITY = 2
ROPE_THETA = 10000.0
ROPE_AXIS = DA_DH // 2
N_MOD = 6
EPS = 1e-6
LANES = 128
SUBLANES = 8
BF16_ROWS = 16
NEG_BIG = -1e30
VT_ROWS = BW + BF16_ROWS
Q_SCALE_LOG2 = DA_DH ** -0.5 * math.log2(math.e)
VMEM_LIMIT = 56 * 1024 * 1024

CB_KA, CB_VA, CB_KB, CB_VB, CB_QA, CB_QB, CB_U, CB_V = range(8)

_NT = (((1,), (1,)), ((), ()))
_TN = (((0,), (0,)), ((), ()))


def _cparams(sem, flags=None):
    return pltpu.CompilerParams(dimension_semantics=sem, vmem_limit_bytes=VMEM_LIMIT, flags=flags)


def _gelu(x):
    return 0.5 * x * (1.0 + jnp.tanh(math.sqrt(2.0 / math.pi) * (x + 0.044715 * (x * x * x))))


def _ada_kernel(c_ref, w_ref, b_ref, o_ref):
    c = c_ref[...]
    s = c * jax.nn.sigmoid(c)
    o_ref[0] = jnp.dot(s, w_ref[0], preferred_element_type=F32,
                       precision=lax.Precision.HIGHEST) + b_ref[0]


def _ada(cc, w_ada, b_ada):
    depth = w_ada.shape[0]
    tn = 512
    return pl.pallas_call(
        _ada_kernel,
        grid=(depth, N_MOD * D // tn),
        in_specs=[pl.BlockSpec((SUBLANES, D), lambda i, j: (0, 0)),
                  pl.BlockSpec((1, D, tn), lambda i, j: (i, 0, j)),
                  pl.BlockSpec((1, 1, tn), lambda i, j: (i, 0, j))],
        out_specs=pl.BlockSpec((1, SUBLANES, tn), lambda i, j: (i, 0, j)),
        out_shape=jax.ShapeDtypeStruct((depth, SUBLANES, N_MOD * D), F32),
        compiler_params=_cparams(("arbitrary", "arbitrary")),
        name="ada_mod",
    )(cc, w_ada, b_ada.reshape(depth, 1, N_MOD * D))


def _proj_kernel(x_ref, g_ref, mod_ref, w_ref, c_ref, sp_ref, sm_ref, p_ref, qt_ref, vt_ref, h_scr, *, mrow):
    j = pl.program_id(1)

    @pl.when(j == 0)
    def _():
        x = x_ref[...]
        ms = jnp.mean(x * x, axis=-1, keepdims=True)
        xn = x * lax.rsqrt(ms + EPS) * g_ref[...]
        sh = mod_ref[mrow:mrow + 1, 0:D]
        sc = mod_ref[mrow:mrow + 1, D:2 * D]
        h_scr[...] = (xn * (1.0 + sc) + sh).astype(BF16)

    z = jnp.dot(h_scr[...], w_ref[...], preferred_element_type=F32)

    def rope(z):
        reps = BW // LANES
        c = jnp.concatenate([c_ref[...]] * reps, axis=1)
        sp = jnp.concatenate([sp_ref[...]] * reps, axis=1)
        sm = jnp.concatenate([sm_ref[...]] * reps, axis=1)
        half = ROPE_AXIS // 2
        return z * c + pltpu.roll(z, half, 1) * sp + pltpu.roll(z, BW - half, 1) * sm

    @pl.when(j == CB_KB)
    def _():
        p_ref[...] = rope(z).astype(BF16)

    @pl.when(j == CB_QB)
    def _():
        zr = rope(z)
        p_ref[...] = zr.astype(BF16)
        qt_ref[...] = (zr * Q_SCALE_LOG2).T.astype(BF16)

    @pl.when(j == CB_VB)
    def _():
        p_ref[...] = z.astype(BF16)
        vt_ref[0:BW, :] = z.T.astype(BF16)
        vt_ref[BW:VT_ROWS, :] = jnp.ones((VT_ROWS - BW, z.shape[0]), BF16)

    @pl.when((j != CB_KB) & (j != CB_QB) & (j != CB_VB))
    def _():
        p_ref[...] = z.astype(BF16)


def _proj(x, g, mod, w, tabs, mrow, tm):
    n = x.shape[0]
    return pl.pallas_call(
        functools.partial(_proj_kernel, mrow=mrow),
        grid=(n // tm, N_COLBLK),
        in_specs=[pl.BlockSpec((tm, D), lambda i, j: (i, 0)),
                  pl.BlockSpec((1, D), lambda i, j: (0, 0)),
                  pl.BlockSpec((SUBLANES, N_MOD * D), lambda i, j: (0, 0)),
                  pl.BlockSpec((D, BW), lambda i, j: (0, j)),
                  pl.BlockSpec((tm, LANES), lambda i, j: (i, 0)),
                  pl.BlockSpec((tm, LANES), lambda i, j: (i, 0)),
                  pl.BlockSpec((tm, LANES), lambda i, j: (i, 0))],
        out_specs=[pl.BlockSpec((tm, BW), lambda i, j: (i, j)),
                   pl.BlockSpec((BW, tm), lambda i, j: (0, i)),
                   pl.BlockSpec((VT_ROWS, tm), lambda i, j: (0, i))],
        out_shape=[jax.ShapeDtypeStruct((n, N_COLBLK * BW), BF16),
                   jax.ShapeDtypeStruct((BW, n), BF16),
                   jax.ShapeDtypeStruct((VT_ROWS, n), BF16)],
        scratch_shapes=[pltpu.VMEM((tm, D), BF16)],
        compiler_params=_cparams(("arbitrary", "arbitrary")),
        name="proj",
    )(x, g, mod, w, *tabs)


def _na_kernel(q_ref, k0_ref, k1_ref, k2_ref, v0_ref, v1_ref, v2_ref, kc_ref, vc_ref, bias_ref, o_ref):
    nq = q_ref.shape[0]
    lane = lax.broadcasted_iota(I32, (nq, LANES), 1)
    k_refs = (kc_ref, k0_ref, k1_ref, k2_ref)
    v_refs = (vc_ref, v0_ref, v1_ref, v2_ref)
    for g in range(NA_HEADS // 2):
        sl = slice(g * LANES, (g + 1) * LANES)
        qp = q_ref[:, sl] * 0.125
        ks = [r[:, sl] for r in k_refs]
        vs = [r[:, sl] for r in v_refs]
        outs = []
        for sub in range(2):
            h = 2 * g + sub
            keep = (lane < 64) if sub == 0 else (lane >= 64)
            qz = jnp.where(keep, qp, jnp.zeros_like(qp))
            ss = [lax.dot_general(qz, ks[0], _NT, preferred_element_type=F32)]
            for t in range(3):
                s = lax.dot_general(qz, ks[1 + t], _NT, preferred_element_type=F32)
                ss.append(s + bias_ref[0, h, :, t * nq:(t + 1) * nq])
            m = ss[0].max(axis=-1, keepdims=True)
            for s in ss[1:]:
                m = jnp.maximum(m, s.max(axis=-1, keepdims=True))
            l = jnp.zeros_like(m)
            o = jnp.zeros((nq, LANES), F32)
            for s, v in zip(ss, vs):
                p = jnp.exp(s - m)
                l = l + p.sum(axis=-1, keepdims=True)
                o = o + jnp.dot(p.astype(BF16), v, preferred_element_type=F32)
            outs.append(o / l)
        o_ref[:, sl] = jnp.where(lane < 64, outs[0], outs[1]).astype(BF16)


def _na_bias(rpb, rows):
    nb = rows // NA_ROWS
    c = np.arange(GRID_W)[:, None]
    kc = np.arange(GRID_W)[None, :]
    c0 = np.clip(c - NA_WIN_C // 2, 0, GRID_W - NA_WIN_C)
    col_ok = (kc >= c0) & (kc < c0 + NA_WIN_C)
    dc = kc - c + (NA_WIN_C - 1)
    sel = np.stack([(dc == d) & col_ok for d in range(2 * NA_WIN_C - 1)]).astype(np.float32)
    toep = jnp.einsum("hrd,dck->hrck", rpb.astype(F32), sel, precision=lax.Precision.HIGHEST)
    toep = toep + np.where(col_ok, 0.0, NEG_BIG).astype(np.float32)
    n_dr = 2 * NA_WIN_R - 1
    toep = jnp.concatenate([toep, jnp.full((NA_HEADS, 1, GRID_W, GRID_W), NEG_BIG, F32)], axis=1)
    which = np.full((3, NA_ROWS, NA_KROWS), n_dr, np.int32)
    for v, b in enumerate((0, 1, nb - 1)):
        kb0 = min(max(b - 1, 0), nb - 3)
        for a in range(NA_ROWS):
            r = NA_ROWS * b + a
            r0 = min(max(r - NA_WIN_R // 2, 0), rows - NA_WIN_R)
            for i in range(NA_KROWS):
                kr = NA_ROWS * kb0 + i
                if r0 <= kr < r0 + NA_WIN_R:
                    which[v, a, i] = kr - r + NA_WIN_R - 1
    blocks = jnp.take(toep, which.reshape(-1), axis=1)
    blocks = blocks.reshape(NA_HEADS, 3, NA_ROWS, NA_KROWS, GRID_W, GRID_W)
    return blocks.transpose(1, 0, 2, 4, 3, 5).reshape(3, NA_HEADS, NA_ROWS * GRID_W, NA_KROWS * GRID_W)


def _na(p_lat, p_ctx, bias):
    nq = NA_ROWS * GRID_W
    n_lat, n_ctx = p_lat.shape[0], p_ctx.shape[0]
    nb = n_lat // nq

    def kmap(t, col):
        return lambda b: (jnp.clip(b - 1, 0, nb - 3) + t, col)

    def bmap(b):
        return (jnp.where(b == 0, 0, jnp.where(b == nb - 1, 2, 1)), 0, 0, 0)

    blk = lambda f: pl.BlockSpec((nq, BW), f)
    return pl.pallas_call(
        _na_kernel,
        grid=(nb,),
        in_specs=[blk(lambda b: (b, CB_QA)),
                  blk(kmap(0, CB_KA)), blk(kmap(1, CB_KA)), blk(kmap(2, CB_KA)),
                  blk(kmap(0, CB_VA)), blk(kmap(1, CB_VA)), blk(kmap(2, CB_VA)),
                  pl.BlockSpec((n_ctx, BW), lambda b: (0, CB_KA)),
                  pl.BlockSpec((n_ctx, BW), lambda b: (0, CB_VA)),
                  pl.BlockSpec((1, NA_HEADS, nq, NA_KROWS * GRID_W), bmap)],
        out_specs=pl.BlockSpec((nq, BW), lambda b: (b, 0)),
        out_shape=jax.ShapeDtypeStruct((n_lat, BW), BF16),
        compiler_params=_cparams(("arbitrary",)),
        name="na_attn",
    )(p_lat, p_lat, p_lat, p_lat, p_lat, p_lat, p_lat, p_ctx, p_ctx, bias)


def _ctx_dense_kernel(q_ref, k_ref, v_ref, o_ref):
    nq = q_ref.shape[0]
    lane = lax.broadcasted_iota(I32, (nq, LANES), 1)
    for g in range(NA_HEADS // 2):
        sl = slice(g * LANES, (g + 1) * LANES)
        qp = q_ref[:, sl] * 0.125
        kp = k_ref[:, sl]
        vp = v_ref[:, sl]
        outs = []
        for sub in range(2):
            keep = (lane < 64) if sub == 0 else (lane >= 64)
            qz = jnp.where(keep, qp, jnp.zeros_like(qp))
            s = lax.dot_general(qz, kp, _NT, preferred_element_type=F32)
            m = s.max(axis=-1, keepdims=True)
            p = jnp.exp(s - m)
            l = p.sum(axis=-1, keepdims=True)
            outs.append(jnp.dot(p.astype(BF16), vp, preferred_element_type=F32) / l)
        o_ref[:, sl] = jnp.where(lane < 64, outs[0], outs[1]).astype(BF16)


def _ctx_dense(p_ctx):
    n_ctx = p_ctx.shape[0]
    blk = lambda col: pl.BlockSpec((n_ctx, BW), lambda i: (0, col))
    return pl.pallas_call(
        _ctx_dense_kernel,
        grid=(1,),
        in_specs=[blk(CB_QA), blk(CB_KA), blk(CB_VA)],
        out_specs=pl.BlockSpec((n_ctx, BW), lambda i: (0, 0)),
        out_shape=jax.ShapeDtypeStruct((n_ctx, BW), BF16),
        compiler_params=_cparams(("arbitrary",)),
        name="ctx_dense_attn",
    )(p_ctx, p_ctx, p_ctx)


def _diff_kernel(*refs, tq, nk, lam_init, with_lat):
    if with_lat:
        (qt_ref, kc_ref, vtc_ref, k_ref, vt_ref, lq1_ref, lk1_ref, lq2_ref, lk2_ref, g_ref, o_ref,
         qz_scr, m_scr, acc_scr, s0_scr, s1_scr, p0_scr, p1_scr) = refs
    else:
        (qt_ref, kc_ref, vtc_ref, lq1_ref, lk1_ref, lq2_ref, lk2_ref, g_ref, o_ref,
         qz_scr, m_scr, acc_scr, s0_scr, s1_scr, p0_scr, p1_scr) = refs
    s_scr = (s0_scr, s1_scr)
    p_scr = (p0_scr, p1_scr)
    j = pl.program_id(1)
    lanes2 = 2 * tq
    rc = max(BF16_ROWS, BF16_ROWS * 1024 // lanes2)
    ck = 256

    def attend_all(k_ref, vt_ref):
        ns = k_ref.shape[0] // ck

        def head_cols(h):
            return slice(h * DA_VD, (h + 1) * DA_VD)

        pv_every = 1

        def run_round(h_qk, h_exp, h_pv, m_b):
            mx = None
            for i in range(ns):
                rows = slice(i * ck, (i + 1) * ck)
                if h_qk is not None:
                    s = jnp.dot(k_ref[rows, head_cols(h_qk)], qz_scr[h_qk], preferred_element_type=F32)
                    s_scr[h_qk % 2][rows, :] = s
                    part = s.reshape(ck // rc, rc, lanes2).max(axis=0)
                    mx = part if mx is None else jnp.maximum(mx, part)
                if h_exp is not None:
                    for c in range(i * ck // rc, (i + 1) * ck // rc):
                        rr = slice(c * rc, (c + 1) * rc)
                        p_scr[h_exp % 2][rr, :] = jnp.exp2(s_scr[h_exp % 2][rr, :] - m_b).astype(BF16)
                if h_pv is not None and (i + 1) % pv_every == 0:
                    keys = slice((i + 1 - pv_every) * ck, (i + 1) * ck)
                    vt_ext = jnp.concatenate([vt_ref[head_cols(h_pv), keys], vt_ref[BW:VT_ROWS, keys]], axis=0)
                    acc_scr[h_pv] += jnp.dot(vt_ext, p_scr[h_pv % 2][keys, :], preferred_element_type=F32)
            return mx

        mx = None
        for r in range(DA_HEADS + 2):
            h_qk = r if r < DA_HEADS else None
            h_exp = r - 1 if 0 <= r - 1 < DA_HEADS else None
            h_pv = r - 2 if 0 <= r - 2 < DA_HEADS else None
            m_b = None
            if h_exp is not None:
                m_old = m_scr[h_exp]
                m_new = jnp.maximum(m_old, mx.max(axis=0, keepdims=True))
                m_scr[h_exp] = m_new
                acc_scr[h_exp] = jnp.exp2(m_old - m_new) * acc_scr[h_exp]
                m_b = jnp.broadcast_to(m_new, (rc, lanes2))
            mx = run_round(h_qk, h_exp, h_pv, m_b)

    @pl.when(j == 0)
    def _():
        row = lax.broadcasted_iota(I32, (DA_VD, tq), 0)
        for h in range(DA_HEADS):
            qh = qt_ref[h * DA_VD:(h + 1) * DA_VD, :]
            zero = jnp.zeros_like(qh)
            qz_scr[h] = jnp.concatenate([jnp.where(row < DA_DH, qh, zero),
                                         jnp.where(row >= DA_DH, qh, zero)], axis=1)
        m_scr[...] = jnp.full(m_scr.shape, -jnp.inf, F32)
        acc_scr[...] = jnp.zeros(acc_scr.shape, F32)
        attend_all(kc_ref, vtc_ref)

    if with_lat:
        attend_all(k_ref, vt_ref)

    @pl.when(j == nk - 1)
    def _():
        lam = (jnp.exp(jnp.sum(lq1_ref[...] * lk1_ref[...], keepdims=True))
               - jnp.exp(jnp.sum(lq2_ref[...] * lk2_ref[...], keepdims=True)) + lam_init)
        for h in range(DA_HEADS):
            o = acc_scr[h, 0:DA_VD, :] / acc_scr[h, DA_VD:DA_VD + 1, :]
            od = o[:, :tq] - lam * o[:, tq:]
            ms = jnp.mean(od * od, axis=0, keepdims=True)
            y = od * lax.rsqrt(ms + EPS) * g_ref[...] * (1.0 - lam_init)
            o_ref[:, h * DA_VD:(h + 1) * DA_VD] = y.T.astype(BF16)


def _diff(qt, p_ctx, vt_ctx, lat, lam_vecs, g_col, lam_init, tq, tk):
    n_q, n_ctx = qt.shape[1], p_ctx.shape[0]
    with_lat = lat is not None
    nk = lat[0].shape[0] // tk if with_lat else 1
    vec = pl.BlockSpec((1, DA_DH), lambda i, j: (0, 0))
    in_specs = [pl.BlockSpec((BW, tq), lambda i, j: (0, i)),
                pl.BlockSpec((n_ctx, BW), lambda i, j: (0, CB_KB)),
                pl.BlockSpec((VT_ROWS, n_ctx), lambda i, j: (0, 0))]
    args = [qt, p_ctx, vt_ctx]
    if with_lat:
        in_specs += [pl.BlockSpec((tk, BW), lambda i, j: (j, CB_KB)),
                     pl.BlockSpec((VT_ROWS, tk), lambda i, j: (0, j))]
        args += list(lat)
    in_specs += [vec, vec, vec, vec, pl.BlockSpec((DA_VD, 1), lambda i, j: (0, 0))]
    return pl.pallas_call(
        functools.partial(_diff_kernel, tq=tq, nk=nk, lam_init=lam_init, with_lat=with_lat),
        grid=(n_q // tq, nk),
        in_specs=in_specs,
        out_specs=pl.BlockSpec((tq, BW), lambda i, j: (i, 0)),
        out_shape=jax.ShapeDtypeStruct((n_q, BW), BF16),
        scratch_shapes=[pltpu.VMEM((DA_HEADS, DA_VD, 2 * tq), BF16),
                        pltpu.VMEM((DA_HEADS, 1, 2 * tq), F32),
                        pltpu.VMEM((DA_HEADS, DA_VD + BF16_ROWS, 2 * tq), F32),
                        pltpu.VMEM((max(tk, n_ctx), 2 * tq), F32),
                        pltpu.VMEM((max(tk, n_ctx), 2 * tq), F32),
                        pltpu.VMEM((max(tk, n_ctx), 2 * tq), BF16),
                        pltpu.VMEM((max(tk, n_ctx), 2 * tq), BF16)],
        compiler_params=_cparams(("arbitrary", "arbitrary")),
        name="diff_attn",
    )(*args, *lam_vecs, g_col)


def _merge_kernel(x_ref, ya_ref, yb_ref, u_ref, v_ref, ga_ref, gb_ref, gc_ref, mod_ref, wb_ref, wo_ref,
                  lng_ref, lnb_ref, ws_ref, bst_ref, g2_ref, wrt_ref,
                  x1_ref, h2_ref, aff_ref, yc_scr, *, mrow, tm):
    ug = _gelu(u_ref[...].astype(F32))
    vg = _gelu(v_ref[...].astype(F32))
    mu = jnp.mean(vg, axis=-1, keepdims=True)
    var = jnp.mean(jnp.square(vg - mu), axis=-1, keepdims=True)
    vn = ((vg - mu) * lax.rsqrt(var + EPS) * lng_ref[...] + lnb_ref[...]).astype(BF16)
    for ci in range(tm // GM_CHUNK):
        rs = slice(ci * GM_CHUNK, (ci + 1) * GM_CHUNK)
        for g in range(GM_GROUPS):
            cs = slice(g * LANES, (g + 1) * LANES)
            mixed = jnp.dot(ws_ref[g], vn[rs, cs], preferred_element_type=F32) + bst_ref[:, g:g + 1]
            yc_scr[rs, cs] = (ug[rs, cs] * mixed).astype(BF16)

    sig = jax.nn.sigmoid
    m = sig(ga_ref[...].astype(F32)) * jnp.dot(ya_ref[...], wb_ref[0], preferred_element_type=F32)
    m = m + sig(gb_ref[...].astype(F32)) * jnp.dot(yb_ref[...], wb_ref[1], preferred_element_type=F32)
    m = m + sig(gc_ref[...].astype(F32)) * jnp.dot(yc_scr[...], wb_ref[2], preferred_element_type=F32)
    y = jnp.dot(m.astype(BF16), wo_ref[...], preferred_element_type=F32)
    gt1 = mod_ref[mrow:mrow + 1, 2 * D:3 * D]
    x1 = x_ref[...] + gt1 * y
    x1_ref[...] = x1

    ms = jnp.mean(x1 * x1, axis=-1, keepdims=True)
    sh2 = mod_ref[mrow:mrow + 1, 3 * D:4 * D]
    sc2 = mod_ref[mrow:mrow + 1, 4 * D:5 * D]
    h2 = x1 * lax.rsqrt(ms + EPS) * g2_ref[...] * (1.0 + sc2) + sh2
    for s in range(D // LANES):
        h2_ref[pl.ds(s, tm, stride=D // LANES), :] = h2[:, s * LANES:(s + 1) * LANES]

    hh = h2.astype(BF16)
    hl = (h2 - hh.astype(F32)).astype(BF16)
    w = wrt_ref[...]
    wh = w.astype(BF16)
    wl = (w - wh.astype(F32)).astype(BF16)
    lg = (lax.dot_general(wh, hh, _NT, preferred_element_type=F32)
          + lax.dot_general(wh, hl, _NT, preferred_element_type=F32)
          + lax.dot_general(wl, hh, _NT, preferred_element_type=F32))
    e = jnp.exp(lg - lg.max(axis=0, keepdims=True))
    aff_ref[...] = e / e.sum(axis=0, keepdims=True)


def _merge(x, ya, yb, p_all, mod, mrow, wb, wo, lng, lnb, ws, bst, g2, wrt, tm):
    n = x.shape[0]
    const = lambda shape: pl.BlockSpec(shape, lambda i: (0,) * len(shape))
    return pl.pallas_call(
        functools.partial(_merge_kernel, mrow=mrow, tm=tm),
        grid=(n // tm,),
        in_specs=[pl.BlockSpec((tm, D), lambda i: (i, 0)),
                  pl.BlockSpec((tm, BW), lambda i: (i, 0)),
                  pl.BlockSpec((tm, BW), lambda i: (i, 0)),
                  pl.BlockSpec((tm, BW), lambda i: (i,CB_U)),
                  pl.BlockSpec((tm, BW), lambda i: (i,CB_V)),
                  pl.BlockSpec((tm, D), lambda i: (i,4)),
                  pl.BlockSpec((tm, D), lambda i: (i,5)),
                  pl.BlockSpec((tm, D), lambda i: (i,6)),
                  const((SUBLANES, N_MOD * D)),
                  const((3, BW, D)), const((D, D)),
                  const((1, BW)), const((1, BW)),
                  const((GM_GROUPS, GM_CHUNK, GM_CHUNK)), const((GM_CHUNK, GM_GROUPS)),
                  const((1, D)), const((N_EXPERTS, D))],
        out_specs=[pl.BlockSpec((tm, D), lambda i: (i, 0)),
                   pl.BlockSpec((tm * (D // LANES), LANES), lambda i: (i, 0)),
                   pl.BlockSpec((N_EXPERTS, tm), lambda i: (0, i))],
        out_shape=[jax.ShapeDtypeStruct((n, D), F32),
                   jax.ShapeDtypeStruct((n * (D // LANES), LANES), F32),
                   jax.ShapeDtypeStruct((N_EXPERTS, n), F32)],
        scratch_shapes=[pltpu.VMEM((tm, BW), BF16)],
        compiler_params=_cparams(("arbitrary",)),
        name="merge_prenorm",
    )(x, ya, yb, p_all, p_all, p_all, p_all, p_all, mod, wb, wo, lng, lnb, ws, bst, g2, wrt)


def _route_kernel(all_ref, a_ref, idx_ref, gate_ref, slot_ref, rank_ref, thr_scr, *, cap, nrow):
    e = pl.program_id(0)

    @pl.when(e == 0)
    def _():
        bits = pltpu.bitcast(all_ref[...], I32).reshape(N_EXPERTS, nrow, LANES)

        def bisect(k, prefix):
            cand = prefix | jnp.left_shift(jnp.int32(1), 30 - k)
            hit = jnp.where(bits >= cand, 1.0, 0.0)
            cnt = hit.sum(axis=1, keepdims=True).sum(axis=2, keepdims=True)
            return jnp.where(cnt >= cap, cand, prefix)

        thr_bits = lax.fori_loop(0, 31, bisect, jnp.zeros((N_EXPERTS, 1, 1), I32))
        tb = jnp.broadcast_to(thr_bits, (N_EXPERTS, SUBLANES, LANES)).reshape(N_EXPERTS * SUBLANES, LANES)
        thr_scr[...] = pltpu.bitcast(tb, F32).reshape(N_EXPERTS, SUBLANES, LANES)

    a = a_ref[0]
    thr = thr_scr[e][0:1, :]
    gt = a > thr
    eq = a == thr

    r_i = lax.broadcasted_iota(I32, (LANES, LANES), 0)
    c_i = lax.broadcasted_iota(I32, (LANES, LANES), 1)
    upper = (r_i <= c_i).astype(BF16)
    rr = lax.broadcasted_iota(I32, (nrow, nrow), 0)
    rc = lax.broadcasted_iota(I32, (nrow, nrow), 1)
    lower_strict = (rc < rr).astype(BF16)
    upper_strict = (rr < rc).astype(BF16)

    def prefix(mask):
        xf = jnp.where(mask, 1.0, 0.0)
        incl = jnp.dot(xf.astype(BF16), upper, preferred_element_type=F32)
        tot = jnp.broadcast_to(incl[:, LANES - 1:LANES], (nrow, LANES))
        base = jnp.dot(lower_strict, tot.astype(BF16), preferred_element_type=F32)
        return xf, incl, base + incl - xf

    _, _, eq_rank = prefix(eq)
    need = cap - jnp.sum(jnp.where(gt, 1.0, 0.0))
    sel = gt | (eq & (eq_rank < need))
    xf, incl, rank = prefix(sel)
    rank_ref[0] = rank.astype(I32)
    slot_ref[0] = jnp.where(sel, rank, -1.0).astype(I32)

    ones = jnp.ones((SUBLANES, LANES), BF16)
    tot_l = lax.dot_general(ones, xf.astype(BF16), _NT, preferred_element_type=F32)
    off_l = jnp.dot(tot_l.astype(BF16), upper_strict, preferred_element_type=F32)[0:1]
    tot_l = tot_l[0:1]

    jf = lax.broadcasted_iota(I32, (cap, nrow), 0).astype(F32)
    oh_row = (off_l <= jf) & (jf < off_l + tot_l)
    ohb = jnp.where(oh_row, 1.0, 0.0).astype(BF16)
    row_id = lax.broadcasted_iota(I32, (cap, nrow), 1).astype(F32)
    off_j = jnp.sum(jnp.where(oh_row, off_l, 0.0), axis=1, keepdims=True)
    row_j = jnp.sum(jnp.where(oh_row, row_id, 0.0), axis=1, keepdims=True)
    key = jnp.where(sel, incl, 0.0).astype(BF16)
    g = jnp.dot(ohb, key, preferred_element_type=F32)
    target = lax.broadcasted_iota(I32, (cap, 1), 0).astype(F32) - off_j + 1.0
    oh_lane = g == target
    lane_id = lax.broadcasted_iota(I32, (cap, LANES), 1).astype(F32)
    lane_j = jnp.sum(jnp.where(oh_lane, lane_id, 0.0), axis=1, keepdims=True)
    idx_ref[0] = (row_j * LANES + lane_j).astype(I32)

    a1 = a.astype(BF16)
    r1 = a - a1.astype(F32)
    a2 = r1.astype(BF16)
    a3 = (r1 - a2.astype(F32)).astype(BF16)
    arow = (jnp.dot(ohb, a1, preferred_element_type=F32) + jnp.dot(ohb, a2, preferred_element_type=F32)
            + jnp.dot(ohb, a3, preferred_element_type=F32))
    gate_ref[0] = jnp.sum(jnp.where(oh_lane, arow, 0.0), axis=1, keepdims=True)


def _route(aff3, cap):
    nrow = aff3.shape[1]
    return pl.pallas_call(
        functools.partial(_route_kernel, cap=cap, nrow=nrow),
        grid=(N_EXPERTS,),
        in_specs=[pl.BlockSpec((N_EXPERTS * nrow, LANES), lambda e: (0, 0)),
                  pl.BlockSpec((1, nrow, LANES), lambda e: (e, 0, 0))],
        out_specs=[pl.BlockSpec((1, cap, 1), lambda e: (e, 0, 0)),
                   pl.BlockSpec((1, cap, 1), lambda e: (e, 0, 0)),
                   pl.BlockSpec((1, nrow, LANES), lambda e: (e, 0, 0)),
                   pl.BlockSpec((1, nrow, LANES), lambda e: (e, 0, 0))],
        out_shape=[jax.ShapeDtypeStruct((N_EXPERTS, cap, 1), I32),
                   jax.ShapeDtypeStruct((N_EXPERTS, cap, 1), F32),
                   jax.ShapeDtypeStruct((N_EXPERTS, nrow, LANES), I32),
                   jax.ShapeDtypeStruct((N_EXPERTS, nrow, LANES), I32)],
        scratch_shapes=[pltpu.VMEM((N_EXPERTS, SUBLANES, LANES), F32)],
        compiler_params=_cparams(("arbitrary",)),
        name="ec_route",
    )(aff3.reshape(N_EXPERTS * nrow, LANES), aff3)


def _ffn_kernel(idx_ref, h2_hbm, gate_ref, w1_ref, w3_ref, w2_ref, ye_ref, buf, sem, wb_scr, *, tc, nc):
    sub = D // LANES

    @pl.when(pl.program_id(1) == 0)
    def _():
        wb_scr[0] = w1_ref[0, 0].astype(BF16)
        wb_scr[1] = w3_ref[0, 0].astype(BF16)
        wb_scr[2] = w2_ref[0, 0].astype(BF16)

    step = pl.program_id(0) * nc + pl.program_id(1)
    nsteps = N_EXPERTS * nc

    def row_copy(tok, r, slot):
        return pltpu.make_async_copy(h2_hbm.at[pl.ds(pl.multiple_of(tok * sub, sub), sub), :],
                                     buf.at[slot, pl.ds(pl.multiple_of(r * sub, sub), sub), :],
                                     sem.at[slot])

    def wait_slot(slot):
        pltpu.make_async_copy(h2_hbm.at[pl.ds(0, tc * sub), :], buf.at[slot], sem.at[slot]).wait()

    @pl.when(step == 0)
    def _():
        def body(r, carry):
            row_copy(idx_ref[0, r], r, 0).start()
            return carry

        lax.fori_loop(0, tc, body, 0, unroll=16)

    nxt = jnp.minimum(step + 1, nsteps - 1)
    nxt_e = nxt // nc
    nxt_base = (nxt % nc) * tc
    nslot = (step + 1) % 2
    parts = 4 if tc % 4 == 0 else 1

    def issue_part(q):
        for r in range(q * tc // parts, (q + 1) * tc // parts):
            row_copy(idx_ref[nxt_e, nxt_base + r], r, nslot).start(priority=r % 2)

    slot = step % 2
    wait_slot(slot)
    xin = jnp.concatenate([buf[slot, pl.ds(s, tc, stride=sub), :] for s in range(sub)], axis=1).astype(BF16)
    issue_part(0)
    a = jnp.dot(xin, wb_scr[0], preferred_element_type=F32)
    if parts == 4:
        issue_part(1)
    b = jnp.dot(xin, wb_scr[1], preferred_element_type=F32)
    hid = (a * jax.nn.sigmoid(a) * b).astype(BF16)
    if parts == 4:
        issue_part(2)
    y = jnp.dot(hid, wb_scr[2], preferred_element_type=F32)
    if parts == 4:
        issue_part(3)
    ye_ref[0] = (y * gate_ref[0]).astype(BF16)

    @pl.when(step == nsteps - 1)
    def _():
        wait_slot(nslot)


def _ffn(idx, gate, h2, w1, w3, w2, layer, tc):
    cap = idx.shape[1]
    nc = cap // tc
    sub = D // LANES
    wspec = pl.BlockSpec((1, 1, D, D), lambda e, c, idx: (layer, e, 0, 0))
    grid_spec = pltpu.PrefetchScalarGridSpec(
        num_scalar_prefetch=1,
        grid=(N_EXPERTS, nc),
        in_specs=[pl.BlockSpec(memory_space=pl.ANY),
                  pl.BlockSpec((1, tc, 1), lambda e, c, idx: (e, c, 0)),
                  wspec, wspec, wspec],
        out_specs=pl.BlockSpec((1, tc, D), lambda e, c, idx: (e, c, 0)),
        scratch_shapes=[pltpu.VMEM((2, tc * sub, LANES), F32),
                        pltpu.SemaphoreType.DMA((2,)),
                        pltpu.VMEM((3, D, D), BF16)])
    return pl.pallas_call(
        functools.partial(_ffn_kernel, tc=tc, nc=nc),
        grid_spec=grid_spec,
        out_shape=jax.ShapeDtypeStruct((N_EXPERTS, cap, D), BF16),
        compiler_params=_cparams(("arbitrary", "arbitrary")),
        name="expert_ffn",
    )(idx, h2, gate, w1, w3, w2)


def _combine_kernel(rs_ref, x_ref, slot_ref, mod_ref, gf_ref, ye_hbm, o_ref, win, sem, y_scr,
                    *, win_rows, cap, nt, tt, mrow, final):
    i = pl.program_id(0)

    def start_of(e, t):
        s0 = rs_ref[e, t]
        s_al = (s0 // BF16_ROWS) * BF16_ROWS
        return pl.multiple_of(jnp.minimum(s_al, cap - win_rows), BF16_ROWS)

    def copy(e, t, sl):
        return pltpu.make_async_copy(ye_hbm.at[e, pl.ds(start_of(e, t), win_rows), :],
                                     win.at[sl, e], sem.at[sl])

    @pl.when(i == 0)
    def _():
        for e in range(N_EXPERTS):
            copy(e, i, 0).start()

    @pl.when(i + 1 < nt)
    def _():
        for e in range(N_EXPERTS):
            copy(e, i + 1, (i + 1) % 2).start()

    sl = i % 2
    for e in range(N_EXPERTS):
        copy(e, i, sl).wait()

    def window_sum(r0, r1):
        acc = jnp.zeros((tt, D), F32)
        wi = lax.broadcasted_iota(I32, (r1 - r0, tt), 0) + r0
        for e in range(N_EXPERTS):
            rel = slot_ref[e:e + 1, :] - start_of(e, i)
            oh_t = jnp.where(wi == rel, 1.0, 0.0).astype(BF16)
            acc = acc + lax.dot_general(oh_t, win[sl, e, r0:r1, :], _TN, preferred_element_type=F32)
        return acc

    main_rows = min(win_rows, tt)
    gt2 = mod_ref[mrow:mrow + 1, 5 * D:6 * D]
    y_scr[...] = x_ref[...] + gt2 * window_sum(0, main_rows)
    if win_rows > main_rows:
        need_tail = False
        for e in range(N_EXPERTS):
            end = rs_ref[e, i + 1] - start_of(e, i)
            need_tail = jnp.logical_or(need_tail, end > main_rows)

        @pl.when(need_tail)
        def _():
            y_scr[...] += gt2 * window_sum(main_rows, win_rows)

    y = y_scr[...]
    if final:
        ms = jnp.mean(y * y, axis=-1, keepdims=True)
        y = y * lax.rsqrt(ms + EPS) * gf_ref[...]
    o_ref[...] = y


def _combine(rstart, x1, slot, mod, mrow, g_final, ye, tt, final):
    n = x1.shape[0]
    cap = ye.shape[1]
    nt = n // tt
    win_rows = min(cap, tt + BF16_ROWS)
    grid_spec = pltpu.PrefetchScalarGridSpec(
        num_scalar_prefetch=1,
        grid=(nt,),
        in_specs=[pl.BlockSpec((tt, D), lambda i, rs: (i, 0)),
                  pl.BlockSpec((N_EXPERTS, tt), lambda i, rs: (0, i)),
                  pl.BlockSpec((SUBLANES, N_MOD * D), lambda i, rs: (0, 0)),
                  pl.BlockSpec((1, D), lambda i, rs: (0, 0)),
                  pl.BlockSpec(memory_space=pl.ANY)],
        out_specs=pl.BlockSpec((tt, D), lambda i, rs: (i, 0)),
        scratch_shapes=[pltpu.VMEM((2, N_EXPERTS, win_rows, D), BF16),
                        pltpu.SemaphoreType.DMA((2,)),
                        pltpu.VMEM((tt, D), F32)])
    return pl.pallas_call(
        functools.partial(_combine_kernel, win_rows=win_rows, cap=cap, nt=nt, tt=tt, mrow=mrow, final=final),
        grid_spec=grid_spec,
        out_shape=jax.ShapeDtypeStruct((n, D), F32),
        compiler_params=_cparams(("arbitrary",)),
        name="moe_combine",
    )(rstart, x1, slot, mod, g_final, ye)


def _rope_tables(n_lat):
    rows = n_lat // GRID_W
    inv = ROPE_THETA ** (-jnp.arange(0, ROPE_AXIS, 2, dtype=F32) / ROPE_AXIS)
    ang_r = jnp.arange(rows).astype(F32)[:, None] * inv
    ang_c = jnp.arange(GRID_W).astype(F32)[:, None] * inv
    per_row = lambda a: jnp.broadcast_to(a[:, None, :], (rows, GRID_W, a.shape[1])).reshape(n_lat, a.shape[1])
    per_col = lambda a: jnp.broadcast_to(a[None, :, :], (rows, GRID_W, a.shape[1])).reshape(n_lat, a.shape[1])
    cr, sr = per_row(jnp.cos(ang_r)), per_row(jnp.sin(ang_r))
    cc, sc = per_col(jnp.cos(ang_c)), per_col(jnp.sin(ang_c))
    zero = jnp.zeros_like(sr)
    rep = lambda a: jnp.concatenate([a] * (LANES // DA_DH), axis=1)
    return (rep(jnp.concatenate([cr, cr, cc, cc], axis=1)),
            rep(jnp.concatenate([zero, sr, zero, sc], axis=1)),
            rep(jnp.concatenate([-sr, zero, -sc, zero], axis=1)))


def _largest_tile(n, unit, limit):
    best = unit
    for k in range(1, n // unit + 1):
        if n % (k * unit) == 0 and k * unit <= limit:
            best = k * unit
    return best


def _moe(x1, h2, aff_t, mod, mrow, w1, w3, w2, layer, g_final, final):
    n = x1.shape[0]
    cap = max(1, EC_CAPACITY * n // N_EXPERTS)
    tile = LANES * LANES if n <= LANES * LANES else n
    n_pad = max(n, tile)
    if n_pad > n:
        aff_t = jnp.concatenate([aff_t, jnp.full((N_EXPERTS, n_pad - n), -1.0, F32)], axis=1)
    idx, gate, slot, rank = _route(aff_t.reshape(N_EXPERTS, n_pad // LANES, LANES), cap)
    tt = min(256, n)
    slot = slot.reshape(N_EXPERTS, n_pad)[:, :n]
    rstart = jnp.concatenate([rank.reshape(N_EXPERTS, n_pad)[:, 0:n:tt],
                              jnp.full((N_EXPERTS, 1), cap, I32)], axis=1)
    tc = min(cap, 512)
    ye = _ffn(idx.reshape(N_EXPERTS, cap), gate, h2, w1, w3, w2, layer, tc)
    return _combine(rstart, x1, slot, mod, mrow, g_final, ye, tt, final)


def kernel(x, c, ctx, c_ctx, w_ada, b_ada, g_norm1, g_norm2, w_in, na_rpb, da_lam_q1, da_lam_k1, da_lam_q2,
           da_lam_k2, da_subln_g, gm_ln_g, gm_ln_b, gm_w_s, gm_b_s, w_branch, w_out, w_router, w_e1, w_e3,
           w_e2, g_final):
    depth = w_ada.shape[0]
    n_lat, n_ctx = x.shape[1], ctx.shape[1]
    rows = n_lat // GRID_W
    xs, xc = x[0], ctx[0]

    cc = jnp.concatenate([c.reshape(1, D), c_ctx.reshape(1, D), jnp.zeros((SUBLANES - 2, D), F32)], axis=0)
    mods = _ada(cc, w_ada, b_ada)

    tabs_lat = _rope_tables(n_lat)
    tabs_ctx = (jnp.ones((n_ctx, LANES), F32), jnp.zeros((n_ctx, LANES), F32), jnp.zeros((n_ctx, LANES), F32))
    tm_proj = _largest_tile(n_lat, 256, 1024)
    tk = _largest_tile(n_lat, 256, 2048)
    tq = 512

    for i in range(depth):
        last = i == depth - 1
        lam_init = 0.8 - 0.6 * math.exp(-0.3 * i)
        mod = mods[i]
        w_in_b = w_in[i].astype(BF16)
        g1 = g_norm1[i].reshape(1, D)
        g2 = g_norm2[i].reshape(1, D)
        gf = g_final.reshape(1, D)
        lam_vecs = [v[i].reshape(1, DA_DH).astype(F32) for v in (da_lam_q1, da_lam_k1, da_lam_q2, da_lam_k2)]
        g_col = da_subln_g[i].reshape(DA_VD, 1)
        merge_w = (w_branch[i].astype(BF16), w_out[i].astype(BF16), gm_ln_g[i].reshape(1, BW),
                   gm_ln_b[i].reshape(1, BW), gm_w_s[i].astype(BF16), gm_b_s[i].T, g2, w_router[i].T)

        p_lat, qt_lat, vt_lat = _proj(xs, g1, mod, w_in_b, tabs_lat, 0, tm_proj)
        p_ctx, qt_ctx, vt_ctx = _proj(xc, g1, mod, w_in_b, tabs_ctx, 1, n_ctx)

        ya = _na(p_lat, p_ctx, _na_bias(na_rpb[i], rows))
        yb = _diff(qt_lat, p_ctx, vt_ctx, (p_lat, vt_lat), lam_vecs, g_col, lam_init, tq, tk)
        x1, h2, aff_t = _merge(xs, ya, yb, p_lat, mod, 0, *merge_w, tm=256)
        xs = _moe(x1, h2, aff_t, mod, 0, w_e1, w_e3, w_e2, i, gf, last)

        if not last:
            yac = _ctx_dense(p_ctx)
            ybc = _diff(qt_ctx, p_ctx, vt_ctx, None, lam_vecs, g_col, lam_init, n_ctx, n_ctx)
            x1c, h2c, aff_tc = _merge(xc, yac, ybc, p_ctx, mod, 1, *merge_w, tm=n_ctx)
            xc = _moe(x1c, h2c, aff_tc, mod, 1, w_e1, w_e3, w_e2, i, gf, False)
    return xs[None]
```

```python
import functools
import math

import numpy as np
import jax
import jax.numpy as jnp
from jax import lax
from jax.experimental import pallas as pl
from jax.experimental.pallas import tpu as pltpu

F32 = jnp.float32
BF16 = jnp.bfloat16
I32 = jnp.int32

D = 1024
GRID_W = 64
BW = 512
N_COLBLK = 14
NA_HEADS = 8
NA_WIN_R = 8
NA_WIN_C = 16
NA_ROWS = 4
NA_KROWS = 12
DA_HEADS = 4
DA_DH = 64
DA_VD = 128
GM_GROUPS = 4
GM_CHUNK = 128
N_EXPERTS = 16
EC_CAPACITY = 2
ROPE_THETA = 10000.0
ROPE_AXIS = DA_DH // 2
N_MOD = 6
EPS = 1e-6
LANES = 128
SUBLANES = 8
BF16_ROWS = 16
NEG_BIG = -1e30
VT_ROWS = BW + BF16_ROWS
Q_SCALE_LOG2 = DA_DH ** -0.5 * math.log2(math.e)
VMEM_LIMIT = 56 * 1024 * 1024

CB_KA, CB_VA, CB_KB, CB_VB, CB_QA, CB_QB, CB_U, CB_V = range(8)

_NT = (((1,), (1,)), ((), ()))
_TN = (((0,), (0,)), ((), ()))


def _cparams(sem, flags=None):
    return pltpu.CompilerParams(dimension_semantics=sem, vmem_limit_bytes=VMEM_LIMIT, flags=flags)


def _gelu(x):
    return 0.5 * x * (1.0 + jnp.tanh(math.sqrt(2.0 / math.pi) * (x + 0.044715 * (x * x * x))))


def _ada_kernel(c_ref, w_ref, b_ref, o_ref):
    c = c_ref[...]
    s = c * jax.nn.sigmoid(c)
    o_ref[0] = jnp.dot(s, w_ref[0], preferred_element_type=F32,
                       precision=lax.Precision.HIGHEST) + b_ref[0]


def _ada(cc, w_ada, b_ada):
    depth = w_ada.shape[0]
    tn = 512
    return pl.pallas_call(
        _ada_kernel,
        grid=(depth, N_MOD * D // tn),
        in_specs=[pl.BlockSpec((SUBLANES, D), lambda i, j: (0, 0)),
                  pl.BlockSpec((1, D, tn), lambda i, j: (i, 0, j)),
                  pl.BlockSpec((1, 1, tn), lambda i, j: (i, 0, j))],
        out_specs=pl.BlockSpec((1, SUBLANES, tn), lambda i, j: (i, 0, j)),
        out_shape=jax.ShapeDtypeStruct((depth, SUBLANES, N_MOD * D), F32),
        compiler_params=_cparams(("arbitrary", "arbitrary")),
        name="ada_mod",
    )(cc, w_ada, b_ada.reshape(depth, 1, N_MOD * D))


def _proj_kernel(x_ref, g_ref, mod_ref, w_ref, c_ref, sp_ref, sm_ref, p_ref, qt_ref, vt_ref, h_scr, *, mrow):
    j = pl.program_id(1)

    @pl.when(j == 0)
    def _():
        x = x_ref[...]
        ms = jnp.mean(x * x, axis=-1, keepdims=True)
        xn = x * lax.rsqrt(ms + EPS) * g_ref[...]
        sh = mod_ref[mrow:mrow + 1, 0:D]
        sc = mod_ref[mrow:mrow + 1, D:2 * D]
        h_scr[...] = (xn * (1.0 + sc) + sh).astype(BF16)

    z = jnp.dot(h_scr[...], w_ref[...], preferred_element_type=F32)

    def rope(z):
        reps = BW // LANES
        c = jnp.concatenate([c_ref[...]] * reps, axis=1)
        sp = jnp.concatenate([sp_ref[...]] * reps, axis=1)
        sm = jnp.concatenate([sm_ref[...]] * reps, axis=1)
        half = ROPE_AXIS // 2
        return z * c + pltpu.roll(z, half, 1) * sp + pltpu.roll(z, BW - half, 1) * sm

    @pl.when(j == CB_KB // 2)
    def _():
        p_ref[:, 0:BW] = rope(z[:, 0:BW]).astype(BF16)
        zv = z[:, BW:2 * BW]
        p_ref[:, BW:2 * BW] = zv.astype(BF16)
        vt_ref[0:BW, :] = zv.T.astype(BF16)
        vt_ref[BW:VT_ROWS, :] = jnp.ones((VT_ROWS - BW, z.shape[0]), BF16)

    @pl.when(j == CB_QB // 2)
    def _():
        p_ref[:, 0:BW] = z[:, 0:BW].astype(BF16)
        zr = rope(z[:, BW:2 * BW])
        p_ref[:, BW:2 * BW] = zr.astype(BF16)
        qt_ref[...] = (zr * Q_SCALE_LOG2).T.astype(BF16)

    @pl.when((j != CB_KB // 2) & (j != CB_QB // 2))
    def _():
        p_ref[...] = z.astype(BF16)


def _proj(x, g, mod, w, tabs, mrow, tm):
    n = x.shape[0]
    return pl.pallas_call(
        functools.partial(_proj_kernel, mrow=mrow),
        grid=(n // tm, N_COLBLK // 2),
        in_specs=[pl.BlockSpec((tm, D), lambda i, j: (i, 0)),
                  pl.BlockSpec((1, D), lambda i, j: (0, 0)),
                  pl.BlockSpec((SUBLANES, N_MOD * D), lambda i, j: (0, 0)),
                  pl.BlockSpec((D, 2 * BW), lambda i, j: (0, j)),
                  pl.BlockSpec((tm, LANES), lambda i, j: (i, 0)),
                  pl.BlockSpec((tm, LANES), lambda i, j: (i, 0)),
                  pl.BlockSpec((tm, LANES), lambda i, j: (i, 0))],
        out_specs=[pl.BlockSpec((tm, 2 * BW), lambda i, j: (i, j)),
                   pl.BlockSpec((BW, tm), lambda i, j: (0, i)),
                   pl.BlockSpec((VT_ROWS, tm), lambda i, j: (0, i))],
        out_shape=[jax.ShapeDtypeStruct((n, N_COLBLK * BW), BF16),
                   jax.ShapeDtypeStruct((BW, n), BF16),
                   jax.ShapeDtypeStruct((VT_ROWS, n), BF16)],
        scratch_shapes=[pltpu.VMEM((tm, D), BF16)],
        compiler_params=_cparams(("arbitrary", "arbitrary")),
        name="proj",
    )(x, g, mod, w, *tabs)


def _na_kernel(q_ref, k0_ref, k1_ref, k2_ref, v0_ref, v1_ref, v2_ref, kc_ref, vc_ref, bias_ref, o_ref):
    nq = q_ref.shape[0]
    lane = lax.broadcasted_iota(I32, (nq, LANES), 1)
    k_refs = (kc_ref, k0_ref, k1_ref, k2_ref)
    v_refs = (vc_ref, v0_ref, v1_ref, v2_ref)
    for g in range(NA_HEADS // 2):
        sl = slice(g * LANES, (g + 1) * LANES)
        qp = q_ref[:, sl] * 0.125
        ks = [r[:, sl] for r in k_refs]
        vs = [r[:, sl] for r in v_refs]
        outs = []
        for sub in range(2):
            h = 2 * g + sub
            keep = (lane < 64) if sub == 0 else (lane >= 64)
            qz = jnp.where(keep, qp, jnp.zeros_like(qp))
            ss = [lax.dot_general(qz, ks[0], _NT, preferred_element_type=F32)]
            for t in range(3):
                s = lax.dot_general(qz, ks[1 + t], _NT, preferred_element_type=F32)
                ss.append(s + bias_ref[0, h, :, t * nq:(t + 1) * nq])
            m = ss[0].max(axis=-1, keepdims=True)
            for s in ss[1:]:
                m = jnp.maximum(m, s.max(axis=-1, keepdims=True))
            l = jnp.zeros_like(m)
            o = jnp.zeros((nq, LANES), F32)
            for s, v in zip(ss, vs):
                p = jnp.exp(s - m)
                l = l + p.sum(axis=-1, keepdims=True)
                o = o + jnp.dot(p.astype(BF16), v, preferred_element_type=F32)
            outs.append(o / l)
        o_ref[:, sl] = jnp.where(lane < 64, outs[0], outs[1]).astype(BF16)


def _na_bias(rpb, rows):
    nb = rows // NA_ROWS
    c = np.arange(GRID_W)[:, None]
    kc = np.arange(GRID_W)[None, :]
    c0 = np.clip(c - NA_WIN_C // 2, 0, GRID_W - NA_WIN_C)
    col_ok = (kc >= c0) & (kc < c0 + NA_WIN_C)
    dc = kc - c + (NA_WIN_C - 1)
    sel = np.stack([(dc == d) & col_ok for d in range(2 * NA_WIN_C - 1)]).astype(np.float32)
    toep = jnp.einsum("hrd,dck->hrck", rpb.astype(F32), sel, precision=lax.Precision.HIGHEST)
    toep = toep + np.where(col_ok, 0.0, NEG_BIG).astype(np.float32)
    n_dr = 2 * NA_WIN_R - 1
    toep = jnp.concatenate([toep, jnp.full((NA_HEADS, 1, GRID_W, GRID_W), NEG_BIG, F32)], axis=1)
    which = np.full((3, NA_ROWS, NA_KROWS), n_dr, np.int32)
    for v, b in enumerate((0, 1, nb - 1)):
        kb0 = min(max(b - 1, 0), nb - 3)
        for a in range(NA_ROWS):
            r = NA_ROWS * b + a
            r0 = min(max(r - NA_WIN_R // 2, 0), rows - NA_WIN_R)
            for i in range(NA_KROWS):
                kr = NA_ROWS * kb0 + i
                if r0 <= kr < r0 + NA_WIN_R:
                    which[v, a, i] = kr - r + NA_WIN_R - 1
    blocks = jnp.take(toep, which.reshape(-1), axis=1)
    blocks = blocks.reshape(NA_HEADS, 3, NA_ROWS, NA_KROWS, GRID_W, GRID_W)
    return blocks.transpose(1, 0, 2, 4, 3, 5).reshape(3, NA_HEADS, NA_ROWS * GRID_W, NA_KROWS * GRID_W)


def _na(p_lat, p_ctx, bias):
    nq = NA_ROWS * GRID_W
    n_lat, n_ctx = p_lat.shape[0], p_ctx.shape[0]
    nb = n_lat // nq

    def kmap(t, col):
        return lambda b: (jnp.clip(b - 1, 0, nb - 3) + t, col)

    def bmap(b):
        return (jnp.where(b == 0, 0, jnp.where(b == nb - 1, 2, 1)), 0, 0, 0)

    blk = lambda f: pl.BlockSpec((nq, BW), f)
    return pl.pallas_call(
        _na_kernel,
        grid=(nb,),
        in_specs=[blk(lambda b: (b, CB_QA)),
                  blk(kmap(0, CB_KA)), blk(kmap(1, CB_KA)), blk(kmap(2, CB_KA)),
                  blk(kmap(0, CB_VA)), blk(kmap(1, CB_VA)), blk(kmap(2, CB_VA)),
                  pl.BlockSpec((n_ctx, BW), lambda b: (0, CB_KA)),
                  pl.BlockSpec((n_ctx, BW), lambda b: (0, CB_VA)),
                  pl.BlockSpec((1, NA_HEADS, nq, NA_KROWS * GRID_W), bmap)],
        out_specs=pl.BlockSpec((nq, BW), lambda b: (b, 0)),
        out_shape=jax.ShapeDtypeStruct((n_lat, BW), BF16),
        compiler_params=_cparams(("arbitrary",)),
        name="na_attn",
    )(p_lat, p_lat, p_lat, p_lat, p_lat, p_lat, p_lat, p_ctx, p_ctx, bias)


def _ctx_dense_kernel(q_ref, k_ref, v_ref, o_ref):
    nq = q_ref.shape[0]
    lane = lax.broadcasted_iota(I32, (nq, LANES), 1)
    for g in range(NA_HEADS // 2):
        sl = slice(g * LANES, (g + 1) * LANES)
        qp = q_ref[:, sl] * 0.125
        kp = k_ref[:, sl]
        vp = v_ref[:, sl]
        outs = []
        for sub in range(2):
            keep = (lane < 64) if sub == 0 else (lane >= 64)
            qz = jnp.where(keep, qp, jnp.zeros_like(qp))
            s = lax.dot_general(qz, kp, _NT, preferred_element_type=F32)
            m = s.max(axis=-1, keepdims=True)
            p = jnp.exp(s - m)
            l = p.sum(axis=-1, keepdims=True)
            outs.append(jnp.dot(p.astype(BF16), vp, preferred_element_type=F32) / l)
        o_ref[:, sl] = jnp.where(lane < 64, outs[0], outs[1]).astype(BF16)


def _ctx_dense(p_ctx):
    n_ctx = p_ctx.shape[0]
    blk = lambda col: pl.BlockSpec((n_ctx, BW), lambda i: (0, col))
    return pl.pallas_call(
        _ctx_dense_kernel,
        grid=(1,),
        in_specs=[blk(CB_QA), blk(CB_KA), blk(CB_VA)],
        out_specs=pl.BlockSpec((n_ctx, BW), lambda i: (0, 0)),
        out_shape=jax.ShapeDtypeStruct((n_ctx, BW), BF16),
        compiler_params=_cparams(("arbitrary",)),
        name="ctx_dense_attn",
    )(p_ctx, p_ctx, p_ctx)


def _diff_kernel(*refs, tq, nk, lam_init, with_lat):
    if with_lat:
        (qt_ref, kc_ref, vtc_ref, k_ref, vt_ref, lq1_ref, lk1_ref, lq2_ref, lk2_ref, g_ref, o_ref,
         qz_scr, m_scr, acc_scr, s0_scr, s1_scr, p0_scr, p1_scr) = refs
    else:
        (qt_ref, kc_ref, vtc_ref, lq1_ref, lk1_ref, lq2_ref, lk2_ref, g_ref, o_ref,
         qz_scr, m_scr, acc_scr, s0_scr, s1_scr, p0_scr, p1_scr) = refs
    s_scr = (s0_scr, s1_scr)
    p_scr = (p0_scr, p1_scr)
    j = pl.program_id(1)
    lanes2 = 2 * tq
    rc = max(BF16_ROWS, BF16_ROWS * 1024 // lanes2)
    ck = 256

    def attend_all(k_ref, vt_ref):
        ns = k_ref.shape[0] // ck

        def head_cols(h):
            return slice(h * DA_VD, (h + 1) * DA_VD)

        pv_every = 1

        def run_round(h_qk, h_exp, h_pv, m_b):
            mx = None
            for i in range(ns):
                rows = slice(i * ck, (i + 1) * ck)
                if h_qk is not None:
                    s = jnp.dot(k_ref[rows, head_cols(h_qk)], qz_scr[h_qk], preferred_element_type=F32)
                    s_scr[h_qk % 2][rows, :] = s
                    part = s.reshape(ck // rc, rc, lanes2).max(axis=0)
                    mx = part if mx is None else jnp.maximum(mx, part)
                if h_exp is not None:
                    for c in range(i * ck // rc, (i + 1) * ck // rc):
                        rr = slice(c * rc, (c + 1) * rc)
                        p_scr[h_exp % 2][rr, :] = jnp.exp2(s_scr[h_exp % 2][rr, :] - m_b).astype(BF16)
                if h_pv is not None and (i + 1) % pv_every == 0:
                    keys = slice((i + 1 - pv_every) * ck, (i + 1) * ck)
                    vt_ext = jnp.concatenate([vt_ref[head_cols(h_pv), keys], vt_ref[BW:VT_ROWS, keys]], axis=0)
                    acc_scr[h_pv] += jnp.dot(vt_ext, p_scr[h_pv % 2][keys, :], preferred_element_type=F32)
            return mx

        mx = None
        for r in range(DA_HEADS + 2):
            h_qk = r if r < DA_HEADS else None
            h_exp = r - 1 if 0 <= r - 1 < DA_HEADS else None
            h_pv = r - 2 if 0 <= r - 2 < DA_HEADS else None
            m_b = None
            if h_exp is not None:
                m_old = m_scr[h_exp]
                m_new = jnp.maximum(m_old, mx.max(axis=0, keepdims=True))
                m_scr[h_exp] = m_new
                acc_scr[h_exp] = jnp.exp2(m_old - m_new) * acc_scr[h_exp]
                m_b = jnp.broadcast_to(m_new, (rc, lanes2))
            mx = run_round(h_qk, h_exp, h_pv, m_b)

    @pl.when(j == 0)
    def _():
        row = lax.broadcasted_iota(I32, (DA_VD, tq), 0)
        for h in range(DA_HEADS):
            qh = qt_ref[h * DA_VD:(h + 1) * DA_VD, :]
            zero = jnp.zeros_like(qh)
            qz_scr[h] = jnp.concatenate([jnp.where(row < DA_DH, qh, zero),
                                         jnp.where(row >= DA_DH, qh, zero)], axis=1)
        m_scr[...] = jnp.full(m_scr.shape, -jnp.inf, F32)
        acc_scr[...] = jnp.zeros(acc_scr.shape, F32)
        attend_all(kc_ref, vtc_ref)

    if with_lat:
        attend_all(k_ref, vt_ref)

    @pl.when(j == nk - 1)
    def _():
        lam = (jnp.exp(jnp.sum(lq1_ref[...] * lk1_ref[...], keepdims=True))
               - jnp.exp(jnp.sum(lq2_ref[...] * lk2_ref[...], keepdims=True)) + lam_init)
        for h in range(DA_HEADS):
            o = acc_scr[h, 0:DA_VD, :] / acc_scr[h, DA_VD:DA_VD + 1, :]
            od = o[:, :tq] - lam * o[:, tq:]
            ms = jnp.mean(od * od, axis=0, keepdims=True)
            y = od * lax.rsqrt(ms + EPS) * g_ref[...] * (1.0 - lam_init)
            o_ref[:, h * DA_VD:(h + 1) * DA_VD] = y.T.astype(BF16)


def _diff(qt, p_ctx, vt_ctx, lat, lam_vecs, g_col, lam_init, tq, tk):
    n_q, n_ctx = qt.shape[1], p_ctx.shape[0]
    with_lat = lat is not None
    nk = lat[0].shape[0] // tk if with_lat else 1
    vec = pl.BlockSpec((1, DA_DH), lambda i, j: (0, 0))
    in_specs = [pl.BlockSpec((BW, tq), lambda i, j: (0, i)),
                pl.BlockSpec((n_ctx, BW), lambda i, j: (0, CB_KB)),
                pl.BlockSpec((VT_ROWS, n_ctx), lambda i, j: (0, 0))]
    args = [qt, p_ctx, vt_ctx]
    if with_lat:
        in_specs += [pl.BlockSpec((tk, BW), lambda i, j: (j, CB_KB)),
                     pl.BlockSpec((VT_ROWS, tk), lambda i, j: (0, j))]
        args += list(lat)
    in_specs += [vec, vec, vec, vec, pl.BlockSpec((DA_VD, 1), lambda i, j: (0, 0))]
    return pl.pallas_call(
        functools.partial(_diff_kernel, tq=tq, nk=nk, lam_init=lam_init, with_lat=with_lat),
        grid=(n_q // tq, nk),
        in_specs=in_specs,
        out_specs=pl.BlockSpec((tq, BW), lambda i, j: (i, 0)),
        out_shape=jax.ShapeDtypeStruct((n_q, BW), BF16),
        scratch_shapes=[pltpu.VMEM((DA_HEADS, DA_VD, 2 * tq), BF16),
                        pltpu.VMEM((DA_HEADS, 1, 2 * tq), F32),
                        pltpu.VMEM((DA_HEADS, DA_VD + BF16_ROWS, 2 * tq), F32),
                        pltpu.VMEM((max(tk, n_ctx), 2 * tq), F32),
                        pltpu.VMEM((max(tk, n_ctx), 2 * tq), F32),
                        pltpu.VMEM((max(tk, n_ctx), 2 * tq), BF16),
                        pltpu.VMEM((max(tk, n_ctx), 2 * tq), BF16)],
        compiler_params=_cparams(("arbitrary", "arbitrary")),
        name="diff_attn",
    )(*args, *lam_vecs, g_col)


def _merge_kernel(x_ref, ya_ref, yb_ref, u_ref, v_ref, ga_ref, gb_ref, gc_ref, mod_ref, wb_ref, wo_ref,
                  lng_ref, lnb_ref, ws_ref, bst_ref, g2_ref, wrt_ref,
                  x1_ref, h2_ref, aff_ref, yc_scr, *, mrow, tm):
    ug = _gelu(u_ref[...].astype(F32))
    vg = _gelu(v_ref[...].astype(F32))
    mu = jnp.mean(vg, axis=-1, keepdims=True)
    var = jnp.mean(jnp.square(vg - mu), axis=-1, keepdims=True)
    vn = ((vg - mu) * lax.rsqrt(var + EPS) * lng_ref[...] + lnb_ref[...]).astype(BF16)
    for ci in range(tm // GM_CHUNK):
        rs = slice(ci * GM_CHUNK, (ci + 1) * GM_CHUNK)
        for g in range(GM_GROUPS):
            cs = slice(g * LANES, (g + 1) * LANES)
            mixed = jnp.dot(ws_ref[g], vn[rs, cs], preferred_element_type=F32) + bst_ref[:, g:g + 1]
            yc_scr[rs, cs] = (ug[rs, cs] * mixed).astype(BF16)

    sig = jax.nn.sigmoid
    m = sig(ga_ref[...].astype(F32)) * jnp.dot(ya_ref[...], wb_ref[0], preferred_element_type=F32)
    m = m + sig(gb_ref[...].astype(F32)) * jnp.dot(yb_ref[...], wb_ref[1], preferred_element_type=F32)
    m = m + sig(gc_ref[...].astype(F32)) * jnp.dot(yc_scr[...], wb_ref[2], preferred_element_type=F32)
    y = jnp.dot(m.astype(BF16), wo_ref[...], preferred_element_type=F32)
    gt1 = mod_ref[mrow:mrow + 1, 2 * D:3 * D]
    x1 = x_ref[...] + gt1 * y
    x1_ref[...] = x1

    ms = jnp.mean(x1 * x1, axis=-1, keepdims=True)
    sh2 = mod_ref[mrow:mrow + 1, 3 * D:4 * D]
    sc2 = mod_ref[mrow:mrow + 1, 4 * D:5 * D]
    h2 = x1 * lax.rsqrt(ms + EPS) * g2_ref[...] * (1.0 + sc2) + sh2
    for s in range(D // LANES):
        h2_ref[pl.ds(s, tm, stride=D // LANES), :] = h2[:, s * LANES:(s + 1) * LANES]

    hh = h2.astype(BF16)
    hl = (h2 - hh.astype(F32)).astype(BF16)
    w = wrt_ref[...]
    wh = w.astype(BF16)
    wl = (w - wh.astype(F32)).astype(BF16)
    lg = (lax.dot_general(wh, hh, _NT, preferred_element_type=F32)
          + lax.dot_general(wh, hl, _NT, preferred_element_type=F32)
          + lax.dot_general(wl, hh, _NT, preferred_element_type=F32))
    e = jnp.exp(lg - lg.max(axis=0, keepdims=True))
    aff_ref[...] = e / e.sum(axis=0, keepdims=True)


def _merge(x, ya, yb, p_all, mod, mrow, wb, wo, lng, lnb, ws, bst, g2, wrt, tm):
    n = x.shape[0]
    const = lambda shape: pl.BlockSpec(shape, lambda i: (0,) * len(shape))
    return pl.pallas_call(
        functools.partial(_merge_kernel, mrow=mrow, tm=tm),
        grid=(n // tm,),
        in_specs=[pl.BlockSpec((tm, D), lambda i: (i, 0)),
                  pl.BlockSpec((tm, BW), lambda i: (i, 0)),
                  pl.BlockSpec((tm, BW), lambda i: (i, 0)),
                  pl.BlockSpec((tm, BW), lambda i: (i,CB_U)),
                  pl.BlockSpec((tm, BW), lambda i: (i,CB_V)),
                  pl.BlockSpec((tm, D), lambda i: (i,4)),
                  pl.BlockSpec((tm, D), lambda i: (i,5)),
                  pl.BlockSpec((tm, D), lambda i: (i,6)),
                  const((SUBLANES, N_MOD * D)),
                  const((3, BW, D)), const((D, D)),
                  const((1, BW)), const((1, BW)),
                  const((GM_GROUPS, GM_CHUNK, GM_CHUNK)), const((GM_CHUNK, GM_GROUPS)),
                  const((1, D)), const((N_EXPERTS, D))],
        out_specs=[pl.BlockSpec((tm, D), lambda i: (i, 0)),
                   pl.BlockSpec((tm * (D // LANES), LANES), lambda i: (i, 0)),
                   pl.BlockSpec((N_EXPERTS, tm), lambda i: (0, i))],
        out_shape=[jax.ShapeDtypeStruct((n, D), F32),
                   jax.ShapeDtypeStruct((n * (D // LANES), LANES), F32),
                   jax.ShapeDtypeStruct((N_EXPERTS, n), F32)],
        scratch_shapes=[pltpu.VMEM((tm, BW), BF16)],
        compiler_params=_cparams(("arbitrary",)),
        name="merge_prenorm",
    )(x, ya, yb, p_all, p_all, p_all, p_all, p_all, mod, wb, wo, lng, lnb, ws, bst, g2, wrt)


def _route_kernel(all_ref, a_ref, idx_ref, gate_ref, slot_ref, rank_ref, thr_scr, *, cap, nrow):
    e = pl.program_id(0)

    @pl.when(e == 0)
    def _():
        bits = pltpu.bitcast(all_ref[...], I32).reshape(N_EXPERTS, nrow, LANES)

        def bisect(k, prefix):
            cand = prefix | jnp.left_shift(jnp.int32(1), 30 - k)
            hit = jnp.where(bits >= cand, 1.0, 0.0)
            cnt = hit.sum(axis=1, keepdims=True).sum(axis=2, keepdims=True)
            return jnp.where(cnt >= cap, cand, prefix)

        thr_bits = lax.fori_loop(0, 31, bisect, jnp.zeros((N_EXPERTS, 1, 1), I32))
        tb = jnp.broadcast_to(thr_bits, (N_EXPERTS, SUBLANES, LANES)).reshape(N_EXPERTS * SUBLANES, LANES)
        thr_scr[...] = pltpu.bitcast(tb, F32).reshape(N_EXPERTS, SUBLANES, LANES)

    a = a_ref[0]
    thr = thr_scr[e][0:1, :]
    gt = a > thr
    eq = a == thr

    r_i = lax.broadcasted_iota(I32, (LANES, LANES), 0)
    c_i = lax.broadcasted_iota(I32, (LANES, LANES), 1)
    upper = (r_i <= c_i).astype(BF16)
    rr = lax.broadcasted_iota(I32, (nrow, nrow), 0)
    rc = lax.broadcasted_iota(I32, (nrow, nrow), 1)
    lower_strict = (rc < rr).astype(BF16)
    upper_strict = (rr < rc).astype(BF16)

    def prefix(mask):
        xf = jnp.where(mask, 1.0, 0.0)
        incl = jnp.dot(xf.astype(BF16), upper, preferred_element_type=F32)
        tot = jnp.broadcast_to(incl[:, LANES - 1:LANES], (nrow, LANES))
        base = jnp.dot(lower_strict, tot.astype(BF16), preferred_element_type=F32)
        return xf, incl, base + incl - xf

    _, _, eq_rank = prefix(eq)
    need = cap - jnp.sum(jnp.where(gt, 1.0, 0.0))
    sel = gt | (eq & (eq_rank < need))
    xf, incl, rank = prefix(sel)
    rank_ref[0] = rank.astype(I32)
    slot_ref[0] = jnp.where(sel, rank, -1.0).astype(I32)

    ones = jnp.ones((SUBLANES, LANES), BF16)
    tot_l = lax.dot_general(ones, xf.astype(BF16), _NT, preferred_element_type=F32)
    off_l = jnp.dot(tot_l.astype(BF16), upper_strict, preferred_element_type=F32)[0:1]
    tot_l = tot_l[0:1]

    jf = lax.broadcasted_iota(I32, (cap, nrow), 0).astype(F32)
    oh_row = (off_l <= jf) & (jf < off_l + tot_l)
    ohb = jnp.where(oh_row, 1.0, 0.0).astype(BF16)
    row_id = lax.broadcasted_iota(I32, (cap, nrow), 1).astype(F32)
    off_j = jnp.sum(jnp.where(oh_row, off_l, 0.0), axis=1, keepdims=True)
    row_j = jnp.sum(jnp.where(oh_row, row_id, 0.0), axis=1, keepdims=True)
    key = jnp.where(sel, incl, 0.0).astype(BF16)
    g = jnp.dot(ohb, key, preferred_element_type=F32)
    target = lax.broadcasted_iota(I32, (cap, 1), 0).astype(F32) - off_j + 1.0
    oh_lane = g == target
    lane_id = lax.broadcasted_iota(I32, (cap, LANES), 1).astype(F32)
    lane_j = jnp.sum(jnp.where(oh_lane, lane_id, 0.0), axis=1, keepdims=True)
    idx_ref[0] = (row_j * LANES + lane_j).astype(I32)

    a1 = a.astype(BF16)
    r1 = a - a1.astype(F32)
    a2 = r1.astype(BF16)
    a3 = (r1 - a2.astype(F32)).astype(BF16)
    arow = (jnp.dot(ohb, a1, preferred_element_type=F32) + jnp.dot(ohb, a2, preferred_element_type=F32)
            + jnp.dot(ohb, a3, preferred_element_type=F32))
    gate_ref[0] = jnp.sum(jnp.where(oh_lane, arow, 0.0), axis=1, keepdims=True)


def _route(aff3, cap):
    nrow = aff3.shape[1]
    return pl.pallas_call(
        functools.partial(_route_kernel, cap=cap, nrow=nrow),
        grid=(N_EXPERTS,),
        in_specs=[pl.BlockSpec((N_EXPERTS * nrow, LANES), lambda e: (0, 0)),
                  pl.BlockSpec((1, nrow, LANES), lambda e: (e, 0, 0))],
        out_specs=[pl.BlockSpec((1, cap, 1), lambda e: (e, 0, 0)),
                   pl.BlockSpec((1, cap, 1), lambda e: (e, 0, 0)),
                   pl.BlockSpec((1, nrow, LANES), lambda e: (e, 0, 0)),
                   pl.BlockSpec((1, nrow, LANES), lambda e: (e, 0, 0))],
        out_shape=[jax.ShapeDtypeStruct((N_EXPERTS, cap, 1), I32),
                   jax.ShapeDtypeStruct((N_EXPERTS, cap, 1), F32),
                   jax.ShapeDtypeStruct((N_EXPERTS, nrow, LANES), I32),
                   jax.ShapeDtypeStruct((N_EXPERTS, nrow, LANES), I32)],
        scratch_shapes=[pltpu.VMEM((N_EXPERTS, SUBLANES, LANES), F32)],
        compiler_params=_cparams(("arbitrary",)),
        name="ec_route",
    )(aff3.reshape(N_EXPERTS * nrow, LANES), aff3)


def _ffn_kernel(idx_ref, h2_hbm, gate_ref, w1_ref, w3_ref, w2_ref, ye_ref, buf, sem, wb_scr, *, tc, nc):
    sub = D // LANES

    @pl.when(pl.program_id(1) == 0)
    def _():
        wb_scr[0] = w1_ref[0, 0].astype(BF16)
        wb_scr[1] = w3_ref[0, 0].astype(BF16)
        wb_scr[2] = w2_ref[0, 0].astype(BF16)

    step = pl.program_id(0) * nc + pl.program_id(1)
    nsteps = N_EXPERTS * nc

    def row_copy(tok, r, slot):
        return pltpu.make_async_copy(h2_hbm.at[pl.ds(pl.multiple_of(tok * sub, sub), sub), :],
                                     buf.at[slot, pl.ds(pl.multiple_of(r * sub, sub), sub), :],
                                     sem.at[slot])

    def wait_slot(slot):
        pltpu.make_async_copy(h2_hbm.at[pl.ds(0, tc * sub), :], buf.at[slot], sem.at[slot]).wait()

    @pl.when(step == 0)
    def _():
        def body(r, carry):
            row_copy(idx_ref[0, r], r, 0).start()
            return carry

        lax.fori_loop(0, tc, body, 0, unroll=16)

    nxt = jnp.minimum(step + 1, nsteps - 1)
    nxt_e = nxt // nc
    nxt_base = (nxt % nc) * tc
    nslot = (step + 1) % 2
    parts = 4 if tc % 4 == 0 else 1

    def issue_part(q):
        for r in range(q * tc // parts, (q + 1) * tc // parts):
            row_copy(idx_ref[nxt_e, nxt_base + r], r, nslot).start(priority=r % 2)

    slot = step % 2
    wait_slot(slot)
    xin = jnp.concatenate([buf[slot, pl.ds(s, tc, stride=sub), :] for s in range(sub)], axis=1).astype(BF16)
    issue_part(0)
    a = jnp.dot(xin, wb_scr[0], preferred_element_type=F32)
    if parts == 4:
        issue_part(1)
    b = jnp.dot(xin, wb_scr[1], preferred_element_type=F32)
    hid = (a * jax.nn.sigmoid(a) * b).astype(BF16)
    if parts == 4:
        issue_part(2)
    y = jnp.dot(hid, wb_scr[2], preferred_element_type=F32)
    if parts == 4:
        issue_part(3)
    ye_ref[0] = (y * gate_ref[0]).astype(BF16)

    @pl.when(step == nsteps - 1)
    def _():
        wait_slot(nslot)


def _ffn(idx, gate, h2, w1, w3, w2, layer, tc):
    cap = idx.shape[1]
    nc = cap // tc
    sub = D // LANES
    wspec = pl.BlockSpec((1, 1, D, D), lambda e, c, idx: (layer, e, 0, 0))
    grid_spec = pltpu.PrefetchScalarGridSpec(
        num_scalar_prefetch=1,
        grid=(N_EXPERTS, nc),
        in_specs=[pl.BlockSpec(memory_space=pl.ANY),
                  pl.BlockSpec((1, tc, 1), lambda e, c, idx: (e, c, 0)),
                  wspec, wspec, wspec],
        out_specs=pl.BlockSpec((1, tc, D), lambda e, c, idx: (e, c, 0)),
        scratch_shapes=[pltpu.VMEM((2, tc * sub, LANES), F32),
                        pltpu.SemaphoreType.DMA((2,)),
                        pltpu.VMEM((3, D, D), BF16)])
    return pl.pallas_call(
        functools.partial(_ffn_kernel, tc=tc, nc=nc),
        grid_spec=grid_spec,
        out_shape=jax.ShapeDtypeStruct((N_EXPERTS, cap, D), BF16),
        compiler_params=_cparams(("arbitrary", "arbitrary")),
        name="expert_ffn",
    )(idx, h2, gate, w1, w3, w2)


def _combine_kernel(rs_ref, x_ref, slot_ref, mod_ref, gf_ref, ye_hbm, o_ref, win, sem, y_scr,
                    *, win_rows, cap, nt, tt, mrow, final):
    i = pl.program_id(0)

    def start_of(e, t):
        s0 = rs_ref[e, t]
        s_al = (s0 // BF16_ROWS) * BF16_ROWS
        return pl.multiple_of(jnp.minimum(s_al, cap - win_rows), BF16_ROWS)

    def copy(e, t, sl):
        return pltpu.make_async_copy(ye_hbm.at[e, pl.ds(start_of(e, t), win_rows), :],
                                     win.at[sl, e], sem.at[sl])

    @pl.when(i == 0)
    def _():
        for e in range(N_EXPERTS):
            copy(e, i, 0).start()

    @pl.when(i + 1 < nt)
    def _():
        for e in range(N_EXPERTS):
            copy(e, i + 1, (i + 1) % 2).start()

    sl = i % 2
    for e in range(N_EXPERTS):
        copy(e, i, sl).wait()

    def window_sum(r0, r1):
        acc = jnp.zeros((tt, D), F32)
        wi = lax.broadcasted_iota(I32, (r1 - r0, tt), 0) + r0
        for e in range(N_EXPERTS):
            rel = slot_ref[e:e + 1, :] - start_of(e, i)
            oh_t = jnp.where(wi == rel, 1.0, 0.0).astype(BF16)
            acc = acc + lax.dot_general(oh_t, win[sl, e, r0:r1, :], _TN, preferred_element_type=F32)
        return acc

    main_rows = min(win_rows, tt)
    gt2 = mod_ref[mrow:mrow + 1, 5 * D:6 * D]
    y_scr[...] = x_ref[...] + gt2 * window_sum(0, main_rows)
    if win_rows > main_rows:
        need_tail = False
        for e in range(N_EXPERTS):
            end = rs_ref[e, i + 1] - start_of(e, i)
            need_tail = jnp.logical_or(need_tail, end > main_rows)

        @pl.when(need_tail)
        def _():
            y_scr[...] += gt2 * window_sum(main_rows, win_rows)

    y = y_scr[...]
    if final:
        ms = jnp.mean(y * y, axis=-1, keepdims=True)
        y = y * lax.rsqrt(ms + EPS) * gf_ref[...]
    o_ref[...] = y


def _combine(rstart, x1, slot, mod, mrow, g_final, ye, tt, final):
    n = x1.shape[0]
    cap = ye.shape[1]
    nt = n // tt
    win_rows = min(cap, tt + BF16_ROWS)
    grid_spec = pltpu.PrefetchScalarGridSpec(
        num_scalar_prefetch=1,
        grid=(nt,),
        in_specs=[pl.BlockSpec((tt, D), lambda i, rs: (i, 0)),
                  pl.BlockSpec((N_EXPERTS, tt), lambda i, rs: (0, i)),
                  pl.BlockSpec((SUBLANES, N_MOD * D), lambda i, rs: (0, 0)),
                  pl.BlockSpec((1, D), lambda i, rs: (0, 0)),
                  pl.BlockSpec(memory_space=pl.ANY)],
        out_specs=pl.BlockSpec((tt, D), lambda i, rs: (i, 0)),
        scratch_shapes=[pltpu.VMEM((2, N_EXPERTS, win_rows, D), BF16),
                        pltpu.SemaphoreType.DMA((2,)),
                        pltpu.VMEM((tt, D), F32)])
    return pl.pallas_call(
        functools.partial(_combine_kernel, win_rows=win_rows, cap=cap, nt=nt, tt=tt, mrow=mrow, final=final),
        grid_spec=grid_spec,
        out_shape=jax.ShapeDtypeStruct((n, D), F32),
        compiler_params=_cparams(("arbitrary",)),
        name="moe_combine",
    )(rstart, x1, slot, mod, g_final, ye)


def _rope_tables(n_lat):
    rows = n_lat // GRID_W
    inv = ROPE_THETA ** (-jnp.arange(0, ROPE_AXIS, 2, dtype=F32) / ROPE_AXIS)
    ang_r = jnp.arange(rows).astype(F32)[:, None] * inv
    ang_c = jnp.arange(GRID_W).astype(F32)[:, None] * inv
    cr, sr, cc, sc = jnp.cos(ang_r), jnp.sin(ang_r), jnp.cos(ang_c), jnp.sin(ang_c)
    zr, zc = jnp.zeros_like(sr), jnp.zeros_like(sc)
    rep = lambda parts: jnp.concatenate(parts * (LANES // DA_DH), axis=1)

    def table(row_parts, col_parts):
        r = rep([row_parts[0], row_parts[1], zr, zr])
        c = rep([zc, zc, col_parts[0], col_parts[1]])
        return (r[:, None, :] + c[None, :, :]).reshape(n_lat, LANES)

    return (table((cr, cr), (cc, cc)), table((zr, sr), (zc, sc)), table((-sr, zr), (-sc, zc)))


def _largest_tile(n, unit, limit):
    best = unit
    for k in range(1, n // unit + 1):
        if n % (k * unit) == 0 and k * unit <= limit:
            best = k * unit
    return best


def _moe(x1, h2, aff_t, mod, mrow, w1, w3, w2, layer, g_final, final):
    n = x1.shape[0]
    cap = max(1, EC_CAPACITY * n // N_EXPERTS)
    tile = LANES * LANES if n <= LANES * LANES else n
    n_pad = max(n, tile)
    if n_pad > n:
        aff_t = jnp.concatenate([aff_t, jnp.full((N_EXPERTS, n_pad - n), -1.0, F32)], axis=1)
    idx, gate, slot, rank = _route(aff_t.reshape(N_EXPERTS, n_pad // LANES, LANES), cap)
    tt = min(256, n)
    slot = slot.reshape(N_EXPERTS, n_pad)[:, :n]
    rstart = jnp.concatenate([rank.reshape(N_EXPERTS, n_pad)[:, 0:n:tt],
                              jnp.full((N_EXPERTS, 1), cap, I32)], axis=1)
    tc = min(cap, 512)
    ye = _ffn(idx.reshape(N_EXPERTS, cap), gate, h2, w1, w3, w2, layer, tc)
    return _combine(rstart, x1, slot, mod, mrow, g_final, ye, tt, final)


def kernel(x, c, ctx, c_ctx, w_ada, b_ada, g_norm1, g_norm2, w_in, na_rpb, da_lam_q1, da_lam_k1, da_lam_q2,
           da_lam_k2, da_subln_g, gm_ln_g, gm_ln_b, gm_w_s, gm_b_s, w_branch, w_out, w_router, w_e1, w_e3,
           w_e2, g_final):
    depth = w_ada.shape[0]
    n_lat, n_ctx = x.shape[1], ctx.shape[1]
    rows = n_lat // GRID_W
    xs, xc = x[0], ctx[0]

    cc = jnp.concatenate([c.reshape(1, D), c_ctx.reshape(1, D), jnp.zeros((SUBLANES - 2, D), F32)], axis=0)
    mods = _ada(cc, w_ada, b_ada)

    tabs_lat = _rope_tables(n_lat)
    tabs_ctx = (jnp.ones((n_ctx, LANES), F32), jnp.zeros((n_ctx, LANES), F32), jnp.zeros((n_ctx, LANES), F32))
    tm_proj = _largest_tile(n_lat, 256, 1024)
    tk = _largest_tile(n_lat, 256, 2048)
    tq = 512

    for i in range(depth):
        last = i == depth - 1
        lam_init = 0.8 - 0.6 * math.exp(-0.3 * i)
        mod = mods[i]
        w_in_b = w_in[i].astype(BF16)
        g1 = g_norm1[i].reshape(1, D)
        g2 = g_norm2[i].reshape(1, D)
        gf = g_final.reshape(1, D)
        lam_vecs = [v[i].reshape(1, DA_DH).astype(F32) for v in (da_lam_q1, da_lam_k1, da_lam_q2, da_lam_k2)]
        g_col = da_subln_g[i].reshape(DA_VD, 1)
        merge_w = (w_branch[i].astype(BF16), w_out[i].astype(BF16), gm_ln_g[i].reshape(1, BW),
                   gm_ln_b[i].reshape(1, BW), gm_w_s[i].astype(BF16), gm_b_s[i].T, g2, w_router[i].T)

        p_lat, qt_lat, vt_lat = _proj(xs, g1, mod, w_in_b, tabs_lat, 0, tm_proj)
        p_ctx, qt_ctx, vt_ctx = _proj(xc, g1, mod, w_in_b, tabs_ctx, 1, n_ctx)

        ya = _na(p_lat, p_ctx, _na_bias(na_rpb[i], rows))
        yb = _diff(qt_lat, p_ctx, vt_ctx, (p_lat, vt_lat), lam_vecs, g_col, lam_init, tq, tk)
        x1, h2, aff_t = _merge(xs, ya, yb, p_lat, mod, 0, *merge_w, tm=_largest_tile(n_lat, 256, 512))
        xs = _moe(x1, h2, aff_t, mod, 0, w_e1, w_e3, w_e2, i, gf, last)

        if not last:
            yac = _ctx_dense(p_ctx)
            ybc = _diff(qt_ctx, p_ctx, vt_ctx, None, lam_vecs, g_col, lam_init, n_ctx, n_ctx)
            x1c, h2c, aff_tc = _merge(xc, yac, ybc, p_ctx, mod, 1, *merge_w, tm=n_ctx)
            xc = _moe(x1c, h2c, aff_tc, mod, 1, w_e1, w_e3, w_e2, i, gf, False)
    return xs[None]
```

```python
import functools
import math

import numpy as np
import jax
import jax.numpy as jnp
from jax import lax
from jax.experimental import pallas as pl
from jax.experimental.pallas import tpu as pltpu

F32 = jnp.float32
BF16 = jnp.bfloat16
I32 = jnp.int32

D = 1024
GRID_W = 64
BW = 512
N_COLBLK = 14
NA_HEADS = 8
NA_WIN_R = 8
NA_WIN_C = 16
NA_ROWS = 4
NA_KROWS = 12
DA_HEADS = 4
DA_DH = 64
DA_VD = 128
GM_GROUPS = 4
GM_CHUNK = 128
N_EXPERTS = 16
EC_CAPACITY = 2
ROPE_THETA = 10000.0
ROPE_AXIS = DA_DH // 2
N_MOD = 6
EPS = 1e-6
LANES = 128
SUBLANES = 8
BF16_ROWS = 16
NEG_BIG = -1e30
VT_ROWS = BW + BF16_ROWS
Q_SCALE_LOG2 = DA_DH ** -0.5 * math.log2(math.e)
VMEM_LIMIT = 56 * 1024 * 1024

CB_KA, CB_VA, CB_KB, CB_VB, CB_QA, CB_QB, CB_U, CB_V = range(8)

_NT = (((1,), (1,)), ((), ()))
_TN = (((0,), (0,)), ((), ()))


def _cparams(sem, flags=None):
    return pltpu.CompilerParams(dimension_semantics=sem, vmem_limit_bytes=VMEM_LIMIT, flags=flags)


def _gelu(x):
    return 0.5 * x * (1.0 + jnp.tanh(math.sqrt(2.0 / math.pi) * (x + 0.044715 * (x * x * x))))


def _ada_kernel(c_ref, w_ref, b_ref, o_ref):
    c = c_ref[...]
    s = c * jax.nn.sigmoid(c)
    o_ref[0] = jnp.dot(s, w_ref[0], preferred_element_type=F32,
                       precision=lax.Precision.HIGHEST) + b_ref[0]


def _ada(cc, w_ada, b_ada):
    depth = w_ada.shape[0]
    tn = 512
    return pl.pallas_call(
        _ada_kernel,
        grid=(depth, N_MOD * D // tn),
        in_specs=[pl.BlockSpec((SUBLANES, D), lambda i, j: (0, 0)),
                  pl.BlockSpec((1, D, tn), lambda i, j: (i, 0, j)),
                  pl.BlockSpec((1, 1, tn), lambda i, j: (i, 0, j))],
        out_specs=pl.BlockSpec((1, SUBLANES, tn), lambda i, j: (i, 0, j)),
        out_shape=jax.ShapeDtypeStruct((depth, SUBLANES, N_MOD * D), F32),
        compiler_params=_cparams(("arbitrary", "arbitrary")),
        name="ada_mod",
    )(cc, w_ada, b_ada.reshape(depth, 1, N_MOD * D))


def _proj_kernel(x_ref, g_ref, mod_ref, w_ref, c_ref, sp_ref, sm_ref, p_ref, qt_ref, vt_ref, h_scr, *, mrow):
    j = pl.program_id(1)

    @pl.when(j == 0)
    def _():
        x = x_ref[...]
        ms = jnp.mean(x * x, axis=-1, keepdims=True)
        xn = x * lax.rsqrt(ms + EPS) * g_ref[...]
        sh = mod_ref[mrow:mrow + 1, 0:D]
        sc = mod_ref[mrow:mrow + 1, D:2 * D]
        h_scr[...] = (xn * (1.0 + sc) + sh).astype(BF16)

    z = jnp.dot(h_scr[...], w_ref[...], preferred_element_type=F32)

    def rope(z):
        reps = BW // LANES
        c = jnp.concatenate([c_ref[...]] * reps, axis=1)
        sp = jnp.concatenate([sp_ref[...]] * reps, axis=1)
        sm = jnp.concatenate([sm_ref[...]] * reps, axis=1)
        half = ROPE_AXIS // 2
        return z * c + pltpu.roll(z, half, 1) * sp + pltpu.roll(z, BW - half, 1) * sm

    @pl.when(j == CB_KB // 2)
    def _():
        p_ref[:, 0:BW] = rope(z[:, 0:BW]).astype(BF16)
        zv = z[:, BW:2 * BW]
        p_ref[:, BW:2 * BW] = zv.astype(BF16)
        vt_ref[0:BW, :] = zv.T.astype(BF16)
        vt_ref[BW:VT_ROWS, :] = jnp.ones((VT_ROWS - BW, z.shape[0]), BF16)

    @pl.when(j == CB_QB // 2)
    def _():
        p_ref[:, 0:BW] = z[:, 0:BW].astype(BF16)
        zr = rope(z[:, BW:2 * BW])
        p_ref[:, BW:2 * BW] = zr.astype(BF16)
        qt_ref[...] = (zr * Q_SCALE_LOG2).T.astype(BF16)

    @pl.when((j != CB_KB // 2) & (j != CB_QB // 2))
    def _():
        p_ref[...] = z.astype(BF16)


def _proj(x, g, mod, w, tabs, mrow, tm):
    n = x.shape[0]
    return pl.pallas_call(
        functools.partial(_proj_kernel, mrow=mrow),
        grid=(n // tm, N_COLBLK // 2),
        in_specs=[pl.BlockSpec((tm, D), lambda i, j: (i, 0)),
                  pl.BlockSpec((1, D), lambda i, j: (0, 0)),
                  pl.BlockSpec((SUBLANES, N_MOD * D), lambda i, j: (0, 0)),
                  pl.BlockSpec((D, 2 * BW), lambda i, j: (0, j)),
                  pl.BlockSpec((tm, LANES), lambda i, j: (i, 0)),
                  pl.BlockSpec((tm, LANES), lambda i, j: (i, 0)),
                  pl.BlockSpec((tm, LANES), lambda i, j: (i, 0))],
        out_specs=[pl.BlockSpec((tm, 2 * BW), lambda i, j: (i, j)),
                   pl.BlockSpec((BW, tm), lambda i, j: (0, i)),
                   pl.BlockSpec((VT_ROWS, tm), lambda i, j: (0, i))],
        out_shape=[jax.ShapeDtypeStruct((n, N_COLBLK * BW), BF16),
                   jax.ShapeDtypeStruct((BW, n), BF16),
                   jax.ShapeDtypeStruct((VT_ROWS, n), BF16)],
        scratch_shapes=[pltpu.VMEM((tm, D), BF16)],
        compiler_params=_cparams(("arbitrary", "arbitrary")),
        name="proj",
    )(x, g, mod, w, *tabs)


def _na_kernel(q_ref, k0_ref, k1_ref, k2_ref, v0_ref, v1_ref, v2_ref, kc_ref, vc_ref, bias_ref, o_ref):
    nq = q_ref.shape[0]
    lane = lax.broadcasted_iota(I32, (nq, LANES), 1)
    k_refs = (kc_ref, k0_ref, k1_ref, k2_ref)
    v_refs = (vc_ref, v0_ref, v1_ref, v2_ref)
    for g in range(NA_HEADS // 2):
        sl = slice(g * LANES, (g + 1) * LANES)
        qp = q_ref[:, sl] * 0.125
        ks = [r[:, sl] for r in k_refs]
        vs = [r[:, sl] for r in v_refs]
        outs = []
        for sub in range(2):
            h = 2 * g + sub
            keep = (lane < 64) if sub == 0 else (lane >= 64)
            qz = jnp.where(keep, qp, jnp.zeros_like(qp))
            ss = [lax.dot_general(qz, ks[0], _NT, preferred_element_type=F32)]
            for t in range(3):
                s = lax.dot_general(qz, ks[1 + t], _NT, preferred_element_type=F32)
                ss.append(s + bias_ref[0, h, :, t * nq:(t + 1) * nq])
            m = ss[0].max(axis=-1, keepdims=True)
            for s in ss[1:]:
                m = jnp.maximum(m, s.max(axis=-1, keepdims=True))
            l = jnp.zeros_like(m)
            o = jnp.zeros((nq, LANES), F32)
            for s, v in zip(ss, vs):
                p = jnp.exp(s - m)
                l = l + p.sum(axis=-1, keepdims=True)
                o = o + jnp.dot(p.astype(BF16), v, preferred_element_type=F32)
            outs.append(o / l)
        o_ref[:, sl] = jnp.where(lane < 64, outs[0], outs[1]).astype(BF16)


def _na_bias(rpb, rows):
    nb = rows // NA_ROWS
    c = np.arange(GRID_W)[:, None]
    kc = np.arange(GRID_W)[None, :]
    c0 = np.clip(c - NA_WIN_C // 2, 0, GRID_W - NA_WIN_C)
    col_ok = (kc >= c0) & (kc < c0 + NA_WIN_C)
    dc = kc - c + (NA_WIN_C - 1)
    sel = np.stack([(dc == d) & col_ok for d in range(2 * NA_WIN_C - 1)]).astype(np.float32)
    toep = jnp.einsum("hrd,dck->hrck", rpb.astype(F32), sel, precision=lax.Precision.HIGHEST)
    toep = toep + np.where(col_ok, 0.0, NEG_BIG).astype(np.float32)
    n_dr = 2 * NA_WIN_R - 1
    toep = jnp.concatenate([toep, jnp.full((NA_HEADS, 1, GRID_W, GRID_W), NEG_BIG, F32)], axis=1)
    which = np.full((3, NA_ROWS, NA_KROWS), n_dr, np.int32)
    for v, b in enumerate((0, 1, nb - 1)):
        kb0 = min(max(b - 1, 0), nb - 3)
        for a in range(NA_ROWS):
            r = NA_ROWS * b + a
            r0 = min(max(r - NA_WIN_R // 2, 0), rows - NA_WIN_R)
            for i in range(NA_KROWS):
                kr = NA_ROWS * kb0 + i
                if r0 <= kr < r0 + NA_WIN_R:
                    which[v, a, i] = kr - r + NA_WIN_R - 1
    blocks = jnp.take(toep, which.reshape(-1), axis=1)
    blocks = blocks.reshape(NA_HEADS, 3, NA_ROWS, NA_KROWS, GRID_W, GRID_W)
    return blocks.transpose(1, 0, 2, 4, 3, 5).reshape(3, NA_HEADS, NA_ROWS * GRID_W, NA_KROWS * GRID_W)


def _na(p_lat, p_ctx, bias):
    nq = NA_ROWS * GRID_W
    n_lat, n_ctx = p_lat.shape[0], p_ctx.shape[0]
    nb = n_lat // nq

    def kmap(t, col):
        return lambda b: (jnp.clip(b - 1, 0, nb - 3) + t, col)

    def bmap(b):
        return (jnp.where(b == 0, 0, jnp.where(b == nb - 1, 2, 1)), 0, 0, 0)

    blk = lambda f: pl.BlockSpec((nq, BW), f)
    return pl.pallas_call(
        _na_kernel,
        grid=(nb,),
        in_specs=[blk(lambda b: (b, CB_QA)),
                  blk(kmap(0, CB_KA)), blk(kmap(1, CB_KA)), blk(kmap(2, CB_KA)),
                  blk(kmap(0, CB_VA)), blk(kmap(1, CB_VA)), blk(kmap(2, CB_VA)),
                  pl.BlockSpec((n_ctx, BW), lambda b: (0, CB_KA)),
                  pl.BlockSpec((n_ctx, BW), lambda b: (0, CB_VA)),
                  pl.BlockSpec((1, NA_HEADS, nq, NA_KROWS * GRID_W), bmap)],
        out_specs=pl.BlockSpec((nq, BW), lambda b: (b, 0)),
        out_shape=jax.ShapeDtypeStruct((n_lat, BW), BF16),
        compiler_params=_cparams(("arbitrary",)),
        name="na_attn",
    )(p_lat, p_lat, p_lat, p_lat, p_lat, p_lat, p_lat, p_ctx, p_ctx, bias)


def _ctx_dense_kernel(q_ref, k_ref, v_ref, o_ref):
    nq = q_ref.shape[0]
    lane = lax.broadcasted_iota(I32, (nq, LANES), 1)
    for g in range(NA_HEADS // 2):
        sl = slice(g * LANES, (g + 1) * LANES)
        qp = q_ref[:, sl] * 0.125
        kp = k_ref[:, sl]
        vp = v_ref[:, sl]
        outs = []
        for sub in range(2):
            keep = (lane < 64) if sub == 0 else (lane >= 64)
            qz = jnp.where(keep, qp, jnp.zeros_like(qp))
            s = lax.dot_general(qz, kp, _NT, preferred_element_type=F32)
            m = s.max(axis=-1, keepdims=True)
            p = jnp.exp(s - m)
            l = p.sum(axis=-1, keepdims=True)
            outs.append(jnp.dot(p.astype(BF16), vp, preferred_element_type=F32) / l)
        o_ref[:, sl] = jnp.where(lane < 64, outs[0], outs[1]).astype(BF16)


def _ctx_dense(p_ctx):
    n_ctx = p_ctx.shape[0]
    blk = lambda col: pl.BlockSpec((n_ctx, BW), lambda i: (0, col))
    return pl.pallas_call(
        _ctx_dense_kernel,
        grid=(1,),
        in_specs=[blk(CB_QA), blk(CB_KA), blk(CB_VA)],
        out_specs=pl.BlockSpec((n_ctx, BW), lambda i: (0, 0)),
        out_shape=jax.ShapeDtypeStruct((n_ctx, BW), BF16),
        compiler_params=_cparams(("arbitrary",)),
        name="ctx_dense_attn",
    )(p_ctx, p_ctx, p_ctx)


def _diff_kernel(*refs, tq, tk, nk, lam_init, with_lat, nbuf):
    n_in = 10 if with_lat else 8
    if with_lat:
        qt_ref, kc_ref, vtc_ref, k_ref, vt_ref, lq1_ref, lk1_ref, lq2_ref, lk2_ref, g_ref = refs[:n_in]
    else:
        qt_ref, kc_ref, vtc_ref, lq1_ref, lk1_ref, lq2_ref, lk2_ref, g_ref = refs[:n_in]
    o_ref, qz_scr, m_scr, acc_scr = refs[n_in:n_in + 4]
    s_scr = refs[n_in + 4:n_in + 4 + nbuf]
    p_scr = refs[n_in + 4 + nbuf:]
    j = pl.program_id(1)
    lanes2 = 2 * tq
    rc = max(BF16_ROWS, BF16_ROWS * 1024 // lanes2)
    ck = 256

    def attend_all(k_ref, vt_ref, base, nkeys):
        ns = nkeys // ck

        def head_cols(h):
            return slice(h * DA_VD, (h + 1) * DA_VD)

        def run_round(h_qk, h_exp, h_pv, m_b):
            mx = None
            for i in range(ns):
                rows = slice(i * ck, (i + 1) * ck)
                keys = rows if isinstance(base, int) else pl.ds(pl.multiple_of(base + i * ck, ck), ck)
                if h_pv is not None:
                    vt_ext = jnp.concatenate([vt_ref[head_cols(h_pv), keys], vt_ref[BW:VT_ROWS, keys]], axis=0)
                    acc_scr[h_pv] += jnp.dot(vt_ext, p_scr[h_pv % nbuf][rows, :], preferred_element_type=F32)
                if h_exp is not None:
                    for c in range(i * ck // rc, (i + 1) * ck // rc):
                        rr = slice(c * rc, (c + 1) * rc)
                        p_scr[h_exp % nbuf][rr, :] = jnp.exp2(s_scr[h_exp % nbuf][rr, :] - m_b).astype(BF16)
                if h_qk is not None:
                    s = jnp.dot(k_ref[keys, head_cols(h_qk)], qz_scr[h_qk], preferred_element_type=F32)
                    s_scr[h_qk % nbuf][rows, :] = s
                    part = s.reshape(ck // rc, rc, lanes2).max(axis=0)
                    mx = part if mx is None else jnp.maximum(mx, part)
            return mx

        mx = None
        for r in range(DA_HEADS + 2):
            h_qk = r if r < DA_HEADS else None
            h_exp = r - 1 if 0 <= r - 1 < DA_HEADS else None
            h_pv = r - 2 if 0 <= r - 2 < DA_HEADS else None
            m_b = None
            if h_exp is not None:
                m_old = m_scr[h_exp]
                m_new = jnp.maximum(m_old, mx.max(axis=0, keepdims=True))
                m_scr[h_exp] = m_new
                acc_scr[h_exp] = jnp.exp2(m_old - m_new) * acc_scr[h_exp]
                m_b = jnp.broadcast_to(m_new, (rc, lanes2))
            mx = run_round(h_qk, h_exp, h_pv, m_b)

    @pl.when(j == 0)
    def _():
        row = lax.broadcasted_iota(I32, (DA_VD, tq), 0)
        for h in range(DA_HEADS):
            qh = qt_ref[h * DA_VD:(h + 1) * DA_VD, :]
            zero = jnp.zeros_like(qh)
            qz_scr[h] = jnp.concatenate([jnp.where(row < DA_DH, qh, zero),
                                         jnp.where(row >= DA_DH, qh, zero)], axis=1)
        m_scr[...] = jnp.full(m_scr.shape, -jnp.inf, F32)
        acc_scr[...] = jnp.zeros(acc_scr.shape, F32)
        attend_all(kc_ref, vtc_ref, 0, kc_ref.shape[0])

    if with_lat:
        ntile = k_ref.shape[0] // tk
        if ntile == 1:
            attend_all(k_ref, vt_ref, 0, tk)
        else:
            def tile_body(t, carry):
                attend_all(k_ref, vt_ref, t * tk, tk)
                return carry

            lax.fori_loop(0, ntile, tile_body, 0)

    @pl.when(j == nk - 1)
    def _():
        lam = (jnp.exp(jnp.sum(lq1_ref[...] * lk1_ref[...], keepdims=True))
               - jnp.exp(jnp.sum(lq2_ref[...] * lk2_ref[...], keepdims=True)) + lam_init)
        for h in range(DA_HEADS):
            o = acc_scr[h, 0:DA_VD, :] / acc_scr[h, DA_VD:DA_VD + 1, :]
            od = o[:, :tq] - lam * o[:, tq:]
            ms = jnp.mean(od * od, axis=0, keepdims=True)
            y = od * lax.rsqrt(ms + EPS) * g_ref[...] * (1.0 - lam_init)
            o_ref[:, h * DA_VD:(h + 1) * DA_VD] = y.T.astype(BF16)


def _diff(qt, p_ctx, vt_ctx, lat, lam_vecs, g_col, lam_init, tq, tk, nbuf=2):
    n_q, n_ctx = qt.shape[1], p_ctx.shape[0]
    with_lat = lat is not None
    nk = 1
    vec = pl.BlockSpec((1, DA_DH), lambda i, j: (0, 0))
    in_specs = [pl.BlockSpec((BW, tq), lambda i, j: (0, i)),
                pl.BlockSpec((n_ctx, BW), lambda i, j: (0, CB_KB)),
                pl.BlockSpec((VT_ROWS, n_ctx), lambda i, j: (0, 0))]
    args = [qt, p_ctx, vt_ctx]
    if with_lat:
        n_lat = lat[0].shape[0]
        in_specs += [pl.BlockSpec((n_lat, BW), lambda i, j: (0, CB_KB), pipeline_mode=pl.Buffered(1)),
                     pl.BlockSpec((VT_ROWS, n_lat), lambda i, j: (0, 0), pipeline_mode=pl.Buffered(1))]
        args += list(lat)
    in_specs += [vec, vec, vec, vec, pl.BlockSpec((DA_VD, 1), lambda i, j: (0, 0))]
    return pl.pallas_call(
        functools.partial(_diff_kernel, tq=tq, tk=tk, nk=nk, lam_init=lam_init, with_lat=with_lat, nbuf=nbuf),
        grid=(n_q // tq, nk),
        in_specs=in_specs,
        out_specs=pl.BlockSpec((tq, BW), lambda i, j: (i, 0)),
        out_shape=jax.ShapeDtypeStruct((n_q, BW), BF16),
        scratch_shapes=[pltpu.VMEM((DA_HEADS, DA_VD, 2 * tq), BF16),
                        pltpu.VMEM((DA_HEADS, 1, 2 * tq), F32),
                        pltpu.VMEM((DA_HEADS, DA_VD + BF16_ROWS, 2 * tq), F32)]
        + [pltpu.VMEM((max(tk, n_ctx), 2 * tq), F32)] * nbuf
        + [pltpu.VMEM((max(tk, n_ctx), 2 * tq), BF16)] * nbuf,
        compiler_params=_cparams(("arbitrary", "arbitrary")),
        name="diff_attn",
    )(*args, *lam_vecs, g_col)


def _merge_kernel(x_ref, ya_ref, yb_ref, u_ref, v_ref, ga_ref, gb_ref, gc_ref, mod_ref, wb_ref, wo_ref,
                  lng_ref, lnb_ref, ws_ref, bst_ref, g2_ref, wrt_ref,
                  x1_ref, h2_ref, aff_ref, yc_scr, *, mrow, tm):
    ug = _gelu(u_ref[...].astype(F32))
    vg = _gelu(v_ref[...].astype(F32))
    mu = jnp.mean(vg, axis=-1, keepdims=True)
    var = jnp.mean(jnp.square(vg - mu), axis=-1, keepdims=True)
    vn = ((vg - mu) * lax.rsqrt(var + EPS) * lng_ref[...] + lnb_ref[...]).astype(BF16)
    for ci in range(tm // GM_CHUNK):
        rs = slice(ci * GM_CHUNK, (ci + 1) * GM_CHUNK)
        for g in range(GM_GROUPS):
            cs = slice(g * LANES, (g + 1) * LANES)
            mixed = jnp.dot(ws_ref[g], vn[rs, cs], preferred_element_type=F32) + bst_ref[:, g:g + 1]
            yc_scr[rs, cs] = (ug[rs, cs] * mixed).astype(BF16)

    sig = jax.nn.sigmoid
    m = sig(ga_ref[...].astype(F32)) * jnp.dot(ya_ref[...], wb_ref[0], preferred_element_type=F32)
    m = m + sig(gb_ref[...].astype(F32)) * jnp.dot(yb_ref[...], wb_ref[1], preferred_element_type=F32)
    m = m + sig(gc_ref[...].astype(F32)) * jnp.dot(yc_scr[...], wb_ref[2], preferred_element_type=F32)
    y = jnp.dot(m.astype(BF16), wo_ref[...], preferred_element_type=F32)
    gt1 = mod_ref[mrow:mrow + 1, 2 * D:3 * D]
    x1 = x_ref[...] + gt1 * y
    x1_ref[...] = x1

    ms = jnp.mean(x1 * x1, axis=-1, keepdims=True)
    sh2 = mod_ref[mrow:mrow + 1, 3 * D:4 * D]
    sc2 = mod_ref[mrow:mrow + 1, 4 * D:5 * D]
    h2 = x1 * lax.rsqrt(ms + EPS) * g2_ref[...] * (1.0 + sc2) + sh2
    for s in range(D // LANES):
        h2_ref[pl.ds(s, tm, stride=D // LANES), :] = h2[:, s * LANES:(s + 1) * LANES]

    hh = h2.astype(BF16)
    hl = (h2 - hh.astype(F32)).astype(BF16)
    w = wrt_ref[...]
    wh = w.astype(BF16)
    wl = (w - wh.astype(F32)).astype(BF16)
    lg = (lax.dot_general(wh, hh, _NT, preferred_element_type=F32)
          + lax.dot_general(wh, hl, _NT, preferred_element_type=F32)
          + lax.dot_general(wl, hh, _NT, preferred_element_type=F32))
    e = jnp.exp(lg - lg.max(axis=0, keepdims=True))
    aff_ref[...] = e / e.sum(axis=0, keepdims=True)


def _merge(x, ya, yb, p_all, mod, mrow, wb, wo, lng, lnb, ws, bst, g2, wrt, tm):
    n = x.shape[0]
    const = lambda shape: pl.BlockSpec(shape, lambda i: (0,) * len(shape))
    return pl.pallas_call(
        functools.partial(_merge_kernel, mrow=mrow, tm=tm),
        grid=(n // tm,),
        in_specs=[pl.BlockSpec((tm, D), lambda i: (i, 0)),
                  pl.BlockSpec((tm, BW), lambda i: (i, 0)),
                  pl.BlockSpec((tm, BW), lambda i: (i, 0)),
                  pl.BlockSpec((tm, BW), lambda i: (i,CB_U)),
                  pl.BlockSpec((tm, BW), lambda i: (i,CB_V)),
                  pl.BlockSpec((tm, D), lambda i: (i,4)),
                  pl.BlockSpec((tm, D), lambda i: (i,5)),
                  pl.BlockSpec((tm, D), lambda i: (i,6)),
                  const((SUBLANES, N_MOD * D)),
                  const((3, BW, D)), const((D, D)),
                  const((1, BW)), const((1, BW)),
                  const((GM_GROUPS, GM_CHUNK, GM_CHUNK)), const((GM_CHUNK, GM_GROUPS)),
                  const((1, D)), const((N_EXPERTS, D))],
        out_specs=[pl.BlockSpec((tm, D), lambda i: (i, 0)),
                   pl.BlockSpec((tm * (D // LANES), LANES), lambda i: (i, 0)),
                   pl.BlockSpec((N_EXPERTS, tm), lambda i: (0, i))],
        out_shape=[jax.ShapeDtypeStruct((n, D), F32),
                   jax.ShapeDtypeStruct((n * (D // LANES), LANES), F32),
                   jax.ShapeDtypeStruct((N_EXPERTS, n), F32)],
        scratch_shapes=[pltpu.VMEM((tm, BW), BF16)],
        compiler_params=_cparams(("arbitrary",)),
        name="merge_prenorm",
    )(x, ya, yb, p_all, p_all, p_all, p_all, p_all, mod, wb, wo, lng, lnb, ws, bst, g2, wrt)


def _route_kernel(all_ref, a_ref, idx_ref, gate_ref, slot_ref, rank_ref, thr_scr, *, cap, nrow):
    e = pl.program_id(0)

    @pl.when(e == 0)
    def _():
        bits = pltpu.bitcast(all_ref[...], I32).reshape(N_EXPERTS, nrow, LANES)

        def bisect(k, prefix):
            cand = prefix | jnp.left_shift(jnp.int32(1), 30 - k)
            hit = jnp.where(bits >= cand, 1.0, 0.0)
            cnt = hit.sum(axis=1, keepdims=True).sum(axis=2, keepdims=True)
            return jnp.where(cnt >= cap, cand, prefix)

        thr_bits = lax.fori_loop(0, 31, bisect, jnp.zeros((N_EXPERTS, 1, 1), I32))
        tb = jnp.broadcast_to(thr_bits, (N_EXPERTS, SUBLANES, LANES)).reshape(N_EXPERTS * SUBLANES, LANES)
        thr_scr[...] = pltpu.bitcast(tb, F32).reshape(N_EXPERTS, SUBLANES, LANES)

    a = a_ref[0]
    thr = thr_scr[e][0:1, :]
    gt = a > thr
    eq = a == thr

    r_i = lax.broadcasted_iota(I32, (LANES, LANES), 0)
    c_i = lax.broadcasted_iota(I32, (LANES, LANES), 1)
    upper = (r_i <= c_i).astype(BF16)
    rr = lax.broadcasted_iota(I32, (nrow, nrow), 0)
    rc = lax.broadcasted_iota(I32, (nrow, nrow), 1)
    lower_strict = (rc < rr).astype(BF16)
    upper_strict = (rr < rc).astype(BF16)

    def prefix(mask):
        xf = jnp.where(mask, 1.0, 0.0)
        incl = jnp.dot(xf.astype(BF16), upper, preferred_element_type=F32)
        tot = jnp.broadcast_to(incl[:, LANES - 1:LANES], (nrow, LANES))
        base = jnp.dot(lower_strict, tot.astype(BF16), preferred_element_type=F32)
        return xf, incl, base + incl - xf

    _, _, eq_rank = prefix(eq)
    need = cap - jnp.sum(jnp.where(gt, 1.0, 0.0))
    sel = gt | (eq & (eq_rank < need))
    xf, incl, rank = prefix(sel)
    rank_ref[0] = rank.astype(I32)
    slot_ref[0] = jnp.where(sel, rank, -1.0).astype(I32)

    ones = jnp.ones((SUBLANES, LANES), BF16)
    tot_l = lax.dot_general(ones, xf.astype(BF16), _NT, preferred_element_type=F32)
    off_l = jnp.dot(tot_l.astype(BF16), upper_strict, preferred_element_type=F32)[0:1]
    tot_l = tot_l[0:1]

    jf = lax.broadcasted_iota(I32, (cap, nrow), 0).astype(F32)
    oh_row = (off_l <= jf) & (jf < off_l + tot_l)
    ohb = jnp.where(oh_row, 1.0, 0.0).astype(BF16)
    row_id = lax.broadcasted_iota(I32, (cap, nrow), 1).astype(F32)
    off_j = jnp.sum(jnp.where(oh_row, off_l, 0.0), axis=1, keepdims=True)
    row_j = jnp.sum(jnp.where(oh_row, row_id, 0.0), axis=1, keepdims=True)
    key = jnp.where(sel, incl, 0.0).astype(BF16)
    g = jnp.dot(ohb, key, preferred_element_type=F32)
    target = lax.broadcasted_iota(I32, (cap, 1), 0).astype(F32) - off_j + 1.0
    oh_lane = g == target
    lane_id = lax.broadcasted_iota(I32, (cap, LANES), 1).astype(F32)
    lane_j = jnp.sum(jnp.where(oh_lane, lane_id, 0.0), axis=1, keepdims=True)
    idx_ref[0] = (row_j * LANES + lane_j).astype(I32)

    a1 = a.astype(BF16)
    r1 = a - a1.astype(F32)
    a2 = r1.astype(BF16)
    a3 = (r1 - a2.astype(F32)).astype(BF16)
    arow = (jnp.dot(ohb, a1, preferred_element_type=F32) + jnp.dot(ohb, a2, preferred_element_type=F32)
            + jnp.dot(ohb, a3, preferred_element_type=F32))
    gate_ref[0] = jnp.sum(jnp.where(oh_lane, arow, 0.0), axis=1, keepdims=True)


def _route(aff3, cap):
    nrow = aff3.shape[1]
    return pl.pallas_call(
        functools.partial(_route_kernel, cap=cap, nrow=nrow),
        grid=(N_EXPERTS,),
        in_specs=[pl.BlockSpec((N_EXPERTS * nrow, LANES), lambda e: (0, 0)),
                  pl.BlockSpec((1, nrow, LANES), lambda e: (e, 0, 0))],
        out_specs=[pl.BlockSpec((1, cap, 1), lambda e: (e, 0, 0)),
                   pl.BlockSpec((1, cap, 1), lambda e: (e, 0, 0)),
                   pl.BlockSpec((1, nrow, LANES), lambda e: (e, 0, 0)),
                   pl.BlockSpec((1, nrow, LANES), lambda e: (e, 0, 0))],
        out_shape=[jax.ShapeDtypeStruct((N_EXPERTS, cap, 1), I32),
                   jax.ShapeDtypeStruct((N_EXPERTS, cap, 1), F32),
                   jax.ShapeDtypeStruct((N_EXPERTS, nrow, LANES), I32),
                   jax.ShapeDtypeStruct((N_EXPERTS, nrow, LANES), I32)],
        scratch_shapes=[pltpu.VMEM((N_EXPERTS, SUBLANES, LANES), F32)],
        compiler_params=_cparams(("arbitrary",)),
        name="ec_route",
    )(aff3.reshape(N_EXPERTS * nrow, LANES), aff3)


def _ffn_kernel(idx_ref, h2_hbm, gate_ref, w1_ref, w3_ref, w2_ref, ye_ref, buf, sem, wb_scr, *, tc, nc):
    sub = D // LANES

    @pl.when(pl.program_id(1) == 0)
    def _():
        wb_scr[0] = w1_ref[0, 0].astype(BF16)
        wb_scr[1] = w3_ref[0, 0].astype(BF16)
        wb_scr[2] = w2_ref[0, 0].astype(BF16)

    step = pl.program_id(0) * nc + pl.program_id(1)
    nsteps = N_EXPERTS * nc

    def row_copy(tok, r, slot):
        return pltpu.make_async_copy(h2_hbm.at[pl.ds(pl.multiple_of(tok * sub, sub), sub), :],
                                     buf.at[slot, pl.ds(pl.multiple_of(r * sub, sub), sub), :],
                                     sem.at[slot])

    def wait_slot(slot):
        pltpu.make_async_copy(h2_hbm.at[pl.ds(0, tc * sub), :], buf.at[slot], sem.at[slot]).wait()

    @pl.when(step == 0)
    def _():
        def body(r, carry):
            row_copy(idx_ref[0, r], r, 0).start()
            return carry

        lax.fori_loop(0, tc, body, 0, unroll=16)

    nxt = jnp.minimum(step + 1, nsteps - 1)
    nxt_e = nxt // nc
    nxt_base = (nxt % nc) * tc
    nslot = (step + 1) % 2
    parts = 4 if tc % 4 == 0 else 1

    def issue_part(q):
        for r in range(q * tc // parts, (q + 1) * tc // parts):
            row_copy(idx_ref[nxt_e, nxt_base + r], r, nslot).start(priority=r % 2)

    slot = step % 2
    wait_slot(slot)
    xin = jnp.concatenate([buf[slot, pl.ds(s, tc, stride=sub), :] for s in range(sub)], axis=1).astype(BF16)
    issue_part(0)
    a = jnp.dot(xin, wb_scr[0], preferred_element_type=F32)
    if parts == 4:
        issue_part(1)
    b = jnp.dot(xin, wb_scr[1], preferred_element_type=F32)
    hid = (a * jax.nn.sigmoid(a) * b).astype(BF16)
    if parts == 4:
        issue_part(2)
    y = jnp.dot(hid, wb_scr[2], preferred_element_type=F32)
    if parts == 4:
        issue_part(3)
    ye_ref[0] = (y * gate_ref[0]).astype(BF16)

    @pl.when(step == nsteps - 1)
    def _():
        wait_slot(nslot)


def _ffn(idx, gate, h2, w1, w3, w2, layer, tc):
    cap = idx.shape[1]
    nc = cap // tc
    sub = D // LANES
    wspec = pl.BlockSpec((1, 1, D, D), lambda e, c, idx: (layer, e, 0, 0))
    grid_spec = pltpu.PrefetchScalarGridSpec(
        num_scalar_prefetch=1,
        grid=(N_EXPERTS, nc),
        in_specs=[pl.BlockSpec(memory_space=pl.ANY),
                  pl.BlockSpec((1, tc, 1), lambda e, c, idx: (e, c, 0)),
                  wspec, wspec, wspec],
        out_specs=pl.BlockSpec((1, tc, D), lambda e, c, idx: (e, c, 0)),
        scratch_shapes=[pltpu.VMEM((2, tc * sub, LANES), F32),
                        pltpu.SemaphoreType.DMA((2,)),
                        pltpu.VMEM((3, D, D), BF16)])
    return pl.pallas_call(
        functools.partial(_ffn_kernel, tc=tc, nc=nc),
        grid_spec=grid_spec,
        out_shape=jax.ShapeDtypeStruct((N_EXPERTS, cap, D), BF16),
        compiler_params=_cparams(("arbitrary", "arbitrary")),
        name="expert_ffn",
    )(idx, h2, gate, w1, w3, w2)


def _combine_kernel(rs_ref, x_ref, slot_ref, mod_ref, gf_ref, ye_hbm, o_ref, win, sem, y_scr,
                    *, win_rows, cap, nt, tt, mrow, final):
    i = pl.program_id(0)

    def start_of(e, t):
        s0 = rs_ref[e, t]
        s_al = (s0 // BF16_ROWS) * BF16_ROWS
        return pl.multiple_of(jnp.minimum(s_al, cap - win_rows), BF16_ROWS)

    def copy(e, t, sl):
        return pltpu.make_async_copy(ye_hbm.at[e, pl.ds(start_of(e, t), win_rows), :],
                                     win.at[sl, e], sem.at[sl])

    @pl.when(i == 0)
    def _():
        for e in range(N_EXPERTS):
            copy(e, i, 0).start()

    @pl.when(i + 1 < nt)
    def _():
        for e in range(N_EXPERTS):
            copy(e, i + 1, (i + 1) % 2).start()

    sl = i % 2
    for e in range(N_EXPERTS):
        copy(e, i, sl).wait()

    def window_sum(r0, r1):
        acc = jnp.zeros((tt, D), F32)
        wi = lax.broadcasted_iota(I32, (r1 - r0, tt), 0) + r0
        for e in range(N_EXPERTS):
            rel = slot_ref[e:e + 1, :] - start_of(e, i)
            oh_t = jnp.where(wi == rel, 1.0, 0.0).astype(BF16)
            acc = acc + lax.dot_general(oh_t, win[sl, e, r0:r1, :], _TN, preferred_element_type=F32)
        return acc

    main_rows = min(win_rows, tt)
    gt2 = mod_ref[mrow:mrow + 1, 5 * D:6 * D]
    y_scr[...] = x_ref[...] + gt2 * window_sum(0, main_rows)
    if win_rows > main_rows:
        need_tail = False
        for e in range(N_EXPERTS):
            end = rs_ref[e, i + 1] - start_of(e, i)
            need_tail = jnp.logical_or(need_tail, end > main_rows)

        @pl.when(need_tail)
        def _():
            y_scr[...] += gt2 * window_sum(main_rows, win_rows)

    y = y_scr[...]
    if final:
        ms = jnp.mean(y * y, axis=-1, keepdims=True)
        y = y * lax.rsqrt(ms + EPS) * gf_ref[...]
    o_ref[...] = y


def _combine(rstart, x1, slot, mod, mrow, g_final, ye, tt, final):
    n = x1.shape[0]
    cap = ye.shape[1]
    nt = n // tt
    win_rows = min(cap, tt + BF16_ROWS)
    grid_spec = pltpu.PrefetchScalarGridSpec(
        num_scalar_prefetch=1,
        grid=(nt,),
        in_specs=[pl.BlockSpec((tt, D), lambda i, rs: (i, 0)),
                  pl.BlockSpec((N_EXPERTS, tt), lambda i, rs: (0, i)),
                  pl.BlockSpec((SUBLANES, N_MOD * D), lambda i, rs: (0, 0)),
                  pl.BlockSpec((1, D), lambda i, rs: (0, 0)),
                  pl.BlockSpec(memory_space=pl.ANY)],
        out_specs=pl.BlockSpec((tt, D), lambda i, rs: (i, 0)),
        scratch_shapes=[pltpu.VMEM((2, N_EXPERTS, win_rows, D), BF16),
                        pltpu.SemaphoreType.DMA((2,)),
                        pltpu.VMEM((tt, D), F32)])
    return pl.pallas_call(
        functools.partial(_combine_kernel, win_rows=win_rows, cap=cap, nt=nt, tt=tt, mrow=mrow, final=final),
        grid_spec=grid_spec,
        out_shape=jax.ShapeDtypeStruct((n, D), F32),
        compiler_params=_cparams(("arbitrary",)),
        name="moe_combine",
    )(rstart, x1, slot, mod, g_final, ye)


def _rope_tables(n_lat):
    rows = n_lat // GRID_W
    inv = ROPE_THETA ** (-jnp.arange(0, ROPE_AXIS, 2, dtype=F32) / ROPE_AXIS)
    ang_r = jnp.arange(rows).astype(F32)[:, None] * inv
    ang_c = jnp.arange(GRID_W).astype(F32)[:, None] * inv
    cr, sr, cc, sc = jnp.cos(ang_r), jnp.sin(ang_r), jnp.cos(ang_c), jnp.sin(ang_c)
    zr, zc = jnp.zeros_like(sr), jnp.zeros_like(sc)
    rep = lambda parts: jnp.concatenate(parts * (LANES // DA_DH), axis=1)

    def table(row_parts, col_parts):
        r = rep([row_parts[0], row_parts[1], zr, zr])
        c = rep([zc, zc, col_parts[0], col_parts[1]])
        return (r[:, None, :] + c[None, :, :]).reshape(n_lat, LANES)

    return (table((cr, cr), (cc, cc)), table((zr, sr), (zc, sc)), table((-sr, zr), (-sc, zc)))


def _largest_tile(n, unit, limit):
    best = unit
    for k in range(1, n // unit + 1):
        if n % (k * unit) == 0 and k * unit <= limit:
            best = k * unit
    return best


def _moe(x1, h2, aff_t, mod, mrow, w1, w3, w2, layer, g_final, final):
    n = x1.shape[0]
    cap = max(1, EC_CAPACITY * n // N_EXPERTS)
    tile = LANES * LANES if n <= LANES * LANES else n
    n_pad = max(n, tile)
    if n_pad > n:
        aff_t = jnp.concatenate([aff_t, jnp.full((N_EXPERTS, n_pad - n), -1.0, F32)], axis=1)
    idx, gate, slot, rank = _route(aff_t.reshape(N_EXPERTS, n_pad // LANES, LANES), cap)
    tt = min(256, n)
    slot = slot.reshape(N_EXPERTS, n_pad)[:, :n]
    rstart = jnp.concatenate([rank.reshape(N_EXPERTS, n_pad)[:, 0:n:tt],
                              jnp.full((N_EXPERTS, 1), cap, I32)], axis=1)
    tc = min(cap, 512)
    ye = _ffn(idx.reshape(N_EXPERTS, cap), gate, h2, w1, w3, w2, layer, tc)
    return _combine(rstart, x1, slot, mod, mrow, g_final, ye, tt, final)


def kernel(x, c, ctx, c_ctx, w_ada, b_ada, g_norm1, g_norm2, w_in, na_rpb, da_lam_q1, da_lam_k1, da_lam_q2,
           da_lam_k2, da_subln_g, gm_ln_g, gm_ln_b, gm_w_s, gm_b_s, w_branch, w_out, w_router, w_e1, w_e3,
           w_e2, g_final):
    depth = w_ada.shape[0]
    n_lat, n_ctx = x.shape[1], ctx.shape[1]
    rows = n_lat // GRID_W
    xs, xc = x[0], ctx[0]

    cc = jnp.concatenate([c.reshape(1, D), c_ctx.reshape(1, D), jnp.zeros((SUBLANES - 2, D), F32)], axis=0)
    mods = _ada(cc, w_ada, b_ada)

    tabs_lat = _rope_tables(n_lat)
    tabs_ctx = (jnp.ones((n_ctx, LANES), F32), jnp.zeros((n_ctx, LANES), F32), jnp.zeros((n_ctx, LANES), F32))
    tm_proj = _largest_tile(n_lat, 256, 1024)
    tk = _largest_tile(n_lat, 256, 2048)
    tq = 256

    for i in range(depth):
        last = i == depth - 1
        lam_init = 0.8 - 0.6 * math.exp(-0.3 * i)
        mod = mods[i]
        w_in_b = w_in[i].astype(BF16)
        g1 = g_norm1[i].reshape(1, D)
        g2 = g_norm2[i].reshape(1, D)
        gf = g_final.reshape(1, D)
        lam_vecs = [v[i].reshape(1, DA_DH).astype(F32) for v in (da_lam_q1, da_lam_k1, da_lam_q2, da_lam_k2)]
        g_col = da_subln_g[i].reshape(DA_VD, 1)
        merge_w = (w_branch[i].astype(BF16), w_out[i].astype(BF16), gm_ln_g[i].reshape(1, BW),
                   gm_ln_b[i].reshape(1, BW), gm_w_s[i].astype(BF16), gm_b_s[i].T, g2, w_router[i].T)

        p_lat, qt_lat, vt_lat = _proj(xs, g1, mod, w_in_b, tabs_lat, 0, tm_proj)
        p_ctx, qt_ctx, vt_ctx = _proj(xc, g1, mod, w_in_b, tabs_ctx, 1, n_ctx)

        ya = _na(p_lat, p_ctx, _na_bias(na_rpb[i], rows))
        yb = _diff(qt_lat, p_ctx, vt_ctx, (p_lat, vt_lat), lam_vecs, g_col, lam_init, tq, tk)
        x1, h2, aff_t = _merge(xs, ya, yb, p_lat, mod, 0, *merge_w, tm=_largest_tile(n_lat, 256, 512))
        xs = _moe(x1, h2, aff_t, mod, 0, w_e1, w_e3, w_e2, i, gf, last)

        if not last:
            yac = _ctx_dense(p_ctx)
            ybc = _diff(qt_ctx, p_ctx, vt_ctx, None, lam_vecs, g_col, lam_init, n_ctx, n_ctx)
            x1c, h2c, aff_tc = _merge(xc, yac, ybc, p_ctx, mod, 1, *merge_w, tm=n_ctx)
            xc = _moe(x1c, h2c, aff_tc, mod, 1, w_e1, w_e3, w_e2, i, gf, False)
    return xs[None]
```

```python
import functools
import math

import numpy as np
import jax
import jax.numpy as jnp
from jax import lax
from jax.experimental import pallas as pl
from jax.experimental.pallas import tpu as pltpu

F32 = jnp.float32
BF16 = jnp.bfloat16
I32 = jnp.int32

D = 1024
GRID_W = 64
BW = 512
N_COLBLK = 14
NA_HEADS = 8
NA_WIN_R = 8
NA_WIN_C = 16
NA_ROWS = 4
NA_KROWS = 12
DA_HEADS = 4
DA_DH = 64
DA_VD = 128
GM_GROUPS = 4
GM_CHUNK = 128
N_EXPERTS = 16
EC_CAPACITY = 2
ROPE_THETA = 10000.0
ROPE_AXIS = DA_DH // 2
N_MOD = 6
EPS = 1e-6
LANES = 128
SUBLANES = 8
BF16_ROWS = 16
NEG_BIG = -1e30
VT_ROWS = BW + BF16_ROWS
Q_SCALE_LOG2 = DA_DH ** -0.5 * math.log2(math.e)
VMEM_LIMIT = 56 * 1024 * 1024

CB_KA, CB_VA, CB_KB, CB_VB, CB_QA, CB_QB, CB_U, CB_V = range(8)

_NT = (((1,), (1,)), ((), ()))
_TN = (((0,), (0,)), ((), ()))


def _cparams(sem):
    return pltpu.CompilerParams(dimension_semantics=sem, vmem_limit_bytes=VMEM_LIMIT)


def _gelu(x):
    return 0.5 * x * (1.0 + jnp.tanh(math.sqrt(2.0 / math.pi) * (x + 0.044715 * (x * x * x))))


def _ada_kernel(c_ref, w_ref, b_ref, o_ref):
    c = c_ref[...]
    s = c * jax.nn.sigmoid(c)
    o_ref[0] = jnp.dot(s, w_ref[0], preferred_element_type=F32,
                       precision=lax.Precision.HIGHEST) + b_ref[0]


def _ada(cc, w_ada, b_ada):
    depth = w_ada.shape[0]
    tn = D
    return pl.pallas_call(
        _ada_kernel,
        grid=(depth, N_MOD * D // tn),
        in_specs=[pl.BlockSpec((SUBLANES, D), lambda i, j: (0, 0)),
                  pl.BlockSpec((1, D, tn), lambda i, j: (i, 0, j)),
                  pl.BlockSpec((1, 1, tn), lambda i, j: (i, 0, j))],
        out_specs=pl.BlockSpec((1, SUBLANES, tn), lambda i, j: (i, 0, j)),
        out_shape=jax.ShapeDtypeStruct((depth, SUBLANES, N_MOD * D), F32),
        compiler_params=_cparams(("arbitrary", "arbitrary")),
        name="ada_mod",
    )(cc, w_ada, b_ada.reshape(depth, 1, N_MOD * D))


def _proj_kernel(x_ref, g_ref, mod_ref, w_ref, c_ref, sp_ref, sm_ref, p_ref, qt_ref, vt_ref, h_scr, *, mrow):
    j = pl.program_id(1)

    @pl.when(j == 0)
    def _():
        x = x_ref[...]
        ms = jnp.mean(x * x, axis=-1, keepdims=True)
        xn = x * lax.rsqrt(ms + EPS) * g_ref[...]
        sh = mod_ref[mrow:mrow + 1, 0:D]
        sc = mod_ref[mrow:mrow + 1, D:2 * D]
        h_scr[...] = (xn * (1.0 + sc) + sh).astype(BF16)

    z = jnp.dot(h_scr[...], w_ref[...], preferred_element_type=F32)

    def rope(z):
        reps = BW // LANES
        c = jnp.concatenate([c_ref[...]] * reps, axis=1)
        sp = jnp.concatenate([sp_ref[...]] * reps, axis=1)
        sm = jnp.concatenate([sm_ref[...]] * reps, axis=1)
        half = ROPE_AXIS // 2
        return z * c + pltpu.roll(z, half, 1) * sp + pltpu.roll(z, BW - half, 1) * sm

    @pl.when(j == CB_KB // 2)
    def _():
        p_ref[:, 0:BW] = rope(z[:, 0:BW]).astype(BF16)
        zv = z[:, BW:2 * BW]
        p_ref[:, BW:2 * BW] = zv.astype(BF16)
        vt_ref[0:BW, :] = zv.T.astype(BF16)
        vt_ref[BW:VT_ROWS, :] = jnp.ones((VT_ROWS - BW, z.shape[0]), BF16)

    @pl.when(j == CB_QB // 2)
    def _():
        p_ref[:, 0:BW] = z[:, 0:BW].astype(BF16)
        zr = rope(z[:, BW:2 * BW])
        p_ref[:, BW:2 * BW] = zr.astype(BF16)
        qt_ref[...] = (zr * Q_SCALE_LOG2).T.astype(BF16)

    @pl.when((j != CB_KB // 2) & (j != CB_QB // 2))
    def _():
        p_ref[...] = z.astype(BF16)


def _proj(x, g, mod, w, tabs, mrow, tm):
    n = x.shape[0]
    return pl.pallas_call(
        functools.partial(_proj_kernel, mrow=mrow),
        grid=(n // tm, N_COLBLK // 2),
        in_specs=[pl.BlockSpec((tm, D), lambda i, j: (i, 0)),
                  pl.BlockSpec((1, D), lambda i, j: (0, 0)),
                  pl.BlockSpec((SUBLANES, N_MOD * D), lambda i, j: (0, 0)),
                  pl.BlockSpec((D, 2 * BW), lambda i, j: (0, j)),
                  pl.BlockSpec((tm, LANES), lambda i, j: (i, 0)),
                  pl.BlockSpec((tm, LANES), lambda i, j: (i, 0)),
                  pl.BlockSpec((tm, LANES), lambda i, j: (i, 0))],
        out_specs=[pl.BlockSpec((tm, 2 * BW), lambda i, j: (i, j)),
                   pl.BlockSpec((BW, tm), lambda i, j: (0, i)),
                   pl.BlockSpec((VT_ROWS, tm), lambda i, j: (0, i))],
        out_shape=[jax.ShapeDtypeStruct((n, N_COLBLK * BW), BF16),
                   jax.ShapeDtypeStruct((BW, n), BF16),
                   jax.ShapeDtypeStruct((VT_ROWS, n), BF16)],
        scratch_shapes=[pltpu.VMEM((tm, D), BF16)],
        compiler_params=_cparams(("arbitrary", "arbitrary")),
        name="proj",
    )(x, g, mod, w, *tabs)


def _na_kernel(q_ref, k0_ref, k1_ref, k2_ref, v0_ref, v1_ref, v2_ref, kc_ref, vc_ref, bias_ref, o_ref):
    nq = q_ref.shape[0]
    lane = lax.broadcasted_iota(I32, (nq, LANES), 1)
    k_refs = (kc_ref, k0_ref, k1_ref, k2_ref)
    v_refs = (vc_ref, v0_ref, v1_ref, v2_ref)
    for g in range(NA_HEADS // 2):
        sl = slice(g * LANES, (g + 1) * LANES)
        qp = q_ref[:, sl] * 0.125
        ks = [r[:, sl] for r in k_refs]
        vs = [r[:, sl] for r in v_refs]
        outs = []
        for sub in range(2):
            h = 2 * g + sub
            keep = (lane < 64) if sub == 0 else (lane >= 64)
            qz = jnp.where(keep, qp, jnp.zeros_like(qp))
            ss = [lax.dot_general(qz, ks[0], _NT, preferred_element_type=F32)]
            for t in range(3):
                s = lax.dot_general(qz, ks[1 + t], _NT, preferred_element_type=F32)
                ss.append(s + bias_ref[0, h, :, t * nq:(t + 1) * nq])
            m = ss[0].max(axis=-1, keepdims=True)
            for s in ss[1:]:
                m = jnp.maximum(m, s.max(axis=-1, keepdims=True))
            l = jnp.zeros_like(m)
            o = jnp.zeros((nq, LANES), F32)
            for s, v in zip(ss, vs):
                p = jnp.exp(s - m)
                l = l + p.sum(axis=-1, keepdims=True)
                o = o + jnp.dot(p.astype(BF16), v, preferred_element_type=F32)
            outs.append(o / l)
        o_ref[:, sl] = jnp.where(lane < 64, outs[0], outs[1]).astype(BF16)


def _na_bias(rpb, rows):
    nb = rows // NA_ROWS
    c = np.arange(GRID_W)[:, None]
    kc = np.arange(GRID_W)[None, :]
    c0 = np.clip(c - NA_WIN_C // 2, 0, GRID_W - NA_WIN_C)
    col_ok = (kc >= c0) & (kc < c0 + NA_WIN_C)
    dc = kc - c + (NA_WIN_C - 1)
    sel = np.stack([(dc == d) & col_ok for d in range(2 * NA_WIN_C - 1)]).astype(np.float32)
    toep = jnp.einsum("hrd,dck->hrck", rpb.astype(F32), sel, precision=lax.Precision.HIGHEST)
    toep = toep + np.where(col_ok, 0.0, NEG_BIG).astype(np.float32)
    n_dr = 2 * NA_WIN_R - 1
    toep = jnp.concatenate([toep, jnp.full((NA_HEADS, 1, GRID_W, GRID_W), NEG_BIG, F32)], axis=1)
    which = np.full((3, NA_ROWS, NA_KROWS), n_dr, np.int32)
    for v, b in enumerate((0, 1, nb - 1)):
        kb0 = min(max(b - 1, 0), nb - 3)
        for a in range(NA_ROWS):
            r = NA_ROWS * b + a
            r0 = min(max(r - NA_WIN_R // 2, 0), rows - NA_WIN_R)
            for i in range(NA_KROWS):
                kr = NA_ROWS * kb0 + i
                if r0 <= kr < r0 + NA_WIN_R:
                    which[v, a, i] = kr - r + NA_WIN_R - 1
    blocks = jnp.take(toep, which.reshape(-1), axis=1)
    blocks = blocks.reshape(NA_HEADS, 3, NA_ROWS, NA_KROWS, GRID_W, GRID_W)
    return blocks.transpose(1, 0, 2, 4, 3, 5).reshape(3, NA_HEADS, NA_ROWS * GRID_W, NA_KROWS * GRID_W)


def _na(p_lat, p_ctx, bias):
    nq = NA_ROWS * GRID_W
    n_lat, n_ctx = p_lat.shape[0], p_ctx.shape[0]
    nb = n_lat // nq

    def kmap(t, col):
        return lambda b: (jnp.clip(b - 1, 0, nb - 3) + t, col)

    def bmap(b):
        return (jnp.where(b == 0, 0, jnp.where(b == nb - 1, 2, 1)), 0, 0, 0)

    blk = lambda f: pl.BlockSpec((nq, BW), f)
    return pl.pallas_call(
        _na_kernel,
        grid=(nb,),
        in_specs=[blk(lambda b: (b, CB_QA)),
                  blk(kmap(0, CB_KA)), blk(kmap(1, CB_KA)), blk(kmap(2, CB_KA)),
                  blk(kmap(0, CB_VA)), blk(kmap(1, CB_VA)), blk(kmap(2, CB_VA)),
                  pl.BlockSpec((n_ctx, BW), lambda b: (0, CB_KA)),
                  pl.BlockSpec((n_ctx, BW), lambda b: (0, CB_VA)),
                  pl.BlockSpec((1, NA_HEADS, nq, NA_KROWS * GRID_W), bmap)],
        out_specs=pl.BlockSpec((nq, BW), lambda b: (b, 0)),
        out_shape=jax.ShapeDtypeStruct((n_lat, BW), BF16),
        compiler_params=_cparams(("arbitrary",)),
        name="na_attn",
    )(p_lat, p_lat, p_lat, p_lat, p_lat, p_lat, p_lat, p_ctx, p_ctx, bias)


def _ctx_dense_kernel(q_ref, k_ref, v_ref, o_ref):
    nq = q_ref.shape[0]
    lane = lax.broadcasted_iota(I32, (nq, LANES), 1)
    for g in range(NA_HEADS // 2):
        sl = slice(g * LANES, (g + 1) * LANES)
        qp = q_ref[:, sl] * 0.125
        kp = k_ref[:, sl]
        vp = v_ref[:, sl]
        outs = []
        for sub in range(2):
            keep = (lane < 64) if sub == 0 else (lane >= 64)
            qz = jnp.where(keep, qp, jnp.zeros_like(qp))
            s = lax.dot_general(qz, kp, _NT, preferred_element_type=F32)
            m = s.max(axis=-1, keepdims=True)
            p = jnp.exp(s - m)
            l = p.sum(axis=-1, keepdims=True)
            outs.append(jnp.dot(p.astype(BF16), vp, preferred_element_type=F32) / l)
        o_ref[:, sl] = jnp.where(lane < 64, outs[0], outs[1]).astype(BF16)


def _ctx_dense(p_ctx):
    n_ctx = p_ctx.shape[0]
    blk = lambda col: pl.BlockSpec((n_ctx, BW), lambda i: (0, col))
    return pl.pallas_call(
        _ctx_dense_kernel,
        grid=(1,),
        in_specs=[blk(CB_QA), blk(CB_KA), blk(CB_VA)],
        out_specs=pl.BlockSpec((n_ctx, BW), lambda i: (0, 0)),
        out_shape=jax.ShapeDtypeStruct((n_ctx, BW), BF16),
        compiler_params=_cparams(("arbitrary",)),
        name="ctx_dense_attn",
    )(p_ctx, p_ctx, p_ctx)


def _diff_kernel(*refs, tq, nk, lam_init, with_lat):
    if with_lat:
        (qt_ref, kc_ref, vtc_ref, k_ref, vt_ref, lq1_ref, lk1_ref, lq2_ref, lk2_ref, g_ref, o_ref,
         qz_scr, m_scr, acc_scr, s0_scr, s1_scr, p0_scr, p1_scr) = refs
    else:
        (qt_ref, kc_ref, vtc_ref, lq1_ref, lk1_ref, lq2_ref, lk2_ref, g_ref, o_ref,
         qz_scr, m_scr, acc_scr, s0_scr, s1_scr, p0_scr, p1_scr) = refs
    s_scr = (s0_scr, s1_scr)
    p_scr = (p0_scr, p1_scr)
    j = pl.program_id(1)
    lanes2 = 2 * tq
    rc = max(BF16_ROWS, BF16_ROWS * 1024 // lanes2)
    ck = 256

    def attend_all(k_ref, vt_ref):
        ns = k_ref.shape[0] // ck

        def head_cols(h):
            return slice(h * DA_VD, (h + 1) * DA_VD)

        def run_round(h_qk, h_exp, h_pv, m_b):
            mx = None
            for i in range(ns):
                rows = slice(i * ck, (i + 1) * ck)
                if h_qk is not None:
                    s = jnp.dot(k_ref[rows, head_cols(h_qk)], qz_scr[h_qk], preferred_element_type=F32)
                    s_scr[h_qk % 2][rows, :] = s
                    part = s.reshape(ck // rc, rc, lanes2).max(axis=0)
                    mx = part if mx is None else jnp.maximum(mx, part)
                if h_exp is not None:
                    for c in range(i * ck // rc, (i + 1) * ck // rc):
                        rr = slice(c * rc, (c + 1) * rc)
                        p_scr[h_exp % 2][rr, :] = jnp.exp2(s_scr[h_exp % 2][rr, :] - m_b).astype(BF16)
                if h_pv is not None:
                    vt_ext = jnp.concatenate([vt_ref[head_cols(h_pv), rows], vt_ref[BW:VT_ROWS, rows]], axis=0)
                    acc_scr[h_pv] += jnp.dot(vt_ext, p_scr[h_pv % 2][rows, :], preferred_element_type=F32)
            return mx

        mx = None
        for r in range(DA_HEADS + 2):
            h_qk = r if r < DA_HEADS else None
            h_exp = r - 1 if 0 <= r - 1 < DA_HEADS else None
            h_pv = r - 2 if 0 <= r - 2 < DA_HEADS else None
            m_b = None
            if h_exp is not None:
                m_old = m_scr[h_exp]
                m_new = jnp.maximum(m_old, mx.max(axis=0, keepdims=True))
                m_scr[h_exp] = m_new
                acc_scr[h_exp] = jnp.exp2(m_old - m_new) * acc_scr[h_exp]
                m_b = jnp.broadcast_to(m_new, (rc, lanes2))
            mx = run_round(h_qk, h_exp, h_pv, m_b)

    @pl.when(j == 0)
    def _():
        row = lax.broadcasted_iota(I32, (DA_VD, tq), 0)
        for h in range(DA_HEADS):
            qh = qt_ref[h * DA_VD:(h + 1) * DA_VD, :]
            zero = jnp.zeros_like(qh)
            qz_scr[h] = jnp.concatenate([jnp.where(row < DA_DH, qh, zero),
                                         jnp.where(row >= DA_DH, qh, zero)], axis=1)
        m_scr[...] = jnp.full(m_scr.shape, -jnp.inf, F32)
        acc_scr[...] = jnp.zeros(acc_scr.shape, F32)
        attend_all(kc_ref, vtc_ref)

    if with_lat:
        attend_all(k_ref, vt_ref)

    @pl.when(j == nk - 1)
    def _():
        lam = (jnp.exp(jnp.sum(lq1_ref[...] * lk1_ref[...], keepdims=True))
               - jnp.exp(jnp.sum(lq2_ref[...] * lk2_ref[...], keepdims=True)) + lam_init)
        for h in range(DA_HEADS):
            o = acc_scr[h, 0:DA_VD, :] / acc_scr[h, DA_VD:DA_VD + 1, :]
            od = o[:, :tq] - lam * o[:, tq:]
            ms = jnp.mean(od * od, axis=0, keepdims=True)
            y = od * lax.rsqrt(ms + EPS) * g_ref[...] * (1.0 - lam_init)
            o_ref[:, h * DA_VD:(h + 1) * DA_VD] = y.T.astype(BF16)


def _diff(qt, p_ctx, vt_ctx, lat, lam_vecs, g_col, lam_init, tq, tk):
    n_q, n_ctx = qt.shape[1], p_ctx.shape[0]
    with_lat = lat is not None
    nk = lat[0].shape[0] // tk if with_lat else 1
    vec = pl.BlockSpec((1, DA_DH), lambda i, j: (0, 0))
    in_specs = [pl.BlockSpec((BW, tq), lambda i, j: (0, i)),
                pl.BlockSpec((n_ctx, BW), lambda i, j: (0, CB_KB)),
                pl.BlockSpec((VT_ROWS, n_ctx), lambda i, j: (0, 0))]
    args = [qt, p_ctx, vt_ctx]
    if with_lat:
        in_specs += [pl.BlockSpec((tk, BW), lambda i, j: (j, CB_KB)),
                     pl.BlockSpec((VT_ROWS, tk), lambda i, j: (0, j))]
        args += list(lat)
    in_specs += [vec, vec, vec, vec, pl.BlockSpec((DA_VD, 1), lambda i, j: (0, 0))]
    return pl.pallas_call(
        functools.partial(_diff_kernel, tq=tq, nk=nk, lam_init=lam_init, with_lat=with_lat),
        grid=(n_q // tq, nk),
        in_specs=in_specs,
        out_specs=pl.BlockSpec((tq, BW), lambda i, j: (i, 0)),
        out_shape=jax.ShapeDtypeStruct((n_q, BW), BF16),
        scratch_shapes=[pltpu.VMEM((DA_HEADS, DA_VD, 2 * tq), BF16),
                        pltpu.VMEM((DA_HEADS, 1, 2 * tq), F32),
                        pltpu.VMEM((DA_HEADS, DA_VD + BF16_ROWS, 2 * tq), F32)]
        + [pltpu.VMEM((max(tk, n_ctx), 2 * tq), F32)] * 2
        + [pltpu.VMEM((max(tk, n_ctx), 2 * tq), BF16)] * 2,
        compiler_params=_cparams(("arbitrary", "arbitrary")),
        name="diff_attn",
    )(*args, *lam_vecs, g_col)


def _merge_kernel(x_ref, ya_ref, yb_ref, u_ref, v_ref, ga_ref, gb_ref, gc_ref, mod_ref, wb_ref, wo_ref,
                  lng_ref, lnb_ref, ws_ref, bst_ref, g2_ref, wrt_ref,
                  x1_ref, h2_ref, aff_ref, yc_scr, *, mrow, tm):
    ug = _gelu(u_ref[...].astype(F32))
    vg = _gelu(v_ref[...].astype(F32))
    mu = jnp.mean(vg, axis=-1, keepdims=True)
    var = jnp.mean(jnp.square(vg - mu), axis=-1, keepdims=True)
    vn = ((vg - mu) * lax.rsqrt(var + EPS) * lng_ref[...] + lnb_ref[...]).astype(BF16)
    for ci in range(tm // GM_CHUNK):
        rs = slice(ci * GM_CHUNK, (ci + 1) * GM_CHUNK)
        for g in range(GM_GROUPS):
            cs = slice(g * LANES, (g + 1) * LANES)
            mixed = jnp.dot(ws_ref[g], vn[rs, cs], preferred_element_type=F32) + bst_ref[:, g:g + 1]
            yc_scr[rs, cs] = (ug[rs, cs] * mixed).astype(BF16)

    sig = jax.nn.sigmoid
    m = sig(ga_ref[...].astype(F32)) * jnp.dot(ya_ref[...], wb_ref[0], preferred_element_type=F32)
    m = m + sig(gb_ref[...].astype(F32)) * jnp.dot(yb_ref[...], wb_ref[1], preferred_element_type=F32)
    m = m + sig(gc_ref[...].astype(F32)) * jnp.dot(yc_scr[...], wb_ref[2], preferred_element_type=F32)
    y = jnp.dot(m.astype(BF16), wo_ref[...], preferred_element_type=F32)
    gt1 = mod_ref[mrow:mrow + 1, 2 * D:3 * D]
    x1 = x_ref[...] + gt1 * y
    x1_ref[...] = x1

    ms = jnp.mean(x1 * x1, axis=-1, keepdims=True)
    sh2 = mod_ref[mrow:mrow + 1, 3 * D:4 * D]
    sc2 = mod_ref[mrow:mrow + 1, 4 * D:5 * D]
    h2 = x1 * lax.rsqrt(ms + EPS) * g2_ref[...] * (1.0 + sc2) + sh2
    for s in range(D // LANES):
        h2_ref[pl.ds(s, tm, stride=D // LANES), :] = h2[:, s * LANES:(s + 1) * LANES]

    hh = h2.astype(BF16)
    hl = (h2 - hh.astype(F32)).astype(BF16)
    w = wrt_ref[...]
    wh = w.astype(BF16)
    wl = (w - wh.astype(F32)).astype(BF16)
    lg = (lax.dot_general(wh, hh, _NT, preferred_element_type=F32)
          + lax.dot_general(wh, hl, _NT, preferred_element_type=F32)
          + lax.dot_general(wl, hh, _NT, preferred_element_type=F32))
    e = jnp.exp(lg - lg.max(axis=0, keepdims=True))
    aff_ref[...] = e / e.sum(axis=0, keepdims=True)


def _merge(x, ya, yb, p_all, mod, mrow, wb, wo, lng, lnb, ws, bst, g2, wrt, tm):
    n = x.shape[0]
    const = lambda shape: pl.BlockSpec(shape, lambda i: (0,) * len(shape))
    return pl.pallas_call(
        functools.partial(_merge_kernel, mrow=mrow, tm=tm),
        grid=(n // tm,),
        in_specs=[pl.BlockSpec((tm, D), lambda i: (i, 0)),
                  pl.BlockSpec((tm, BW), lambda i: (i, 0)),
                  pl.BlockSpec((tm, BW), lambda i: (i, 0)),
                  pl.BlockSpec((tm, BW), lambda i: (i, CB_U)),
                  pl.BlockSpec((tm, BW), lambda i: (i, CB_V)),
                  pl.BlockSpec((tm, D), lambda i: (i, 4)),
                  pl.BlockSpec((tm, D), lambda i: (i, 5)),
                  pl.BlockSpec((tm, D), lambda i: (i, 6)),
                  const((SUBLANES, N_MOD * D)),
                  const((3, BW, D)), const((D, D)),
                  const((1, BW)), const((1, BW)),
                  const((GM_GROUPS, GM_CHUNK, GM_CHUNK)), const((GM_CHUNK, GM_GROUPS)),
                  const((1, D)), const((N_EXPERTS, D))],
        out_specs=[pl.BlockSpec((tm, D), lambda i: (i, 0)),
                   pl.BlockSpec((tm * (D // LANES), LANES), lambda i: (i, 0)),
                   pl.BlockSpec((N_EXPERTS, tm), lambda i: (0, i))],
        out_shape=[jax.ShapeDtypeStruct((n, D), F32),
                   jax.ShapeDtypeStruct((n * (D // LANES), LANES), F32),
                   jax.ShapeDtypeStruct((N_EXPERTS, n), F32)],
        scratch_shapes=[pltpu.VMEM((tm, BW), BF16)],
        compiler_params=_cparams(("arbitrary",)),
        name="merge_prenorm",
    )(x, ya, yb, p_all, p_all, p_all, p_all, p_all, mod, wb, wo, lng, lnb, ws, bst, g2, wrt)


def _route_kernel(all_ref, a_ref, idx_ref, gate_ref, slot_ref, rank_ref, thr_scr, *, cap, nrow):
    e = pl.program_id(0)

    @pl.when(e == 0)
    def _():
        bits = pltpu.bitcast(all_ref[...], I32).reshape(N_EXPERTS, nrow, LANES)

        def bisect(k, prefix):
            cand = prefix | jnp.left_shift(jnp.int32(1), 30 - k)
            hit = jnp.where(bits >= cand, 1.0, 0.0)
            cnt = hit.sum(axis=1, keepdims=True).sum(axis=2, keepdims=True)
            return jnp.where(cnt >= cap, cand, prefix)

        thr_bits = lax.fori_loop(0, 31, bisect, jnp.zeros((N_EXPERTS, 1, 1), I32))
        tb = jnp.broadcast_to(thr_bits, (N_EXPERTS, SUBLANES, LANES)).reshape(N_EXPERTS * SUBLANES, LANES)
        thr_scr[...] = pltpu.bitcast(tb, F32).reshape(N_EXPERTS, SUBLANES, LANES)

    a = a_ref[0]
    thr = thr_scr[e][0:1, :]
    gt = a > thr
    eq = a == thr

    r_i = lax.broadcasted_iota(I32, (LANES, LANES), 0)
    c_i = lax.broadcasted_iota(I32, (LANES, LANES), 1)
    upper = (r_i <= c_i).astype(BF16)
    rr = lax.broadcasted_iota(I32, (nrow, nrow), 0)
    rc = lax.broadcasted_iota(I32, (nrow, nrow), 1)
    lower_strict = (rc < rr).astype(BF16)
    upper_strict = (rr < rc).astype(BF16)

    def prefix(mask):
        xf = jnp.where(mask, 1.0, 0.0)
        incl = jnp.dot(xf.astype(BF16), upper, preferred_element_type=F32)
        tot = jnp.broadcast_to(incl[:, LANES - 1:LANES], (nrow, LANES))
        base = jnp.dot(lower_strict, tot.astype(BF16), preferred_element_type=F32)
        return xf, incl, base + incl - xf

    _, _, eq_rank = prefix(eq)
    need = cap - jnp.sum(jnp.where(gt, 1.0, 0.0))
    sel = gt | (eq & (eq_rank < need))
    xf, incl, rank = prefix(sel)
    rank_ref[0] = rank.astype(I32)
    slot_ref[0] = jnp.where(sel, rank, -1.0).astype(I32)

    ones = jnp.ones((SUBLANES, LANES), BF16)
    tot_l = lax.dot_general(ones, xf.astype(BF16), _NT, preferred_element_type=F32)
    off_l = jnp.dot(tot_l.astype(BF16), upper_strict, preferred_element_type=F32)[0:1]
    tot_l = tot_l[0:1]

    jf = lax.broadcasted_iota(I32, (cap, nrow), 0).astype(F32)
    oh_row = (off_l <= jf) & (jf < off_l + tot_l)
    ohb = jnp.where(oh_row, 1.0, 0.0).astype(BF16)
    row_id = lax.broadcasted_iota(I32, (cap, nrow), 1).astype(F32)
    off_j = jnp.sum(jnp.where(oh_row, off_l, 0.0), axis=1, keepdims=True)
    row_j = jnp.sum(jnp.where(oh_row, row_id, 0.0), axis=1, keepdims=True)
    key = jnp.where(sel, incl, 0.0).astype(BF16)
    g = jnp.dot(ohb, key, preferred_element_type=F32)
    target = lax.broadcasted_iota(I32, (cap, 1), 0).astype(F32) - off_j + 1.0
    oh_lane = g == target
    lane_id = lax.broadcasted_iota(I32, (cap, LANES), 1).astype(F32)
    lane_j = jnp.sum(jnp.where(oh_lane, lane_id, 0.0), axis=1, keepdims=True)
    idx_ref[0] = (row_j * LANES + lane_j).astype(I32)

    a1 = a.astype(BF16)
    r1 = a - a1.astype(F32)
    a2 = r1.astype(BF16)
    a3 = (r1 - a2.astype(F32)).astype(BF16)
    arow = (jnp.dot(ohb, a1, preferred_element_type=F32) + jnp.dot(ohb, a2, preferred_element_type=F32)
            + jnp.dot(ohb, a3, preferred_element_type=F32))
    gate_ref[0] = jnp.sum(jnp.where(oh_lane, arow, 0.0), axis=1, keepdims=True)


def _route(aff3, cap):
    nrow = aff3.shape[1]
    return pl.pallas_call(
        functools.partial(_route_kernel, cap=cap, nrow=nrow),
        grid=(N_EXPERTS,),
        in_specs=[pl.BlockSpec((N_EXPERTS * nrow, LANES), lambda e: (0, 0)),
                  pl.BlockSpec((1, nrow, LANES), lambda e: (e, 0, 0))],
        out_specs=[pl.BlockSpec((1, cap, 1), lambda e: (e, 0, 0)),
                   pl.BlockSpec((1, cap, 1), lambda e: (e, 0, 0)),
                   pl.BlockSpec((1, nrow, LANES), lambda e: (e, 0, 0)),
                   pl.BlockSpec((1, nrow, LANES), lambda e: (e, 0, 0))],
        out_shape=[jax.ShapeDtypeStruct((N_EXPERTS, cap, 1), I32),
                   jax.ShapeDtypeStruct((N_EXPERTS, cap, 1), F32),
                   jax.ShapeDtypeStruct((N_EXPERTS, nrow, LANES), I32),
                   jax.ShapeDtypeStruct((N_EXPERTS, nrow, LANES), I32)],
        scratch_shapes=[pltpu.VMEM((N_EXPERTS, SUBLANES, LANES), F32)],
        compiler_params=_cparams(("arbitrary",)),
        name="ec_route",
    )(aff3.reshape(N_EXPERTS * nrow, LANES), aff3)


def _ffn_kernel(idx_ref, h2_hbm, gate_ref, w1_ref, w3_ref, w2_ref, ye_ref, buf, sem, wb_scr, *, tc, nc):
    sub = D // LANES

    @pl.when(pl.program_id(1) == 0)
    def _():
        wb_scr[0] = w1_ref[0, 0].astype(BF16)
        wb_scr[1] = w3_ref[0, 0].astype(BF16)
        wb_scr[2] = w2_ref[0, 0].astype(BF16)

    step = pl.program_id(0) * nc + pl.program_id(1)
    nsteps = N_EXPERTS * nc

    def row_copy(tok, r, slot):
        return pltpu.make_async_copy(h2_hbm.at[pl.ds(pl.multiple_of(tok * sub, sub), sub), :],
                                     buf.at[slot, pl.ds(pl.multiple_of(r * sub, sub), sub), :],
                                     sem.at[slot])

    def wait_slot(slot):
        pltpu.make_async_copy(h2_hbm.at[pl.ds(0, tc * sub), :], buf.at[slot], sem.at[slot]).wait()

    @pl.when(step == 0)
    def _():
        def body(r, carry):
            row_copy(idx_ref[0, r], r, 0).start()
            return carry

        lax.fori_loop(0, tc, body, 0, unroll=16)

    nxt = jnp.minimum(step + 1, nsteps - 1)
    nxt_e = nxt // nc
    nxt_base = (nxt % nc) * tc
    nslot = (step + 1) % 2
    parts = 4 if tc % 4 == 0 else 1

    def issue_part(q):
        for r in range(q * tc // parts, (q + 1) * tc // parts):
            row_copy(idx_ref[nxt_e, nxt_base + r], r, nslot).start(priority=r % 2)

    slot = step % 2
    wait_slot(slot)
    xin = jnp.concatenate([buf[slot, pl.ds(s, tc, stride=sub), :] for s in range(sub)], axis=1).astype(BF16)
    issue_part(0)
    a = jnp.dot(xin, wb_scr[0], preferred_element_type=F32)
    if parts == 4:
        issue_part(1)
    b = jnp.dot(xin, wb_scr[1], preferred_element_type=F32)
    hid = (a * jax.nn.sigmoid(a) * b).astype(BF16)
    if parts == 4:
        issue_part(2)
    y = jnp.dot(hid, wb_scr[2], preferred_element_type=F32)
    if parts == 4:
        issue_part(3)
    ye_ref[0] = (y * gate_ref[0]).astype(BF16)

    @pl.when(step == nsteps - 1)
    def _():
        wait_slot(nslot)


def _ffn(idx, gate, h2, w1, w3, w2, layer, tc):
    cap = idx.shape[1]
    nc = cap // tc
    sub = D // LANES
    wspec = pl.BlockSpec((1, 1, D, D), lambda e, c, idx: (layer, e, 0, 0))
    grid_spec = pltpu.PrefetchScalarGridSpec(
        num_scalar_prefetch=1,
        grid=(N_EXPERTS, nc),
        in_specs=[pl.BlockSpec(memory_space=pl.ANY),
                  pl.BlockSpec((1, tc, 1), lambda e, c, idx: (e, c, 0)),
                  wspec, wspec, wspec],
        out_specs=pl.BlockSpec((1, tc, D), lambda e, c, idx: (e, c, 0)),
        scratch_shapes=[pltpu.VMEM((2, tc * sub, LANES), F32),
                        pltpu.SemaphoreType.DMA((2,)),
                        pltpu.VMEM((3, D, D), BF16)])
    return pl.pallas_call(
        functools.partial(_ffn_kernel, tc=tc, nc=nc),
        grid_spec=grid_spec,
        out_shape=jax.ShapeDtypeStruct((N_EXPERTS, cap, D), BF16),
        compiler_params=_cparams(("arbitrary", "arbitrary")),
        name="expert_ffn",
    )(idx, h2, gate, w1, w3, w2)


def _combine_kernel(rs_ref, x_ref, slot_ref, mod_ref, gf_ref, ye_hbm, o_ref, win, sem, y_scr,
                    *, win_rows, cap, nt, tt, mrow, final):
    i = pl.program_id(0)

    def start_of(e, t):
        s0 = rs_ref[e, t]
        s_al = (s0 // BF16_ROWS) * BF16_ROWS
        return pl.multiple_of(jnp.minimum(s_al, cap - win_rows), BF16_ROWS)

    def copy(e, t, sl):
        return pltpu.make_async_copy(ye_hbm.at[e, pl.ds(start_of(e, t), win_rows), :],
                                     win.at[sl, e], sem.at[sl])

    @pl.when(i == 0)
    def _():
        for e in range(N_EXPERTS):
            copy(e, i, 0).start()

    @pl.when(i + 1 < nt)
    def _():
        for e in range(N_EXPERTS):
            copy(e, i + 1, (i + 1) % 2).start()

    sl = i % 2
    for e in range(N_EXPERTS):
        copy(e, i, sl).wait()

    def window_sum(r0, r1):
        acc = jnp.zeros((tt, D), F32)
        wi = lax.broadcasted_iota(I32, (r1 - r0, tt), 0) + r0
        for e in range(N_EXPERTS):
            rel = slot_ref[e:e + 1, :] - start_of(e, i)
            oh_t = jnp.where(wi == rel, 1.0, 0.0).astype(BF16)
            acc = acc + lax.dot_general(oh_t, win[sl, e, r0:r1, :], _TN, preferred_element_type=F32)
        return acc

    main_rows = min(win_rows, tt)
    gt2 = mod_ref[mrow:mrow + 1, 5 * D:6 * D]
    y_scr[...] = x_ref[...] + gt2 * window_sum(0, main_rows)
    if win_rows > main_rows:
        need_tail = False
        for e in range(N_EXPERTS):
            end = rs_ref[e, i + 1] - start_of(e, i)
            need_tail = jnp.logical_or(need_tail, end > main_rows)

        @pl.when(need_tail)
        def _():
            y_scr[...] += gt2 * window_sum(main_rows, win_rows)

    y = y_scr[...]
    if final:
        ms = jnp.mean(y * y, axis=-1, keepdims=True)
        y = y * lax.rsqrt(ms + EPS) * gf_ref[...]
    o_ref[...] = y


def _combine(rstart, x1, slot, mod, mrow, g_final, ye, tt, final):
    n = x1.shape[0]
    cap = ye.shape[1]
    nt = n // tt
    win_rows = min(cap, tt + BF16_ROWS)
    grid_spec = pltpu.PrefetchScalarGridSpec(
        num_scalar_prefetch=1,
        grid=(nt,),
        in_specs=[pl.BlockSpec((tt, D), lambda i, rs: (i, 0)),
                  pl.BlockSpec((N_EXPERTS, tt), lambda i, rs: (0, i)),
                  pl.BlockSpec((SUBLANES, N_MOD * D), lambda i, rs: (0, 0)),
                  pl.BlockSpec((1, D), lambda i, rs: (0, 0)),
                  pl.BlockSpec(memory_space=pl.ANY)],
        out_specs=pl.BlockSpec((tt, D), lambda i, rs: (i, 0)),
        scratch_shapes=[pltpu.VMEM((2, N_EXPERTS, win_rows, D), BF16),
                        pltpu.SemaphoreType.DMA((2,)),
                        pltpu.VMEM((tt, D), F32)])
    return pl.pallas_call(
        functools.partial(_combine_kernel, win_rows=win_rows, cap=cap, nt=nt, tt=tt, mrow=mrow, final=final),
        grid_spec=grid_spec,
        out_shape=jax.ShapeDtypeStruct((n, D), F32),
        compiler_params=_cparams(("arbitrary",)),
        name="moe_combine",
    )(rstart, x1, slot, mod, g_final, ye)


def _rope_tables(n_lat):
    rows = n_lat // GRID_W
    inv = ROPE_THETA ** (-jnp.arange(0, ROPE_AXIS, 2, dtype=F32) / ROPE_AXIS)
    ang_r = jnp.arange(rows).astype(F32)[:, None] * inv
    ang_c = jnp.arange(GRID_W).astype(F32)[:, None] * inv
    cr, sr, cc, sc = jnp.cos(ang_r), jnp.sin(ang_r), jnp.cos(ang_c), jnp.sin(ang_c)
    zr, zc = jnp.zeros_like(sr), jnp.zeros_like(sc)
    rep = lambda parts: jnp.concatenate(parts * (LANES // DA_DH), axis=1)

    def table(row_parts, col_parts):
        r = rep([row_parts[0], row_parts[1], zr, zr])
        c = rep([zc, zc, col_parts[0], col_parts[1]])
        return (r[:, None, :] + c[None, :, :]).reshape(n_lat, LANES)

    return (table((cr, cr), (cc, cc)), table((zr, sr), (zc, sc)), table((-sr, zr), (-sc, zc)))


def _largest_tile(n, unit, limit):
    best = unit
    for k in range(1, n // unit + 1):
        if n % (k * unit) == 0 and k * unit <= limit:
            best = k * unit
    return best


def _moe(x1, h2, aff_t, mod, mrow, w1, w3, w2, layer, g_final, final):
    n = x1.shape[0]
    cap = max(1, EC_CAPACITY * n // N_EXPERTS)
    tile = LANES * LANES if n <= LANES * LANES else n
    n_pad = max(n, tile)
    if n_pad > n:
        aff_t = jnp.concatenate([aff_t, jnp.full((N_EXPERTS, n_pad - n), -1.0, F32)], axis=1)
    idx, gate, slot, rank = _route(aff_t.reshape(N_EXPERTS, n_pad // LANES, LANES), cap)
    tt = min(256, n)
    slot = slot.reshape(N_EXPERTS, n_pad)[:, :n]
    rstart = jnp.concatenate([rank.reshape(N_EXPERTS, n_pad)[:, 0:n:tt],
                              jnp.full((N_EXPERTS, 1), cap, I32)], axis=1)
    tc = min(cap, 512)
    ye = _ffn(idx.reshape(N_EXPERTS, cap), gate, h2, w1, w3, w2, layer, tc)
    return _combine(rstart, x1, slot, mod, mrow, g_final, ye, tt, final)


def kernel(x, c, ctx, c_ctx, w_ada, b_ada, g_norm1, g_norm2, w_in, na_rpb, da_lam_q1, da_lam_k1, da_lam_q2,
           da_lam_k2, da_subln_g, gm_ln_g, gm_ln_b, gm_w_s, gm_b_s, w_branch, w_out, w_router, w_e1, w_e3,
           w_e2, g_final):
    depth = w_ada.shape[0]
    n_lat, n_ctx = x.shape[1], ctx.shape[1]
    rows = n_lat // GRID_W
    xs, xc = x[0], ctx[0]

    cc = jnp.concatenate([c.reshape(1, D), c_ctx.reshape(1, D), jnp.zeros((SUBLANES - 2, D), F32)], axis=0)
    mods = _ada(cc, w_ada, b_ada)

    tabs_lat = _rope_tables(n_lat)
    tabs_ctx = (jnp.ones((n_ctx, LANES), F32), jnp.zeros((n_ctx, LANES), F32), jnp.zeros((n_ctx, LANES), F32))
    tm_proj = _largest_tile(n_lat, 256, 1024)
    tk = _largest_tile(n_lat, 256, 2048)
    tq = 512

    for i in range(depth):
        last = i == depth - 1
        lam_init = 0.8 - 0.6 * math.exp(-0.3 * i)
        mod = mods[i]
        w_in_b = w_in[i].astype(BF16)
        g1 = g_norm1[i].reshape(1, D)
        g2 = g_norm2[i].reshape(1, D)
        gf = g_final.reshape(1, D)
        lam_vecs = [v[i].reshape(1, DA_DH).astype(F32) for v in (da_lam_q1, da_lam_k1, da_lam_q2, da_lam_k2)]
        g_col = da_subln_g[i].reshape(DA_VD, 1)
        merge_w = (w_branch[i].astype(BF16), w_out[i].astype(BF16), gm_ln_g[i].reshape(1, BW),
                   gm_ln_b[i].reshape(1, BW), gm_w_s[i].astype(BF16), gm_b_s[i].T, g2, w_router[i].T)

        p_lat, qt_lat, vt_lat = _proj(xs, g1, mod, w_in_b, tabs_lat, 0, tm_proj)
        p_ctx, qt_ctx, vt_ctx = _proj(xc, g1, mod, w_in_b, tabs_ctx, 1, n_ctx)

        ya = _na(p_lat, p_ctx, _na_bias(na_rpb[i], rows))
        yb = _diff(qt_lat, p_ctx, vt_ctx, (p_lat, vt_lat), lam_vecs, g_col, lam_init, tq, tk)
        x1, h2, aff_t = _merge(xs, ya, yb, p_lat, mod, 0, *merge_w, tm=_largest_tile(n_lat, 256, 512))
        xs = _moe(x1, h2, aff_t, mod, 0, w_e1, w_e3, w_e2, i, gf, last)

        if not last:
            yac = _ctx_dense(p_ctx)
            ybc = _diff(qt_ctx, p_ctx, vt_ctx, None, lam_vecs, g_col, lam_init, n_ctx, n_ctx)
            x1c, h2c, aff_tc = _merge(xc, yac, ybc, p_ctx, mod, 1, *merge_w, tm=n_ctx)
            xc = _moe(x1c, h2c, aff_tc, mod, 1, w_e1, w_e3, w_e2, i, gf, False)
    return xs[None]
```

```python
import functools
import math

import numpy as np
import jax
import jax.numpy as jnp
from jax import lax
from jax.experimental import pallas as pl
from jax.experimental.pallas import tpu as pltpu

F32 = jnp.float32
BF16 = jnp.bfloat16
I32 = jnp.int32

D = 1024
GRID_W = 64
BW = 512
N_COLBLK = 14
NA_HEADS = 8
NA_WIN_R = 8
NA_WIN_C = 16
NA_ROWS = 4
NA_KROWS = 12
DA_HEADS = 4
DA_DH = 64
DA_VD = 128
GM_GROUPS = 4
GM_CHUNK = 128
N_EXPERTS = 16
EC_CAPACITY = 2
ROPE_THETA = 10000.0
ROPE_AXIS = DA_DH // 2
N_MOD = 6
EPS = 1e-6
LANES = 128
SUBLANES = 8
BF16_ROWS = 16
NEG_BIG = -1e30
VT_ROWS = BW + BF16_ROWS
Q_SCALE_LOG2 = DA_DH ** -0.5 * math.log2(math.e)
VMEM_LIMIT = 56 * 1024 * 1024
STALE_MARGIN = 64.0

CB_KA, CB_VA, CB_KB, CB_VB, CB_QA, CB_QB, CB_U, CB_V = range(8)

_NT = (((1,), (1,)), ((), ()))
_TN = (((0,), (0,)), ((), ()))


def _cparams(sem):
    return pltpu.CompilerParams(dimension_semantics=sem, vmem_limit_bytes=VMEM_LIMIT)


def _gelu(x):
    return 0.5 * x * (1.0 + jnp.tanh(math.sqrt(2.0 / math.pi) * (x + 0.044715 * (x * x * x))))


def _ada_kernel(c_ref, w_ref, b_ref, o_ref):
    c = c_ref[...]
    s = c * jax.nn.sigmoid(c)
    o_ref[0] = jnp.dot(s, w_ref[0], preferred_element_type=F32,
                       precision=lax.Precision.HIGHEST) + b_ref[0]


def _ada(cc, w_ada, b_ada):
    depth = w_ada.shape[0]
    tn = D
    return pl.pallas_call(
        _ada_kernel,
        grid=(depth, N_MOD * D // tn),
        in_specs=[pl.BlockSpec((SUBLANES, D), lambda i, j: (0, 0)),
                  pl.BlockSpec((1, D, tn), lambda i, j: (i, 0, j)),
                  pl.BlockSpec((1, 1, tn), lambda i, j: (i, 0, j))],
        out_specs=pl.BlockSpec((1, SUBLANES, tn), lambda i, j: (i, 0, j)),
        out_shape=jax.ShapeDtypeStruct((depth, SUBLANES, N_MOD * D), F32),
        compiler_params=_cparams(("arbitrary", "arbitrary")),
        name="ada_mod",
    )(cc, w_ada, b_ada.reshape(depth, 1, N_MOD * D))


def _proj_kernel(x_ref, g_ref, mod_ref, w_ref, c_ref, sp_ref, sm_ref, p_ref, qt_ref, vt_ref, h_scr, *, mrow):
    j = pl.program_id(1)

    @pl.when(j == 0)
    def _():
        x = x_ref[...]
        ms = jnp.mean(x * x, axis=-1, keepdims=True)
        xn = x * lax.rsqrt(ms + EPS) * g_ref[...]
        sh = mod_ref[mrow:mrow + 1, 0:D]
        sc = mod_ref[mrow:mrow + 1, D:2 * D]
        h_scr[...] = (xn * (1.0 + sc) + sh).astype(BF16)

    z = jnp.dot(h_scr[...], w_ref[...], preferred_element_type=F32)

    def rope(z):
        reps = BW // LANES
        c = jnp.concatenate([c_ref[...]] * reps, axis=1)
        sp = jnp.concatenate([sp_ref[...]] * reps, axis=1)
        sm = jnp.concatenate([sm_ref[...]] * reps, axis=1)
        half = ROPE_AXIS // 2
        return z * c + pltpu.roll(z, half, 1) * sp + pltpu.roll(z, BW - half, 1) * sm

    @pl.when(j == CB_KB // 2)
    def _():
        p_ref[:, 0:BW] = rope(z[:, 0:BW]).astype(BF16)
        zv = z[:, BW:2 * BW]
        p_ref[:, BW:2 * BW] = zv.astype(BF16)
        vt_ref[0:BW, :] = zv.T.astype(BF16)
        vt_ref[BW:VT_ROWS, :] = jnp.ones((VT_ROWS - BW, z.shape[0]), BF16)

    @pl.when(j == CB_QB // 2)
    def _():
        p_ref[:, 0:BW] = z[:, 0:BW].astype(BF16)
        zr = rope(z[:, BW:2 * BW])
        p_ref[:, BW:2 * BW] = zr.astype(BF16)
        qt_ref[...] = (zr * Q_SCALE_LOG2).T.astype(BF16)

    @pl.when((j != CB_KB // 2) & (j != CB_QB // 2))
    def _():
        p_ref[...] = z.astype(BF16)


def _proj(x, g, mod, w, tabs, mrow, tm):
    n = x.shape[0]
    return pl.pallas_call(
        functools.partial(_proj_kernel, mrow=mrow),
        grid=(n // tm, N_COLBLK // 2),
        in_specs=[pl.BlockSpec((tm, D), lambda i, j: (i, 0)),
                  pl.BlockSpec((1, D), lambda i, j: (0, 0)),
                  pl.BlockSpec((SUBLANES, N_MOD * D), lambda i, j: (0, 0)),
                  pl.BlockSpec((D, 2 * BW), lambda i, j: (0, j)),
                  pl.BlockSpec((tm, LANES), lambda i, j: (i, 0)),
                  pl.BlockSpec((tm, LANES), lambda i, j: (i, 0)),
                  pl.BlockSpec((tm, LANES), lambda i, j: (i, 0))],
        out_specs=[pl.BlockSpec((tm, 2 * BW), lambda i, j: (i, j)),
                   pl.BlockSpec((BW, tm), lambda i, j: (0, i)),
                   pl.BlockSpec((VT_ROWS, tm), lambda i, j: (0, i))],
        out_shape=[jax.ShapeDtypeStruct((n, N_COLBLK * BW), BF16),
                   jax.ShapeDtypeStruct((BW, n), BF16),
                   jax.ShapeDtypeStruct((VT_ROWS, n), BF16)],
        scratch_shapes=[pltpu.VMEM((tm, D), BF16)],
        compiler_params=_cparams(("arbitrary", "arbitrary")),
        name="proj",
    )(x, g, mod, w, *tabs)


def _na_kernel(q_ref, k0_ref, k1_ref, k2_ref, v0_ref, v1_ref, v2_ref, kc_ref, vc_ref, bias_ref, o_ref):
    nq = q_ref.shape[0]
    lane = lax.broadcasted_iota(I32, (nq, LANES), 1)
    k_refs = (kc_ref, k0_ref, k1_ref, k2_ref)
    v_refs = (vc_ref, v0_ref, v1_ref, v2_ref)
    for g in range(NA_HEADS // 2):
        sl = slice(g * LANES, (g + 1) * LANES)
        qp = q_ref[:, sl] * 0.125
        ks = [r[:, sl] for r in k_refs]
        vs = [r[:, sl] for r in v_refs]
        outs = []
        for sub in range(2):
            h = 2 * g + sub
            keep = (lane < 64) if sub == 0 else (lane >= 64)
            qz = jnp.where(keep, qp, jnp.zeros_like(qp))
            ss = [lax.dot_general(qz, ks[0], _NT, preferred_element_type=F32)]
            for t in range(3):
                s = lax.dot_general(qz, ks[1 + t], _NT, preferred_element_type=F32)
                ss.append(s + bias_ref[0, h, :, t * nq:(t + 1) * nq])
            m = ss[0].max(axis=-1, keepdims=True)
            for s in ss[1:]:
                m = jnp.maximum(m, s.max(axis=-1, keepdims=True))
            l = jnp.zeros_like(m)
            o = jnp.zeros((nq, LANES), F32)
            for s, v in zip(ss, vs):
                p = jnp.exp(s - m)
                l = l + p.sum(axis=-1, keepdims=True)
                o = o + jnp.dot(p.astype(BF16), v, preferred_element_type=F32)
            outs.append(o / l)
        o_ref[:, sl] = jnp.where(lane < 64, outs[0], outs[1]).astype(BF16)


def _na_bias(rpb, rows):
    nb = rows // NA_ROWS
    c = np.arange(GRID_W)[:, None]
    kc = np.arange(GRID_W)[None, :]
    c0 = np.clip(c - NA_WIN_C // 2, 0, GRID_W - NA_WIN_C)
    col_ok = (kc >= c0) & (kc < c0 + NA_WIN_C)
    dc = kc - c + (NA_WIN_C - 1)
    sel = np.stack([(dc == d) & col_ok for d in range(2 * NA_WIN_C - 1)]).astype(np.float32)
    toep = jnp.einsum("hrd,dck->hrck", rpb.astype(F32), sel, precision=lax.Precision.HIGHEST)
    toep = toep + np.where(col_ok, 0.0, NEG_BIG).astype(np.float32)
    n_dr = 2 * NA_WIN_R - 1
    toep = jnp.concatenate([toep, jnp.full((NA_HEADS, 1, GRID_W, GRID_W), NEG_BIG, F32)], axis=1)
    which = np.full((3, NA_ROWS, NA_KROWS), n_dr, np.int32)
    for v, b in enumerate((0, 1, nb - 1)):
        kb0 = min(max(b - 1, 0), nb - 3)
        for a in range(NA_ROWS):
            r = NA_ROWS * b + a
            r0 = min(max(r - NA_WIN_R // 2, 0), rows - NA_WIN_R)
            for i in range(NA_KROWS):
                kr = NA_ROWS * kb0 + i
                if r0 <= kr < r0 + NA_WIN_R:
                    which[v, a, i] = kr - r + NA_WIN_R - 1
    blocks = jnp.take(toep, which.reshape(-1), axis=1)
    blocks = blocks.reshape(NA_HEADS, 3, NA_ROWS, NA_KROWS, GRID_W, GRID_W)
    return blocks.transpose(1, 0, 2, 4, 3, 5).reshape(3, NA_HEADS, NA_ROWS * GRID_W, NA_KROWS * GRID_W)


def _na(p_lat, p_ctx, bias):
    nq = NA_ROWS * GRID_W
    n_lat, n_ctx = p_lat.shape[0], p_ctx.shape[0]
    nb = n_lat // nq

    def kmap(t, col):
        return lambda b: (jnp.clip(b - 1, 0, nb - 3) + t, col)

    def bmap(b):
        return (jnp.where(b == 0, 0, jnp.where(b == nb - 1, 2, 1)), 0, 0, 0)

    blk = lambda f: pl.BlockSpec((nq, BW), f)
    return pl.pallas_call(
        _na_kernel,
        grid=(nb,),
        in_specs=[blk(lambda b: (b, CB_QA)),
                  blk(kmap(0, CB_KA)), blk(kmap(1, CB_KA)), blk(kmap(2, CB_KA)),
                  blk(kmap(0, CB_VA)), blk(kmap(1, CB_VA)), blk(kmap(2, CB_VA)),
                  pl.BlockSpec((n_ctx, BW), lambda b: (0, CB_KA)),
                  pl.BlockSpec((n_ctx, BW), lambda b: (0, CB_VA)),
                  pl.BlockSpec((1, NA_HEADS, nq, NA_KROWS * GRID_W), bmap)],
        out_specs=pl.BlockSpec((nq, BW), lambda b: (b, 0)),
        out_shape=jax.ShapeDtypeStruct((n_lat, BW), BF16),
        compiler_params=_cparams(("arbitrary",)),
        name="na_attn",
    )(p_lat, p_lat, p_lat, p_lat, p_lat, p_lat, p_lat, p_ctx, p_ctx, bias)


def _ctx_dense_kernel(q_ref, k_ref, v_ref, o_ref):
    nq = q_ref.shape[0]
    lane = lax.broadcasted_iota(I32, (nq, LANES), 1)
    for g in range(NA_HEADS // 2):
        sl = slice(g * LANES, (g + 1) * LANES)
        qp = q_ref[:, sl] * 0.125
        kp = k_ref[:, sl]
        vp = v_ref[:, sl]
        outs = []
        for sub in range(2):
            keep = (lane < 64) if sub == 0 else (lane >= 64)
            qz = jnp.where(keep, qp, jnp.zeros_like(qp))
            s = lax.dot_general(qz, kp, _NT, preferred_element_type=F32)
            m = s.max(axis=-1, keepdims=True)
            p = jnp.exp(s - m)
            l = p.sum(axis=-1, keepdims=True)
            outs.append(jnp.dot(p.astype(BF16), vp, preferred_element_type=F32) / l)
        o_ref[:, sl] = jnp.where(lane < 64, outs[0], outs[1]).astype(BF16)


def _ctx_dense(p_ctx):
    n_ctx = p_ctx.shape[0]
    blk = lambda col: pl.BlockSpec((n_ctx, BW), lambda i: (0, col))
    return pl.pallas_call(
        _ctx_dense_kernel,
        grid=(1,),
        in_specs=[blk(CB_QA), blk(CB_KA), blk(CB_VA)],
        out_specs=pl.BlockSpec((n_ctx, BW), lambda i: (0, 0)),
        out_shape=jax.ShapeDtypeStruct((n_ctx, BW), BF16),
        compiler_params=_cparams(("arbitrary",)),
        name="ctx_dense_attn",
    )(p_ctx, p_ctx, p_ctx)


def _diff_kernel(*refs, tq, nk, lam_init, with_lat):
    if with_lat:
        (qt_ref, kc_ref, vtc_ref, k_ref, vt_ref, lq1_ref, lk1_ref, lq2_ref, lk2_ref, g_ref, o_ref,
         qz_scr, m_scr, acc_scr, bak_scr, s0_scr, s1_scr, p0_scr, p1_scr) = refs
    else:
        (qt_ref, kc_ref, vtc_ref, lq1_ref, lk1_ref, lq2_ref, lk2_ref, g_ref, o_ref,
         qz_scr, m_scr, acc_scr, bak_scr, s0_scr, s1_scr, p0_scr, p1_scr) = refs
    s_scr = (s0_scr, s1_scr)
    p_scr = (p0_scr, p1_scr)
    j = pl.program_id(1)
    lanes2 = 2 * tq
    rc = max(BF16_ROWS, BF16_ROWS * 1024 // lanes2)
    ck = 256

    def attend_all(k_ref, vt_ref):
        ns = k_ref.shape[0] // ck

        def head_cols(h):
            return slice(h * DA_VD, (h + 1) * DA_VD)

        def run_round(h_qk, h_exp, h_pv, m_b):
            mx = None
            for i in range(ns):
                rows = slice(i * ck, (i + 1) * ck)
                if h_qk is not None:
                    s = jnp.dot(k_ref[rows, head_cols(h_qk)], qz_scr[h_qk], preferred_element_type=F32)
                    s_scr[h_qk % 2][rows, :] = s
                    part = s.reshape(ck // rc, rc, lanes2).max(axis=0)
                    mx = part if mx is None else jnp.maximum(mx, part)
                if h_exp is not None:
                    for c in range(i * ck // rc, (i + 1) * ck // rc):
                        rr = slice(c * rc, (c + 1) * rc)
                        p_scr[h_exp % 2][rr, :] = jnp.exp2(s_scr[h_exp % 2][rr, :] - m_b).astype(BF16)
                if h_pv is not None:
                    vt_ext = jnp.concatenate([vt_ref[head_cols(h_pv), rows], vt_ref[BW:VT_ROWS, rows]], axis=0)
                    acc_scr[h_pv] += jnp.dot(vt_ext, p_scr[h_pv % 2][rows, :], preferred_element_type=F32)
            return mx

        mx = None
        for r in range(DA_HEADS + 2):
            h_qk = r if r < DA_HEADS else None
            h_exp = r - 1 if 0 <= r - 1 < DA_HEADS else None
            h_pv = r - 2 if 0 <= r - 2 < DA_HEADS else None
            m_b = None
            if h_exp is not None:
                m_old = m_scr[h_exp]
                m_new = jnp.maximum(m_old, mx.max(axis=0, keepdims=True))
                m_scr[h_exp] = m_new
                acc_scr[h_exp] = jnp.exp2(m_old - m_new) * acc_scr[h_exp]
                m_b = jnp.broadcast_to(m_new, (rc, lanes2))
            mx = run_round(h_qk, h_exp, h_pv, m_b)

    @pl.when(j == 0)
    def _():
        row = lax.broadcasted_iota(I32, (DA_VD, tq), 0)
        for h in range(DA_HEADS):
            qh = qt_ref[h * DA_VD:(h + 1) * DA_VD, :]
            zero = jnp.zeros_like(qh)
            qz_scr[h] = jnp.concatenate([jnp.where(row < DA_DH, qh, zero),
                                         jnp.where(row >= DA_DH, qh, zero)], axis=1)
        m_scr[...] = jnp.full(m_scr.shape, -jnp.inf, F32)
        acc_scr[...] = jnp.zeros(acc_scr.shape, F32)
        attend_all(kc_ref, vtc_ref)

    def attend_stale(k_ref, vt_ref):
        ns = k_ref.shape[0] // ck
        maxima = []
        for h in range(DA_HEADS):
            cols = slice(h * DA_VD, (h + 1) * DA_VD)
            m_b = jnp.broadcast_to(m_scr[h], (ck, lanes2))
            mx = None
            for i in range(ns):
                rows = slice(i * ck, (i + 1) * ck)
                s = jnp.dot(k_ref[rows, cols], qz_scr[h], preferred_element_type=F32)
                part = s.reshape(ck // rc, rc, lanes2).max(axis=0)
                mx = part if mx is None else jnp.maximum(mx, part)
                p = jnp.exp2(s - m_b).astype(BF16)
                vt_ext = jnp.concatenate([vt_ref[cols, rows], vt_ref[BW:VT_ROWS, rows]], axis=0)
                acc_scr[h] += jnp.dot(vt_ext, p, preferred_element_type=F32)
            maxima.append(mx.max(axis=0, keepdims=True))
        return maxima

    if with_lat:
        bak_scr[...] = acc_scr[...]
        maxima = attend_stale(k_ref, vt_ref)
        excess = maxima[0] - m_scr[0]
        for h in range(1, DA_HEADS):
            excess = jnp.maximum(excess, maxima[h] - m_scr[h])
        safe = jnp.max(excess) <= STALE_MARGIN

        @pl.when(safe)
        def _():
            for h in range(DA_HEADS):
                m_old = m_scr[h]
                m_new = jnp.maximum(m_old, maxima[h])
                m_scr[h] = m_new
                acc_scr[h] = jnp.exp2(m_old - m_new) * acc_scr[h]

        @pl.when(jnp.logical_not(safe))
        def _():
            acc_scr[...] = bak_scr[...]
            attend_all(k_ref, vt_ref)

    @pl.when(j == nk - 1)
    def _():
        lam = (jnp.exp(jnp.sum(lq1_ref[...] * lk1_ref[...], keepdims=True))
               - jnp.exp(jnp.sum(lq2_ref[...] * lk2_ref[...], keepdims=True)) + lam_init)
        for h in range(DA_HEADS):
            o = acc_scr[h, 0:DA_VD, :] / acc_scr[h, DA_VD:DA_VD + 1, :]
            od = o[:, :tq] - lam * o[:, tq:]
            ms = jnp.mean(od * od, axis=0, keepdims=True)
            y = od * lax.rsqrt(ms + EPS) * g_ref[...] * (1.0 - lam_init)
            o_ref[:, h * DA_VD:(h + 1) * DA_VD] = y.T.astype(BF16)


def _diff(qt, p_ctx, vt_ctx, lat, lam_vecs, g_col, lam_init, tq, tk):
    n_q, n_ctx = qt.shape[1], p_ctx.shape[0]
    with_lat = lat is not None
    nk = lat[0].shape[0] // tk if with_lat else 1
    vec = pl.BlockSpec((1, DA_DH), lambda i, j: (0, 0))
    in_specs = [pl.BlockSpec((BW, tq), lambda i, j: (0, i)),
                pl.BlockSpec((n_ctx, BW), lambda i, j: (0, CB_KB)),
                pl.BlockSpec((VT_ROWS, n_ctx), lambda i, j: (0, 0))]
    args = [qt, p_ctx, vt_ctx]
    if with_lat:
        in_specs += [pl.BlockSpec((tk, BW), lambda i, j: (j, CB_KB)),
                     pl.BlockSpec((VT_ROWS, tk), lambda i, j: (0, j))]
        args += list(lat)
    in_specs += [vec, vec, vec, vec, pl.BlockSpec((DA_VD, 1), lambda i, j: (0, 0))]
    return pl.pallas_call(
        functools.partial(_diff_kernel, tq=tq, nk=nk, lam_init=lam_init, with_lat=with_lat),
        grid=(n_q // tq, nk),
        in_specs=in_specs,
        out_specs=pl.BlockSpec((tq, BW), lambda i, j: (i, 0)),
        out_shape=jax.ShapeDtypeStruct((n_q, BW), BF16),
        scratch_shapes=[pltpu.VMEM((DA_HEADS, DA_VD, 2 * tq), BF16),
                        pltpu.VMEM((DA_HEADS, 1, 2 * tq), F32),
                        pltpu.VMEM((DA_HEADS, DA_VD + BF16_ROWS, 2 * tq), F32),
                        pltpu.VMEM((DA_HEADS, DA_VD + BF16_ROWS, 2 * tq), F32)]
        + [pltpu.VMEM((max(tk, n_ctx), 2 * tq), F32)] * 2
        + [pltpu.VMEM((max(tk, n_ctx), 2 * tq), BF16)] * 2,
        compiler_params=_cparams(("arbitrary", "arbitrary")),
        name="diff_attn",
    )(*args, *lam_vecs, g_col)


def _merge_kernel(x_ref, ya_ref, yb_ref, u_ref, v_ref, ga_ref, gb_ref, gc_ref, mod_ref, wb_ref, wo_ref,
                  lng_ref, lnb_ref, ws_ref, bst_ref, g2_ref, wrt_ref,
                  x1_ref, h2_ref, aff_ref, yc_scr, *, mrow, tm):
    ug = _gelu(u_ref[...].astype(F32))
    vg = _gelu(v_ref[...].astype(F32))
    mu = jnp.mean(vg, axis=-1, keepdims=True)
    var = jnp.mean(jnp.square(vg - mu), axis=-1, keepdims=True)
    vn = ((vg - mu) * lax.rsqrt(var + EPS) * lng_ref[...] + lnb_ref[...]).astype(BF16)
    for ci in range(tm // GM_CHUNK):
        rs = slice(ci * GM_CHUNK, (ci + 1) * GM_CHUNK)
        for g in range(GM_GROUPS):
            cs = slice(g * LANES, (g + 1) * LANES)
            mixed = jnp.dot(ws_ref[g], vn[rs, cs], preferred_element_type=F32) + bst_ref[:, g:g + 1]
            yc_scr[rs, cs] = (ug[rs, cs] * mixed).astype(BF16)

    sig = jax.nn.sigmoid
    m = sig(ga_ref[...].astype(F32)) * jnp.dot(ya_ref[...], wb_ref[0], preferred_element_type=F32)
    m = m + sig(gb_ref[...].astype(F32)) * jnp.dot(yb_ref[...], wb_ref[1], preferred_element_type=F32)
    m = m + sig(gc_ref[...].astype(F32)) * jnp.dot(yc_scr[...], wb_ref[2], preferred_element_type=F32)
    y = jnp.dot(m.astype(BF16), wo_ref[...], preferred_element_type=F32)
    gt1 = mod_ref[mrow:mrow + 1, 2 * D:3 * D]
    x1 = x_ref[...] + gt1 * y
    x1_ref[...] = x1

    ms = jnp.mean(x1 * x1, axis=-1, keepdims=True)
    sh2 = mod_ref[mrow:mrow + 1, 3 * D:4 * D]
    sc2 = mod_ref[mrow:mrow + 1, 4 * D:5 * D]
    h2 = x1 * lax.rsqrt(ms + EPS) * g2_ref[...] * (1.0 + sc2) + sh2
    for s in range(D // LANES):
        h2_ref[pl.ds(s, tm, stride=D // LANES), :] = h2[:, s * LANES:(s + 1) * LANES]

    hh = h2.astype(BF16)
    hl = (h2 - hh.astype(F32)).astype(BF16)
    w = wrt_ref[...]
    wh = w.astype(BF16)
    wl = (w - wh.astype(F32)).astype(BF16)
    lg = (lax.dot_general(wh, hh, _NT, preferred_element_type=F32)
          + lax.dot_general(wh, hl, _NT, preferred_element_type=F32)
          + lax.dot_general(wl, hh, _NT, preferred_element_type=F32))
    e = jnp.exp(lg - lg.max(axis=0, keepdims=True))
    aff_ref[...] = e / e.sum(axis=0, keepdims=True)


def _merge(x, ya, yb, p_all, mod, mrow, wb, wo, lng, lnb, ws, bst, g2, wrt, tm):
    n = x.shape[0]
    const = lambda shape: pl.BlockSpec(shape, lambda i: (0,) * len(shape))
    return pl.pallas_call(
        functools.partial(_merge_kernel, mrow=mrow, tm=tm),
        grid=(n // tm,),
        in_specs=[pl.BlockSpec((tm, D), lambda i: (i, 0)),
                  pl.BlockSpec((tm, BW), lambda i: (i, 0)),
                  pl.BlockSpec((tm, BW), lambda i: (i, 0)),
                  pl.BlockSpec((tm, BW), lambda i: (i, CB_U)),
                  pl.BlockSpec((tm, BW), lambda i: (i, CB_V)),
                  pl.BlockSpec((tm, D), lambda i: (i, 4)),
                  pl.BlockSpec((tm, D), lambda i: (i, 5)),
                  pl.BlockSpec((tm, D), lambda i: (i, 6)),
                  const((SUBLANES, N_MOD * D)),
                  const((3, BW, D)), const((D, D)),
                  const((1, BW)), const((1, BW)),
                  const((GM_GROUPS, GM_CHUNK, GM_CHUNK)), const((GM_CHUNK, GM_GROUPS)),
                  const((1, D)), const((N_EXPERTS, D))],
        out_specs=[pl.BlockSpec((tm, D), lambda i: (i, 0)),
                   pl.BlockSpec((tm * (D // LANES), LANES), lambda i: (i, 0)),
                   pl.BlockSpec((N_EXPERTS, tm), lambda i: (0, i))],
        out_shape=[jax.ShapeDtypeStruct((n, D), F32),
                   jax.ShapeDtypeStruct((n * (D // LANES), LANES), F32),
                   jax.ShapeDtypeStruct((N_EXPERTS, n), F32)],
        scratch_shapes=[pltpu.VMEM((tm, BW), BF16)],
        compiler_params=_cparams(("arbitrary",)),
        name="merge_prenorm",
    )(x, ya, yb, p_all, p_all, p_all, p_all, p_all, mod, wb, wo, lng, lnb, ws, bst, g2, wrt)


def _route_kernel(all_ref, a_ref, idx_ref, gate_ref, slot_ref, rank_ref, thr_scr, *, cap, nrow):
    e = pl.program_id(0)

    @pl.when(e == 0)
    def _():
        bits = pltpu.bitcast(all_ref[...], I32).reshape(N_EXPERTS, nrow, LANES)

        def bisect(k, prefix):
            cand = prefix | jnp.left_shift(jnp.int32(1), 30 - k)
            hit = jnp.where(bits >= cand, 1.0, 0.0)
            cnt = hit.sum(axis=1, keepdims=True).sum(axis=2, keepdims=True)
            return jnp.where(cnt >= cap, cand, prefix)

        thr_bits = lax.fori_loop(0, 31, bisect, jnp.zeros((N_EXPERTS, 1, 1), I32))
        tb = jnp.broadcast_to(thr_bits, (N_EXPERTS, SUBLANES, LANES)).reshape(N_EXPERTS * SUBLANES, LANES)
        thr_scr[...] = pltpu.bitcast(tb, F32).reshape(N_EXPERTS, SUBLANES, LANES)

    a = a_ref[0]
    thr = thr_scr[e][0:1, :]
    gt = a > thr
    eq = a == thr

    r_i = lax.broadcasted_iota(I32, (LANES, LANES), 0)
    c_i = lax.broadcasted_iota(I32, (LANES, LANES), 1)
    upper = (r_i <= c_i).astype(BF16)
    rr = lax.broadcasted_iota(I32, (nrow, nrow), 0)
    rc = lax.broadcasted_iota(I32, (nrow, nrow), 1)
    lower_strict = (rc < rr).astype(BF16)
    upper_strict = (rr < rc).astype(BF16)

    def prefix(mask):
        xf = jnp.where(mask, 1.0, 0.0)
        incl = jnp.dot(xf.astype(BF16), upper, preferred_element_type=F32)
        tot = jnp.broadcast_to(incl[:, LANES - 1:LANES], (nrow, LANES))
        base = jnp.dot(lower_strict, tot.astype(BF16), preferred_element_type=F32)
        return xf, incl, base + incl - xf

    _, _, eq_rank = prefix(eq)
    need = cap - jnp.sum(jnp.where(gt, 1.0, 0.0))
    sel = gt | (eq & (eq_rank < need))
    xf, incl, rank = prefix(sel)
    rank_ref[0] = rank.astype(I32)
    slot_ref[0] = jnp.where(sel, rank, -1.0).astype(I32)

    ones = jnp.ones((SUBLANES, LANES), BF16)
    tot_l = lax.dot_general(ones, xf.astype(BF16), _NT, preferred_element_type=F32)
    off_l = jnp.dot(tot_l.astype(BF16), upper_strict, preferred_element_type=F32)[0:1]
    tot_l = tot_l[0:1]

    jf = lax.broadcasted_iota(I32, (cap, nrow), 0).astype(F32)
    oh_row = (off_l <= jf) & (jf < off_l + tot_l)
    ohb = jnp.where(oh_row, 1.0, 0.0).astype(BF16)
    row_id = lax.broadcasted_iota(I32, (cap, nrow), 1).astype(F32)
    off_j = jnp.sum(jnp.where(oh_row, off_l, 0.0), axis=1, keepdims=True)
    row_j = jnp.sum(jnp.where(oh_row, row_id, 0.0), axis=1, keepdims=True)
    key = jnp.where(sel, incl, 0.0).astype(BF16)
    g = jnp.dot(ohb, key, preferred_element_type=F32)
    target = lax.broadcasted_iota(I32, (cap, 1), 0).astype(F32) - off_j + 1.0
    oh_lane = g == target
    lane_id = lax.broadcasted_iota(I32, (cap, LANES), 1).astype(F32)
    lane_j = jnp.sum(jnp.where(oh_lane, lane_id, 0.0), axis=1, keepdims=True)
    idx_ref[0] = (row_j * LANES + lane_j).astype(I32)

    a1 = a.astype(BF16)
    r1 = a - a1.astype(F32)
    a2 = r1.astype(BF16)
    a3 = (r1 - a2.astype(F32)).astype(BF16)
    arow = (jnp.dot(ohb, a1, preferred_element_type=F32) + jnp.dot(ohb, a2, preferred_element_type=F32)
            + jnp.dot(ohb, a3, preferred_element_type=F32))
    gate_ref[0] = jnp.sum(jnp.where(oh_lane, arow, 0.0), axis=1, keepdims=True)


def _route(aff3, cap):
    nrow = aff3.shape[1]
    return pl.pallas_call(
        functools.partial(_route_kernel, cap=cap, nrow=nrow),
        grid=(N_EXPERTS,),
        in_specs=[pl.BlockSpec((N_EXPERTS * nrow, LANES), lambda e: (0, 0)),
                  pl.BlockSpec((1, nrow, LANES), lambda e: (e, 0, 0))],
        out_specs=[pl.BlockSpec((1, cap, 1), lambda e: (e, 0, 0)),
                   pl.BlockSpec((1, cap, 1), lambda e: (e, 0, 0)),
                   pl.BlockSpec((1, nrow, LANES), lambda e: (e, 0, 0)),
                   pl.BlockSpec((1, nrow, LANES), lambda e: (e, 0, 0))],
        out_shape=[jax.ShapeDtypeStruct((N_EXPERTS, cap, 1), I32),
                   jax.ShapeDtypeStruct((N_EXPERTS, cap, 1), F32),
                   jax.ShapeDtypeStruct((N_EXPERTS, nrow, LANES), I32),
                   jax.ShapeDtypeStruct((N_EXPERTS, nrow, LANES), I32)],
        scratch_shapes=[pltpu.VMEM((N_EXPERTS, SUBLANES, LANES), F32)],
        compiler_params=_cparams(("arbitrary",)),
        name="ec_route",
    )(aff3.reshape(N_EXPERTS * nrow, LANES), aff3)


def _ffn_kernel(idx_ref, h2_hbm, gate_ref, w1_ref, w3_ref, w2_ref, ye_ref, buf, sem, wb_scr, *, tc, nc):
    sub = D // LANES

    @pl.when(pl.program_id(1) == 0)
    def _():
        wb_scr[0] = w1_ref[0, 0].astype(BF16)
        wb_scr[1] = w3_ref[0, 0].astype(BF16)
        wb_scr[2] = w2_ref[0, 0].astype(BF16)

    step = pl.program_id(0) * nc + pl.program_id(1)
    nsteps = N_EXPERTS * nc

    def row_copy(tok, r, slot):
        return pltpu.make_async_copy(h2_hbm.at[pl.ds(pl.multiple_of(tok * sub, sub), sub), :],
                                     buf.at[slot, pl.ds(pl.multiple_of(r * sub, sub), sub), :],
                                     sem.at[slot])

    def wait_slot(slot):
        pltpu.make_async_copy(h2_hbm.at[pl.ds(0, tc * sub), :], buf.at[slot], sem.at[slot]).wait()

    @pl.when(step == 0)
    def _():
        def body(r, carry):
            row_copy(idx_ref[0, r], r, 0).start()
            return carry

        lax.fori_loop(0, tc, body, 0, unroll=16)

    nxt = jnp.minimum(step + 1, nsteps - 1)
    nxt_e = nxt // nc
    nxt_base = (nxt % nc) * tc
    nslot = (step + 1) % 2
    parts = 4 if tc % 4 == 0 else 1

    def issue_part(q):
        for r in range(q * tc // parts, (q + 1) * tc // parts):
            row_copy(idx_ref[nxt_e, nxt_base + r], r, nslot).start(priority=r % 2)

    slot = step % 2
    wait_slot(slot)
    xin = jnp.concatenate([buf[slot, pl.ds(s, tc, stride=sub), :] for s in range(sub)], axis=1).astype(BF16)
    issue_part(0)
    a = jnp.dot(xin, wb_scr[0], preferred_element_type=F32)
    if parts == 4:
        issue_part(1)
    b = jnp.dot(xin, wb_scr[1], preferred_element_type=F32)
    hid = (a * jax.nn.sigmoid(a) * b).astype(BF16)
    if parts == 4:
        issue_part(2)
    y = jnp.dot(hid, wb_scr[2], preferred_element_type=F32)
    if parts == 4:
        issue_part(3)
    ye_ref[0] = (y * gate_ref[0]).astype(BF16)

    @pl.when(step == nsteps - 1)
    def _():
        wait_slot(nslot)


def _ffn(idx, gate, h2, w1, w3, w2, layer, tc):
    cap = idx.shape[1]
    nc = cap // tc
    sub = D // LANES
    wspec = pl.BlockSpec((1, 1, D, D), lambda e, c, idx: (layer, e, 0, 0))
    grid_spec = pltpu.PrefetchScalarGridSpec(
        num_scalar_prefetch=1,
        grid=(N_EXPERTS, nc),
        in_specs=[pl.BlockSpec(memory_space=pl.ANY),
                  pl.BlockSpec((1, tc, 1), lambda e, c, idx: (e, c, 0)),
                  wspec, wspec, wspec],
        out_specs=pl.BlockSpec((1, tc, D), lambda e, c, idx: (e, c, 0)),
        scratch_shapes=[pltpu.VMEM((2, tc * sub, LANES), F32),
                        pltpu.SemaphoreType.DMA((2,)),
                        pltpu.VMEM((3, D, D), BF16)])
    return pl.pallas_call(
        functools.partial(_ffn_kernel, tc=tc, nc=nc),
        grid_spec=grid_spec,
        out_shape=jax.ShapeDtypeStruct((N_EXPERTS, cap, D), BF16),
        compiler_params=_cparams(("arbitrary", "arbitrary")),
        name="expert_ffn",
    )(idx, h2, gate, w1, w3, w2)


def _combine_kernel(rs_ref, x_ref, slot_ref, mod_ref, gf_ref, ye_hbm, o_ref, win, sem, y_scr,
                    *, win_rows, cap, nt, tt, mrow, final):
    i = pl.program_id(0)

    def start_of(e, t):
        s0 = rs_ref[e, t]
        s_al = (s0 // BF16_ROWS) * BF16_ROWS
        return pl.multiple_of(jnp.minimum(s_al, cap - win_rows), BF16_ROWS)

    def copy(e, t, sl):
        return pltpu.make_async_copy(ye_hbm.at[e, pl.ds(start_of(e, t), win_rows), :],
                                     win.at[sl, e], sem.at[sl])

    @pl.when(i == 0)
    def _():
        for e in range(N_EXPERTS):
            copy(e, i, 0).start()

    @pl.when(i + 1 < nt)
    def _():
        for e in range(N_EXPERTS):
            copy(e, i + 1, (i + 1) % 2).start()

    sl = i % 2
    for e in range(N_EXPERTS):
        copy(e, i, sl).wait()

    def window_sum(r0, r1):
        acc = jnp.zeros((tt, D), F32)
        wi = lax.broadcasted_iota(I32, (r1 - r0, tt), 0) + r0
        for e in range(N_EXPERTS):
            rel = slot_ref[e:e + 1, :] - start_of(e, i)
            oh_t = jnp.where(wi == rel, 1.0, 0.0).astype(BF16)
            acc = acc + lax.dot_general(oh_t, win[sl, e, r0:r1, :], _TN, preferred_element_type=F32)
        return acc

    main_rows = min(win_rows, tt)
    gt2 = mod_ref[mrow:mrow + 1, 5 * D:6 * D]
    y_scr[...] = x_ref[...] + gt2 * window_sum(0, main_rows)
    if win_rows > main_rows:
        need_tail = False
        for e in range(N_EXPERTS):
            end = rs_ref[e, i + 1] - start_of(e, i)
            need_tail = jnp.logical_or(need_tail, end > main_rows)

        @pl.when(need_tail)
        def _():
            y_scr[...] += gt2 * window_sum(main_rows, win_rows)

    y = y_scr[...]
    if final:
        ms = jnp.mean(y * y, axis=-1, keepdims=True)
        y = y * lax.rsqrt(ms + EPS) * gf_ref[...]
    o_ref[...] = y


def _combine(rstart, x1, slot, mod, mrow, g_final, ye, tt, final):
    n = x1.shape[0]
    cap = ye.shape[1]
    nt = n // tt
    win_rows = min(cap, tt + BF16_ROWS)
    grid_spec = pltpu.PrefetchScalarGridSpec(
        num_scalar_prefetch=1,
        grid=(nt,),
        in_specs=[pl.BlockSpec((tt, D), lambda i, rs: (i, 0)),
                  pl.BlockSpec((N_EXPERTS, tt), lambda i, rs: (0, i)),
                  pl.BlockSpec((SUBLANES, N_MOD * D), lambda i, rs: (0, 0)),
                  pl.BlockSpec((1, D), lambda i, rs: (0, 0)),
                  pl.BlockSpec(memory_space=pl.ANY)],
        out_specs=pl.BlockSpec((tt, D), lambda i, rs: (i, 0)),
        scratch_shapes=[pltpu.VMEM((2, N_EXPERTS, win_rows, D), BF16),
                        pltpu.SemaphoreType.DMA((2,)),
                        pltpu.VMEM((tt, D), F32)])
    return pl.pallas_call(
        functools.partial(_combine_kernel, win_rows=win_rows, cap=cap, nt=nt, tt=tt, mrow=mrow, final=final),
        grid_spec=grid_spec,
        out_shape=jax.ShapeDtypeStruct((n, D), F32),
        compiler_params=_cparams(("arbitrary",)),
        name="moe_combine",
    )(rstart, x1, slot, mod, g_final, ye)


def _rope_tables(n_lat):
    rows = n_lat // GRID_W
    inv = ROPE_THETA ** (-jnp.arange(0, ROPE_AXIS, 2, dtype=F32) / ROPE_AXIS)
    ang_r = jnp.arange(rows).astype(F32)[:, None] * inv
    ang_c = jnp.arange(GRID_W).astype(F32)[:, None] * inv
    cr, sr, cc, sc = jnp.cos(ang_r), jnp.sin(ang_r), jnp.cos(ang_c), jnp.sin(ang_c)
    zr, zc = jnp.zeros_like(sr), jnp.zeros_like(sc)
    rep = lambda parts: jnp.concatenate(parts * (LANES // DA_DH), axis=1)

    def table(row_parts, col_parts):
        r = rep([row_parts[0], row_parts[1], zr, zr])
        c = rep([zc, zc, col_parts[0], col_parts[1]])
        return (r[:, None, :] + c[None, :, :]).reshape(n_lat, LANES)

    return (table((cr, cr), (cc, cc)), table((zr, sr), (zc, sc)), table((-sr, zr), (-sc, zc)))


def _largest_tile(n, unit, limit):
    best = unit
    for k in range(1, n // unit + 1):
        if n % (k * unit) == 0 and k * unit <= limit:
            best = k * unit
    return best


def _moe(x1, h2, aff_t, mod, mrow, w1, w3, w2, layer, g_final, final):
    n = x1.shape[0]
    cap = max(1, EC_CAPACITY * n // N_EXPERTS)
    tile = LANES * LANES if n <= LANES * LANES else n
    n_pad = max(n, tile)
    if n_pad > n:
        aff_t = jnp.concatenate([aff_t, jnp.full((N_EXPERTS, n_pad - n), -1.0, F32)], axis=1)
    idx, gate, slot, rank = _route(aff_t.reshape(N_EXPERTS, n_pad // LANES, LANES), cap)
    tt = min(256, n)
    slot = slot.reshape(N_EXPERTS, n_pad)[:, :n]
    rstart = jnp.concatenate([rank.reshape(N_EXPERTS, n_pad)[:, 0:n:tt],
                              jnp.full((N_EXPERTS, 1), cap, I32)], axis=1)
    tc = min(cap, 512)
    ye = _ffn(idx.reshape(N_EXPERTS, cap), gate, h2, w1, w3, w2, layer, tc)
    return _combine(rstart, x1, slot, mod, mrow, g_final, ye, tt, final)


def kernel(x, c, ctx, c_ctx, w_ada, b_ada, g_norm1, g_norm2, w_in, na_rpb, da_lam_q1, da_lam_k1, da_lam_q2,
           da_lam_k2, da_subln_g, gm_ln_g, gm_ln_b, gm_w_s, gm_b_s, w_branch, w_out, w_router, w_e1, w_e3,
           w_e2, g_final):
    depth = w_ada.shape[0]
    n_lat, n_ctx = x.shape[1], ctx.shape[1]
    rows = n_lat // GRID_W
    xs, xc = x[0], ctx[0]

    cc = jnp.concatenate([c.reshape(1, D), c_ctx.reshape(1, D), jnp.zeros((SUBLANES - 2, D), F32)], axis=0)
    mods = _ada(cc, w_ada, b_ada)

    tabs_lat = _rope_tables(n_lat)
    tabs_ctx = (jnp.ones((n_ctx, LANES), F32), jnp.zeros((n_ctx, LANES), F32), jnp.zeros((n_ctx, LANES), F32))
    tm_proj = _largest_tile(n_lat, 256, 1024)
    tk = _largest_tile(n_lat, 256, 2048)
    tq = 512

    for i in range(depth):
        last = i == depth - 1
        lam_init = 0.8 - 0.6 * math.exp(-0.3 * i)
        mod = mods[i]
        w_in_b = w_in[i].astype(BF16)
        g1 = g_norm1[i].reshape(1, D)
        g2 = g_norm2[i].reshape(1, D)
        gf = g_final.reshape(1, D)
        lam_vecs = [v[i].reshape(1, DA_DH).astype(F32) for v in (da_lam_q1, da_lam_k1, da_lam_q2, da_lam_k2)]
        g_col = da_subln_g[i].reshape(DA_VD, 1)
        merge_w = (w_branch[i].astype(BF16), w_out[i].astype(BF16), gm_ln_g[i].reshape(1, BW),
                   gm_ln_b[i].reshape(1, BW), gm_w_s[i].astype(BF16), gm_b_s[i].T, g2, w_router[i].T)

        p_lat, qt_lat, vt_lat = _proj(xs, g1, mod, w_in_b, tabs_lat, 0, tm_proj)
        p_ctx, qt_ctx, vt_ctx = _proj(xc, g1, mod, w_in_b, tabs_ctx, 1, n_ctx)

        ya = _na(p_lat, p_ctx, _na_bias(na_rpb[i], rows))
        yb = _diff(qt_lat, p_ctx, vt_ctx, (p_lat, vt_lat), lam_vecs, g_col, lam_init, tq, tk)
        x1, h2, aff_t = _merge(xs, ya, yb, p_lat, mod, 0, *merge_w, tm=_largest_tile(n_lat, 256, 512))
        xs = _moe(x1, h2, aff_t, mod, 0, w_e1, w_e3, w_e2, i, gf, last)

        if not last:
            yac = _ctx_dense(p_ctx)
            ybc = _diff(qt_ctx, p_ctx, vt_ctx, None, lam_vecs, g_col, lam_init, n_ctx, n_ctx)
            x1c, h2c, aff_tc = _merge(xc, yac, ybc, p_ctx, mod, 1, *merge_w, tm=n_ctx)
            xc = _moe(x1c, h2c, aff_tc, mod, 1, w_e1, w_e3, w_e2, i, gf, False)
    return xs[None]
```

```python
import functools
import math

import numpy as np
import jax
import jax.numpy as jnp
from jax import lax
from jax.experimental import pallas as pl
from jax.experimental.pallas import tpu as pltpu

F32 = jnp.float32
BF16 = jnp.bfloat16
I32 = jnp.int32

D = 1024
GRID_W = 64
BW = 512
N_COLBLK = 14
NA_HEADS = 8
NA_WIN_R = 8
NA_WIN_C = 16
NA_ROWS = 4
NA_KROWS = 12
DA_HEADS = 4
DA_DH = 64
DA_VD = 128
GM_GROUPS = 4
GM_CHUNK = 128
N_EXPERTS = 16
EC_CAPACITY = 2
ROPE_THETA = 10000.0
ROPE_AXIS = DA_DH // 2
N_MOD = 6
EPS = 1e-6
LANES = 128
SUBLANES = 8
BF16_ROWS = 16
NEG_BIG = -1e30
VT_ROWS = BW + BF16_ROWS
Q_SCALE_LOG2 = DA_DH ** -0.5 * math.log2(math.e)
VMEM_LIMIT = 60000 * 1024
STALE_MARGIN = 64.0

CB_KA, CB_VA, CB_KB, CB_VB, CB_QA, CB_QB, CB_U, CB_V = range(8)

_NT = (((1,), (1,)), ((), ()))
_TN = (((0,), (0,)), ((), ()))


def _cparams(sem):
    return pltpu.CompilerParams(dimension_semantics=sem, vmem_limit_bytes=VMEM_LIMIT)


def _gelu(x):
    return 0.5 * x * (1.0 + jnp.tanh(math.sqrt(2.0 / math.pi) * (x + 0.044715 * (x * x * x))))


def _ada_kernel(c_ref, w_ref, b_ref, o_ref):
    c = c_ref[...]
    s = c * jax.nn.sigmoid(c)
    o_ref[0] = jnp.dot(s, w_ref[0], preferred_element_type=F32,
                       precision=lax.Precision.HIGHEST) + b_ref[0]


def _ada(cc, w_ada, b_ada):
    depth = w_ada.shape[0]
    tn = D
    return pl.pallas_call(
        _ada_kernel,
        grid=(depth, N_MOD * D // tn),
        in_specs=[pl.BlockSpec((SUBLANES, D), lambda i, j: (0, 0)),
                  pl.BlockSpec((1, D, tn), lambda i, j: (i, 0, j)),
                  pl.BlockSpec((1, 1, tn), lambda i, j: (i, 0, j))],
        out_specs=pl.BlockSpec((1, SUBLANES, tn), lambda i, j: (i, 0, j)),
        out_shape=jax.ShapeDtypeStruct((depth, SUBLANES, N_MOD * D), F32),
        compiler_params=_cparams(("arbitrary", "arbitrary")),
        name="ada_mod",
    )(cc, w_ada, b_ada.reshape(depth, 1, N_MOD * D))


def _proj_kernel(x_ref, g_ref, mod_ref, w_ref, c_ref, sp_ref, sm_ref, p_ref, qt_ref, vt_ref, h_scr, *, mrow):
    j = pl.program_id(1)

    @pl.when(j == 0)
    def _():
        x = x_ref[...]
        ms = jnp.mean(x * x, axis=-1, keepdims=True)
        xn = x * lax.rsqrt(ms + EPS) * g_ref[...]
        sh = mod_ref[mrow:mrow + 1, 0:D]
        sc = mod_ref[mrow:mrow + 1, D:2 * D]
        h_scr[...] = (xn * (1.0 + sc) + sh).astype(BF16)

    z = jnp.dot(h_scr[...], w_ref[...], preferred_element_type=F32)

    def rope(z):
        reps = BW // LANES
        c = jnp.concatenate([c_ref[...]] * reps, axis=1)
        sp = jnp.concatenate([sp_ref[...]] * reps, axis=1)
        sm = jnp.concatenate([sm_ref[...]] * reps, axis=1)
        half = ROPE_AXIS // 2
        return z * c + pltpu.roll(z, half, 1) * sp + pltpu.roll(z, BW - half, 1) * sm

    @pl.when(j == CB_KB // 2)
    def _():
        p_ref[:, 0:BW] = rope(z[:, 0:BW]).astype(BF16)
        zv = z[:, BW:2 * BW]
        p_ref[:, BW:2 * BW] = zv.astype(BF16)
        vt_ref[0:BW, :] = zv.T.astype(BF16)
        vt_ref[BW:VT_ROWS, :] = jnp.ones((VT_ROWS - BW, z.shape[0]), BF16)

    @pl.when(j == CB_QB // 2)
    def _():
        p_ref[:, 0:BW] = z[:, 0:BW].astype(BF16)
        zr = rope(z[:, BW:2 * BW])
        p_ref[:, BW:2 * BW] = zr.astype(BF16)
        qt_ref[...] = (zr * Q_SCALE_LOG2).T.astype(BF16)

    @pl.when((j != CB_KB // 2) & (j != CB_QB // 2))
    def _():
        p_ref[...] = z.astype(BF16)


def _proj(x, g, mod, w, tabs, mrow, tm):
    n = x.shape[0]
    return pl.pallas_call(
        functools.partial(_proj_kernel, mrow=mrow),
        grid=(n // tm, N_COLBLK // 2),
        in_specs=[pl.BlockSpec((tm, D), lambda i, j: (i, 0)),
                  pl.BlockSpec((1, D), lambda i, j: (0, 0)),
                  pl.BlockSpec((SUBLANES, N_MOD * D), lambda i, j: (0, 0)),
                  pl.BlockSpec((D, 2 * BW), lambda i, j: (0, j)),
                  pl.BlockSpec((tm, LANES), lambda i, j: (i, 0)),
                  pl.BlockSpec((tm, LANES), lambda i, j: (i, 0)),
                  pl.BlockSpec((tm, LANES), lambda i, j: (i, 0))],
        out_specs=[pl.BlockSpec((tm, 2 * BW), lambda i, j: (i, j)),
                   pl.BlockSpec((BW, tm), lambda i, j: (0, i)),
                   pl.BlockSpec((VT_ROWS, tm), lambda i, j: (0, i))],
        out_shape=[jax.ShapeDtypeStruct((n, N_COLBLK * BW), BF16),
                   jax.ShapeDtypeStruct((BW, n), BF16),
                   jax.ShapeDtypeStruct((VT_ROWS, n), BF16)],
        scratch_shapes=[pltpu.VMEM((tm, D), BF16)],
        compiler_params=_cparams(("arbitrary", "arbitrary")),
        name="proj",
    )(x, g, mod, w, *tabs)


def _na_kernel(q_ref, k0_ref, k1_ref, k2_ref, v0_ref, v1_ref, v2_ref, kc_ref, vc_ref, bias_ref, o_ref):
    nq = q_ref.shape[0]
    lane = lax.broadcasted_iota(I32, (nq, LANES), 1)
    k_refs = (kc_ref, k0_ref, k1_ref, k2_ref)
    v_refs = (vc_ref, v0_ref, v1_ref, v2_ref)
    for g in range(NA_HEADS // 2):
        sl = slice(g * LANES, (g + 1) * LANES)
        qp = q_ref[:, sl] * 0.125
        ks = [r[:, sl] for r in k_refs]
        vs = [r[:, sl] for r in v_refs]
        outs = []
        for sub in range(2):
            h = 2 * g + sub
            keep = (lane < 64) if sub == 0 else (lane >= 64)
            qz = jnp.where(keep, qp, jnp.zeros_like(qp))
            ss = [lax.dot_general(qz, ks[0], _NT, preferred_element_type=F32)]
            for t in range(3):
                s = lax.dot_general(qz, ks[1 + t], _NT, preferred_element_type=F32)
                ss.append(s + bias_ref[0, h, :, t * nq:(t + 1) * nq])
            m = ss[0].max(axis=-1, keepdims=True)
            for s in ss[1:]:
                m = jnp.maximum(m, s.max(axis=-1, keepdims=True))
            l = jnp.zeros_like(m)
            o = jnp.zeros((nq, LANES), F32)
            for s, v in zip(ss, vs):
                p = jnp.exp(s - m)
                l = l + p.sum(axis=-1, keepdims=True)
                o = o + jnp.dot(p.astype(BF16), v, preferred_element_type=F32)
            outs.append(o / l)
        o_ref[:, sl] = jnp.where(lane < 64, outs[0], outs[1]).astype(BF16)


def _na_bias(rpb, rows):
    nb = rows // NA_ROWS
    c = np.arange(GRID_W)[:, None]
    kc = np.arange(GRID_W)[None, :]
    c0 = np.clip(c - NA_WIN_C // 2, 0, GRID_W - NA_WIN_C)
    col_ok = (kc >= c0) & (kc < c0 + NA_WIN_C)
    dc = kc - c + (NA_WIN_C - 1)
    sel = np.stack([(dc == d) & col_ok for d in range(2 * NA_WIN_C - 1)]).astype(np.float32)
    toep = jnp.einsum("hrd,dck->hrck", rpb.astype(F32), sel, precision=lax.Precision.HIGHEST)
    toep = toep + np.where(col_ok, 0.0, NEG_BIG).astype(np.float32)
    n_dr = 2 * NA_WIN_R - 1
    toep = jnp.concatenate([toep, jnp.full((NA_HEADS, 1, GRID_W, GRID_W), NEG_BIG, F32)], axis=1)
    which = np.full((3, NA_ROWS, NA_KROWS), n_dr, np.int32)
    for v, b in enumerate((0, 1, nb - 1)):
        kb0 = min(max(b - 1, 0), nb - 3)
        for a in range(NA_ROWS):
            r = NA_ROWS * b + a
            r0 = min(max(r - NA_WIN_R // 2, 0), rows - NA_WIN_R)
            for i in range(NA_KROWS):
                kr = NA_ROWS * kb0 + i
                if r0 <= kr < r0 + NA_WIN_R:
                    which[v, a, i] = kr - r + NA_WIN_R - 1
    blocks = jnp.take(toep, which.reshape(-1), axis=1)
    blocks = blocks.reshape(NA_HEADS, 3, NA_ROWS, NA_KROWS, GRID_W, GRID_W)
    return blocks.transpose(1, 0, 2, 4, 3, 5).reshape(3, NA_HEADS, NA_ROWS * GRID_W, NA_KROWS * GRID_W)


def _na(p_lat, p_ctx, bias):
    nq = NA_ROWS * GRID_W
    n_lat, n_ctx = p_lat.shape[0], p_ctx.shape[0]
    nb = n_lat // nq

    def kmap(t, col):
        return lambda b: (jnp.clip(b - 1, 0, nb - 3) + t, col)

    def bmap(b):
        return (jnp.where(b == 0, 0, jnp.where(b == nb - 1, 2, 1)), 0, 0, 0)

    blk = lambda f: pl.BlockSpec((nq, BW), f)
    return pl.pallas_call(
        _na_kernel,
        grid=(nb,),
        in_specs=[blk(lambda b: (b, CB_QA)),
                  blk(kmap(0, CB_KA)), blk(kmap(1, CB_KA)), blk(kmap(2, CB_KA)),
                  blk(kmap(0, CB_VA)), blk(kmap(1, CB_VA)), blk(kmap(2, CB_VA)),
                  pl.BlockSpec((n_ctx, BW), lambda b: (0, CB_KA)),
                  pl.BlockSpec((n_ctx, BW), lambda b: (0, CB_VA)),
                  pl.BlockSpec((1, NA_HEADS, nq, NA_KROWS * GRID_W), bmap)],
        out_specs=pl.BlockSpec((nq, BW), lambda b: (b, 0)),
        out_shape=jax.ShapeDtypeStruct((n_lat, BW), BF16),
        compiler_params=_cparams(("arbitrary",)),
        name="na_attn",
    )(p_lat, p_lat, p_lat, p_lat, p_lat, p_lat, p_lat, p_ctx, p_ctx, bias)


def _ctx_dense_kernel(q_ref, k_ref, v_ref, o_ref):
    nq = q_ref.shape[0]
    lane = lax.broadcasted_iota(I32, (nq, LANES), 1)
    for g in range(NA_HEADS // 2):
        sl = slice(g * LANES, (g + 1) * LANES)
        qp = q_ref[:, sl] * 0.125
        kp = k_ref[:, sl]
        vp = v_ref[:, sl]
        outs = []
        for sub in range(2):
            keep = (lane < 64) if sub == 0 else (lane >= 64)
            qz = jnp.where(keep, qp, jnp.zeros_like(qp))
            s = lax.dot_general(qz, kp, _NT, preferred_element_type=F32)
            m = s.max(axis=-1, keepdims=True)
            p = jnp.exp(s - m)
            l = p.sum(axis=-1, keepdims=True)
            outs.append(jnp.dot(p.astype(BF16), vp, preferred_element_type=F32) / l)
        o_ref[:, sl] = jnp.where(lane < 64, outs[0], outs[1]).astype(BF16)


def _ctx_dense(p_ctx):
    n_ctx = p_ctx.shape[0]
    blk = lambda col: pl.BlockSpec((n_ctx, BW), lambda i: (0, col))
    return pl.pallas_call(
        _ctx_dense_kernel,
        grid=(1,),
        in_specs=[blk(CB_QA), blk(CB_KA), blk(CB_VA)],
        out_specs=pl.BlockSpec((n_ctx, BW), lambda i: (0, 0)),
        out_shape=jax.ShapeDtypeStruct((n_ctx, BW), BF16),
        compiler_params=_cparams(("arbitrary",)),
        name="ctx_dense_attn",
    )(p_ctx, p_ctx, p_ctx)


def _diff_kernel(*refs, tq, nk, lam_init, with_lat):
    if with_lat:
        (qt_ref, kc_ref, vtc_ref, k_ref, vt_ref, lq1_ref, lk1_ref, lq2_ref, lk2_ref, g_ref, o_ref,
         qz_scr, m_scr, acc_scr, bak_scr, s0_scr, s1_scr, p0_scr, p1_scr) = refs
    else:
        (qt_ref, kc_ref, vtc_ref, lq1_ref, lk1_ref, lq2_ref, lk2_ref, g_ref, o_ref,
         qz_scr, m_scr, acc_scr, bak_scr, s0_scr, s1_scr, p0_scr, p1_scr) = refs
    s_scr = (s0_scr, s1_scr)
    p_scr = (p0_scr, p1_scr)
    j = pl.program_id(1)
    lanes2 = 2 * tq
    rc = max(BF16_ROWS, BF16_ROWS * 1024 // lanes2)
    ck = 256

    def attend_all(k_ref, vt_ref):
        ns = k_ref.shape[0] // ck

        def head_cols(h):
            return slice(h * DA_VD, (h + 1) * DA_VD)

        def run_round(h_qk, h_exp, h_pv, m_b):
            mx = None
            for i in range(ns):
                rows = slice(i * ck, (i + 1) * ck)
                if h_qk is not None:
                    s = jnp.dot(k_ref[rows, head_cols(h_qk)], qz_scr[h_qk], preferred_element_type=F32)
                    s_scr[h_qk % 2][rows, :] = s
                    part = s.reshape(ck // rc, rc, lanes2).max(axis=0)
                    mx = part if mx is None else jnp.maximum(mx, part)
                if h_exp is not None:
                    for c in range(i * ck // rc, (i + 1) * ck // rc):
                        rr = slice(c * rc, (c + 1) * rc)
                        p_scr[h_exp % 2][rr, :] = jnp.exp2(s_scr[h_exp % 2][rr, :] - m_b).astype(BF16)
                if h_pv is not None:
                    vt_ext = jnp.concatenate([vt_ref[head_cols(h_pv), rows], vt_ref[BW:VT_ROWS, rows]], axis=0)
                    acc_scr[h_pv] += jnp.dot(vt_ext, p_scr[h_pv % 2][rows, :], preferred_element_type=F32)
            return mx

        mx = None
        for r in range(DA_HEADS + 2):
            h_qk = r if r < DA_HEADS else None
            h_exp = r - 1 if 0 <= r - 1 < DA_HEADS else None
            h_pv = r - 2 if 0 <= r - 2 < DA_HEADS else None
            m_b = None
            if h_exp is not None:
                m_old = m_scr[h_exp]
                m_new = jnp.maximum(m_old, mx.max(axis=0, keepdims=True))
                m_scr[h_exp] = m_new
                acc_scr[h_exp] = jnp.exp2(m_old - m_new) * acc_scr[h_exp]
                m_b = jnp.broadcast_to(m_new, (rc, lanes2))
            mx = run_round(h_qk, h_exp, h_pv, m_b)

    @pl.when(j == 0)
    def _():
        row = lax.broadcasted_iota(I32, (DA_VD, tq), 0)
        for h in range(DA_HEADS):
            qh = qt_ref[h * DA_VD:(h + 1) * DA_VD, :]
            zero = jnp.zeros_like(qh)
            qz_scr[h] = jnp.concatenate([jnp.where(row < DA_DH, qh, zero),
                                         jnp.where(row >= DA_DH, qh, zero)], axis=1)
        m_scr[...] = jnp.full(m_scr.shape, -jnp.inf, F32)
        acc_scr[...] = jnp.zeros(acc_scr.shape, F32)
        attend_all(kc_ref, vtc_ref)

    def attend_stale(k_ref, vt_ref):
        ns = k_ref.shape[0] // ck
        maxima = []
        for r in range(DA_HEADS + 1):
            h_qk = r if r < DA_HEADS else None
            h_pv = r - 1 if r >= 1 else None
            if h_qk is not None:
                m_b = jnp.broadcast_to(m_scr[h_qk], (rc, lanes2))
            mx = None
            for i in range(ns):
                rows = slice(i * ck, (i + 1) * ck)
                if h_qk is not None:
                    cols = slice(h_qk * DA_VD, (h_qk + 1) * DA_VD)
                    s = jnp.dot(k_ref[rows, cols], qz_scr[h_qk], preferred_element_type=F32)
                    part = s.reshape(ck // rc, rc, lanes2).max(axis=0)
                    mx = part if mx is None else jnp.maximum(mx, part)
                    for c in range(ck // rc):
                        rr = slice(i * ck + c * rc, i * ck + (c + 1) * rc)
                        p_scr[h_qk % 2][rr, :] = jnp.exp2(s[c * rc:(c + 1) * rc, :] - m_b).astype(BF16)
                if h_pv is not None:
                    cols = slice(h_pv * DA_VD, (h_pv + 1) * DA_VD)
                    vt_ext = jnp.concatenate([vt_ref[cols, rows], vt_ref[BW:VT_ROWS, rows]], axis=0)
                    acc_scr[h_pv] += jnp.dot(vt_ext, p_scr[h_pv % 2][rows, :], preferred_element_type=F32)
            if h_qk is not None:
                maxima.append(mx.max(axis=0, keepdims=True))
        return maxima

    if with_lat:
        bak_scr[...] = acc_scr[...]
        maxima = attend_stale(k_ref, vt_ref)
        excess = maxima[0] - m_scr[0]
        for h in range(1, DA_HEADS):
            excess = jnp.maximum(excess, maxima[h] - m_scr[h])
        safe = jnp.max(excess) <= STALE_MARGIN

        @pl.when(safe)
        def _():
            for h in range(DA_HEADS):
                m_old = m_scr[h]
                m_new = jnp.maximum(m_old, maxima[h])
                m_scr[h] = m_new
                acc_scr[h] = jnp.exp2(m_old - m_new) * acc_scr[h]

        @pl.when(jnp.logical_not(safe))
        def _():
            acc_scr[...] = bak_scr[...]
            attend_all(k_ref, vt_ref)

    @pl.when(j == nk - 1)
    def _():
        lam = (jnp.exp(jnp.sum(lq1_ref[...] * lk1_ref[...], keepdims=True))
               - jnp.exp(jnp.sum(lq2_ref[...] * lk2_ref[...], keepdims=True)) + lam_init)
        for h in range(DA_HEADS):
            o = acc_scr[h, 0:DA_VD, :] / acc_scr[h, DA_VD:DA_VD + 1, :]
            od = o[:, :tq] - lam * o[:, tq:]
            ms = jnp.mean(od * od, axis=0, keepdims=True)
            y = od * lax.rsqrt(ms + EPS) * g_ref[...] * (1.0 - lam_init)
            o_ref[:, h * DA_VD:(h + 1) * DA_VD] = y.T.astype(BF16)


def _diff(qt, p_ctx, vt_ctx, lat, lam_vecs, g_col, lam_init, tq, tk):
    n_q, n_ctx = qt.shape[1], p_ctx.shape[0]
    with_lat = lat is not None
    nk = lat[0].shape[0] // tk if with_lat else 1
    vec = pl.BlockSpec((1, DA_DH), lambda i, j: (0, 0))
    in_specs = [pl.BlockSpec((BW, tq), lambda i, j: (0, i)),
                pl.BlockSpec((n_ctx, BW), lambda i, j: (0, CB_KB)),
                pl.BlockSpec((VT_ROWS, n_ctx), lambda i, j: (0, 0))]
    args = [qt, p_ctx, vt_ctx]
    if with_lat:
        in_specs += [pl.BlockSpec((tk, BW), lambda i, j: (j, CB_KB)),
                     pl.BlockSpec((VT_ROWS, tk), lambda i, j: (0, j))]
        args += list(lat)
    in_specs += [vec, vec, vec, vec, pl.BlockSpec((DA_VD, 1), lambda i, j: (0, 0))]
    return pl.pallas_call(
        functools.partial(_diff_kernel, tq=tq, nk=nk, lam_init=lam_init, with_lat=with_lat),
        grid=(n_q // tq, nk),
        in_specs=in_specs,
        out_specs=pl.BlockSpec((tq, BW), lambda i, j: (i, 0)),
        out_shape=jax.ShapeDtypeStruct((n_q, BW), BF16),
        scratch_shapes=[pltpu.VMEM((DA_HEADS, DA_VD, 2 * tq), BF16),
                        pltpu.VMEM((DA_HEADS, 1, 2 * tq), F32),
                        pltpu.VMEM((DA_HEADS, DA_VD + BF16_ROWS, 2 * tq), F32),
                        pltpu.VMEM((DA_HEADS, DA_VD + BF16_ROWS, 2 * tq), F32)]
        + [pltpu.VMEM((max(tk, n_ctx), 2 * tq), F32)] * 2
        + [pltpu.VMEM((max(tk, n_ctx), 2 * tq), BF16)] * 2,
        compiler_params=_cparams(("arbitrary", "arbitrary")),
        name="diff_attn",
    )(*args, *lam_vecs, g_col)


def _merge_kernel(x_ref, ya_ref, yb_ref, u_ref, v_ref, ga_ref, gb_ref, gc_ref, mod_ref, wb_ref, wo_ref,
                  lng_ref, lnb_ref, ws_ref, bst_ref, g2_ref, wrt_ref,
                  x1_ref, h2_ref, aff_ref, yc_scr, *, mrow, tm):
    ug = _gelu(u_ref[...].astype(F32))
    vg = _gelu(v_ref[...].astype(F32))
    mu = jnp.mean(vg, axis=-1, keepdims=True)
    var = jnp.mean(jnp.square(vg - mu), axis=-1, keepdims=True)
    vn = ((vg - mu) * lax.rsqrt(var + EPS) * lng_ref[...] + lnb_ref[...]).astype(BF16)
    for ci in range(tm // GM_CHUNK):
        rs = slice(ci * GM_CHUNK, (ci + 1) * GM_CHUNK)
        for g in range(GM_GROUPS):
            cs = slice(g * LANES, (g + 1) * LANES)
            mixed = jnp.dot(ws_ref[g], vn[rs, cs], preferred_element_type=F32) + bst_ref[:, g:g + 1]
            yc_scr[rs, cs] = (ug[rs, cs] * mixed).astype(BF16)

    sig = jax.nn.sigmoid
    m = sig(ga_ref[...].astype(F32)) * jnp.dot(ya_ref[...], wb_ref[0], preferred_element_type=F32)
    m = m + sig(gb_ref[...].astype(F32)) * jnp.dot(yb_ref[...], wb_ref[1], preferred_element_type=F32)
    m = m + sig(gc_ref[...].astype(F32)) * jnp.dot(yc_scr[...], wb_ref[2], preferred_element_type=F32)
    y = jnp.dot(m.astype(BF16), wo_ref[...], preferred_element_type=F32)
    gt1 = mod_ref[mrow:mrow + 1, 2 * D:3 * D]
    x1 = x_ref[...] + gt1 * y
    x1_ref[...] = x1

    ms = jnp.mean(x1 * x1, axis=-1, keepdims=True)
    sh2 = mod_ref[mrow:mrow + 1, 3 * D:4 * D]
    sc2 = mod_ref[mrow:mrow + 1, 4 * D:5 * D]
    h2 = x1 * lax.rsqrt(ms + EPS) * g2_ref[...] * (1.0 + sc2) + sh2
    for s in range(D // LANES):
        h2_ref[pl.ds(s, tm, stride=D // LANES), :] = h2[:, s * LANES:(s + 1) * LANES]

    hh = h2.astype(BF16)
    hl = (h2 - hh.astype(F32)).astype(BF16)
    w = wrt_ref[...]
    wh = w.astype(BF16)
    wl = (w - wh.astype(F32)).astype(BF16)
    lg = (lax.dot_general(wh, hh, _NT, preferred_element_type=F32)
          + lax.dot_general(wh, hl, _NT, preferred_element_type=F32)
          + lax.dot_general(wl, hh, _NT, preferred_element_type=F32))
    e = jnp.exp(lg - lg.max(axis=0, keepdims=True))
    aff_ref[...] = e / e.sum(axis=0, keepdims=True)


def _merge(x, ya, yb, p_all, mod, mrow, wb, wo, lng, lnb, ws, bst, g2, wrt, tm):
    n = x.shape[0]
    const = lambda shape: pl.BlockSpec(shape, lambda i: (0,) * len(shape))
    return pl.pallas_call(
        functools.partial(_merge_kernel, mrow=mrow, tm=tm),
        grid=(n // tm,),
        in_specs=[pl.BlockSpec((tm, D), lambda i: (i, 0)),
                  pl.BlockSpec((tm, BW), lambda i: (i, 0)),
                  pl.BlockSpec((tm, BW), lambda i: (i, 0)),
                  pl.BlockSpec((tm, BW), lambda i: (i, CB_U)),
                  pl.BlockSpec((tm, BW), lambda i: (i, CB_V)),
                  pl.BlockSpec((tm, D), lambda i: (i, 4)),
                  pl.BlockSpec((tm, D), lambda i: (i, 5)),
                  pl.BlockSpec((tm, D), lambda i: (i, 6)),
                  const((SUBLANES, N_MOD * D)),
                  const((3, BW, D)), const((D, D)),
                  const((1, BW)), const((1, BW)),
                  const((GM_GROUPS, GM_CHUNK, GM_CHUNK)), const((GM_CHUNK, GM_GROUPS)),
                  const((1, D)), const((N_EXPERTS, D))],
        out_specs=[pl.BlockSpec((tm, D), lambda i: (i, 0)),
                   pl.BlockSpec((tm * (D // LANES), LANES), lambda i: (i, 0)),
                   pl.BlockSpec((N_EXPERTS, tm), lambda i: (0, i))],
        out_shape=[jax.ShapeDtypeStruct((n, D), F32),
                   jax.ShapeDtypeStruct((n * (D // LANES), LANES), F32),
                   jax.ShapeDtypeStruct((N_EXPERTS, n), F32)],
        scratch_shapes=[pltpu.VMEM((tm, BW), BF16)],
        compiler_params=_cparams(("arbitrary",)),
        name="merge_prenorm",
    )(x, ya, yb, p_all, p_all, p_all, p_all, p_all, mod, wb, wo, lng, lnb, ws, bst, g2, wrt)


def _route_kernel(all_ref, a_ref, idx_ref, gate_ref, slot_ref, rank_ref, thr_scr, *, cap, nrow):
    e = pl.program_id(0)

    @pl.when(e == 0)
    def _():
        bits = pltpu.bitcast(all_ref[...], I32).reshape(N_EXPERTS, nrow, LANES)

        def bisect(k, prefix):
            cand = prefix | jnp.left_shift(jnp.int32(1), 30 - k)
            hit = jnp.where(bits >= cand, 1.0, 0.0)
            cnt = hit.sum(axis=1, keepdims=True).sum(axis=2, keepdims=True)
            return jnp.where(cnt >= cap, cand, prefix)

        thr_bits = lax.fori_loop(0, 31, bisect, jnp.zeros((N_EXPERTS, 1, 1), I32))
        tb = jnp.broadcast_to(thr_bits, (N_EXPERTS, SUBLANES, LANES)).reshape(N_EXPERTS * SUBLANES, LANES)
        thr_scr[...] = pltpu.bitcast(tb, F32).reshape(N_EXPERTS, SUBLANES, LANES)

    a = a_ref[0]
    thr = thr_scr[e][0:1, :]
    gt = a > thr
    eq = a == thr

    r_i = lax.broadcasted_iota(I32, (LANES, LANES), 0)
    c_i = lax.broadcasted_iota(I32, (LANES, LANES), 1)
    upper = (r_i <= c_i).astype(BF16)
    rr = lax.broadcasted_iota(I32, (nrow, nrow), 0)
    rc = lax.broadcasted_iota(I32, (nrow, nrow), 1)
    lower_strict = (rc < rr).astype(BF16)
    upper_strict = (rr < rc).astype(BF16)

    def prefix(mask):
        xf = jnp.where(mask, 1.0, 0.0)
        incl = jnp.dot(xf.astype(BF16), upper, preferred_element_type=F32)
        tot = jnp.broadcast_to(incl[:, LANES - 1:LANES], (nrow, LANES))
        base = jnp.dot(lower_strict, tot.astype(BF16), preferred_element_type=F32)
        return xf, incl, base + incl - xf

    _, _, eq_rank = prefix(eq)
    need = cap - jnp.sum(jnp.where(gt, 1.0, 0.0))
    sel = gt | (eq & (eq_rank < need))
    xf, incl, rank = prefix(sel)
    rank_ref[0] = rank.astype(I32)
    slot_ref[0] = jnp.where(sel, rank, -1.0).astype(I32)

    ones = jnp.ones((SUBLANES, LANES), BF16)
    tot_l = lax.dot_general(ones, xf.astype(BF16), _NT, preferred_element_type=F32)
    off_l = jnp.dot(tot_l.astype(BF16), upper_strict, preferred_element_type=F32)[0:1]
    tot_l = tot_l[0:1]

    jf = lax.broadcasted_iota(I32, (cap, nrow), 0).astype(F32)
    oh_row = (off_l <= jf) & (jf < off_l + tot_l)
    ohb = jnp.where(oh_row, 1.0, 0.0).astype(BF16)
    row_id = lax.broadcasted_iota(I32, (cap, nrow), 1).astype(F32)
    off_j = jnp.sum(jnp.where(oh_row, off_l, 0.0), axis=1, keepdims=True)
    row_j = jnp.sum(jnp.where(oh_row, row_id, 0.0), axis=1, keepdims=True)
    key = jnp.where(sel, incl, 0.0).astype(BF16)
    g = jnp.dot(ohb, key, preferred_element_type=F32)
    target = lax.broadcasted_iota(I32, (cap, 1), 0).astype(F32) - off_j + 1.0
    oh_lane = g == target
    lane_id = lax.broadcasted_iota(I32, (cap, LANES), 1).astype(F32)
    lane_j = jnp.sum(jnp.where(oh_lane, lane_id, 0.0), axis=1, keepdims=True)
    idx_ref[0] = (row_j * LANES + lane_j).astype(I32)

    a1 = a.astype(BF16)
    r1 = a - a1.astype(F32)
    a2 = r1.astype(BF16)
    a3 = (r1 - a2.astype(F32)).astype(BF16)
    arow = (jnp.dot(ohb, a1, preferred_element_type=F32) + jnp.dot(ohb, a2, preferred_element_type=F32)
            + jnp.dot(ohb, a3, preferred_element_type=F32))
    gate_ref[0] = jnp.sum(jnp.where(oh_lane, arow, 0.0), axis=1, keepdims=True)


def _route(aff3, cap):
    nrow = aff3.shape[1]
    return pl.pallas_call(
        functools.partial(_route_kernel, cap=cap, nrow=nrow),
        grid=(N_EXPERTS,),
        in_specs=[pl.BlockSpec((N_EXPERTS * nrow, LANES), lambda e: (0, 0)),
                  pl.BlockSpec((1, nrow, LANES), lambda e: (e, 0, 0))],
        out_specs=[pl.BlockSpec((1, cap, 1), lambda e: (e, 0, 0)),
                   pl.BlockSpec((1, cap, 1), lambda e: (e, 0, 0)),
                   pl.BlockSpec((1, nrow, LANES), lambda e: (e, 0, 0)),
                   pl.BlockSpec((1, nrow, LANES), lambda e: (e, 0, 0))],
        out_shape=[jax.ShapeDtypeStruct((N_EXPERTS, cap, 1), I32),
                   jax.ShapeDtypeStruct((N_EXPERTS, cap, 1), F32),
                   jax.ShapeDtypeStruct((N_EXPERTS, nrow, LANES), I32),
                   jax.ShapeDtypeStruct((N_EXPERTS, nrow, LANES), I32)],
        scratch_shapes=[pltpu.VMEM((N_EXPERTS, SUBLANES, LANES), F32)],
        compiler_params=_cparams(("arbitrary",)),
        name="ec_route",
    )(aff3.reshape(N_EXPERTS * nrow, LANES), aff3)


def _ffn_kernel(idx_ref, h2_hbm, gate_ref, w1_ref, w3_ref, w2_ref, ye_ref, buf, sem, wb_scr, *, tc, nc):
    sub = D // LANES

    @pl.when(pl.program_id(1) == 0)
    def _():
        wb_scr[0] = w1_ref[0, 0].astype(BF16)
        wb_scr[1] = w3_ref[0, 0].astype(BF16)
        wb_scr[2] = w2_ref[0, 0].astype(BF16)

    step = pl.program_id(0) * nc + pl.program_id(1)
    nsteps = N_EXPERTS * nc

    def row_copy(tok, r, slot):
        return pltpu.make_async_copy(h2_hbm.at[pl.ds(pl.multiple_of(tok * sub, sub), sub), :],
                                     buf.at[slot, pl.ds(pl.multiple_of(r * sub, sub), sub), :],
                                     sem.at[slot])

    def wait_slot(slot):
        pltpu.make_async_copy(h2_hbm.at[pl.ds(0, tc * sub), :], buf.at[slot], sem.at[slot]).wait()

    @pl.when(step == 0)
    def _():
        def body(r, carry):
            row_copy(idx_ref[0, r], r, 0).start()
            return carry

        lax.fori_loop(0, tc, body, 0, unroll=16)

    nxt = jnp.minimum(step + 1, nsteps - 1)
    nxt_e = nxt // nc
    nxt_base = (nxt % nc) * tc
    nslot = (step + 1) % 2
    parts = 4 if tc % 4 == 0 else 1

    def issue_part(q):
        for r in range(q * tc // parts, (q + 1) * tc // parts):
            row_copy(idx_ref[nxt_e, nxt_base + r], r, nslot).start(priority=r % 2)

    slot = step % 2
    wait_slot(slot)
    xin = jnp.concatenate([buf[slot, pl.ds(s, tc, stride=sub), :] for s in range(sub)], axis=1).astype(BF16)
    issue_part(0)
    a = jnp.dot(xin, wb_scr[0], preferred_element_type=F32)
    if parts == 4:
        issue_part(1)
    b = jnp.dot(xin, wb_scr[1], preferred_element_type=F32)
    hid = (a * jax.nn.sigmoid(a) * b).astype(BF16)
    if parts == 4:
        issue_part(2)
    y = jnp.dot(hid, wb_scr[2], preferred_element_type=F32)
    if parts == 4:
        issue_part(3)
    ye_ref[0] = (y * gate_ref[0]).astype(BF16)

    @pl.when(step == nsteps - 1)
    def _():
        wait_slot(nslot)


def _ffn(idx, gate, h2, w1, w3, w2, layer, tc):
    cap = idx.shape[1]
    nc = cap // tc
    sub = D // LANES
    wspec = pl.BlockSpec((1, 1, D, D), lambda e, c, idx: (layer, e, 0, 0))
    grid_spec = pltpu.PrefetchScalarGridSpec(
        num_scalar_prefetch=1,
        grid=(N_EXPERTS, nc),
        in_specs=[pl.BlockSpec(memory_space=pl.ANY),
                  pl.BlockSpec((1, tc, 1), lambda e, c, idx: (e, c, 0)),
                  wspec, wspec, wspec],
        out_specs=pl.BlockSpec((1, tc, D), lambda e, c, idx: (e, c, 0)),
        scratch_shapes=[pltpu.VMEM((2, tc * sub, LANES), F32),
                        pltpu.SemaphoreType.DMA((2,)),
                        pltpu.VMEM((3, D, D), BF16)])
    return pl.pallas_call(
        functools.partial(_ffn_kernel, tc=tc, nc=nc),
        grid_spec=grid_spec,
        out_shape=jax.ShapeDtypeStruct((N_EXPERTS, cap, D), BF16),
        compiler_params=_cparams(("arbitrary", "arbitrary")),
        name="expert_ffn",
    )(idx, h2, gate, w1, w3, w2)


def _combine_kernel(rs_ref, x_ref, slot_ref, mod_ref, gf_ref, ye_hbm, o_ref, win, sem, y_scr,
                    *, win_rows, cap, nt, tt, mrow, final):
    i = pl.program_id(0)

    def start_of(e, t):
        s0 = rs_ref[e, t]
        s_al = (s0 // BF16_ROWS) * BF16_ROWS
        return pl.multiple_of(jnp.minimum(s_al, cap - win_rows), BF16_ROWS)

    def copy(e, t, sl):
        return pltpu.make_async_copy(ye_hbm.at[e, pl.ds(start_of(e, t), win_rows), :],
                                     win.at[sl, e], sem.at[sl])

    @pl.when(i == 0)
    def _():
        for e in range(N_EXPERTS):
            copy(e, i, 0).start()

    @pl.when(i + 1 < nt)
    def _():
        for e in range(N_EXPERTS):
            copy(e, i + 1, (i + 1) % 2).start()

    sl = i % 2
    for e in range(N_EXPERTS):
        copy(e, i, sl).wait()

    def window_sum(r0, r1):
        acc = jnp.zeros((tt, D), F32)
        wi = lax.broadcasted_iota(I32, (r1 - r0, tt), 0) + r0
        for e in range(N_EXPERTS):
            rel = slot_ref[e:e + 1, :] - start_of(e, i)
            oh_t = jnp.where(wi == rel, 1.0, 0.0).astype(BF16)
            acc = acc + lax.dot_general(oh_t, win[sl, e, r0:r1, :], _TN, preferred_element_type=F32)
        return acc

    main_rows = min(win_rows, tt)
    gt2 = mod_ref[mrow:mrow + 1, 5 * D:6 * D]
    y_scr[...] = x_ref[...] + gt2 * window_sum(0, main_rows)
    if win_rows > main_rows:
        need_tail = False
        for e in range(N_EXPERTS):
            end = rs_ref[e, i + 1] - start_of(e, i)
            need_tail = jnp.logical_or(need_tail, end > main_rows)

        @pl.when(need_tail)
        def _():
            y_scr[...] += gt2 * window_sum(main_rows, win_rows)

    y = y_scr[...]
    if final:
        ms = jnp.mean(y * y, axis=-1, keepdims=True)
        y = y * lax.rsqrt(ms + EPS) * gf_ref[...]
    o_ref[...] = y


def _combine(rstart, x1, slot, mod, mrow, g_final, ye, tt, final):
    n = x1.shape[0]
    cap = ye.shape[1]
    nt = n // tt
    win_rows = min(cap, tt + BF16_ROWS)
    grid_spec = pltpu.PrefetchScalarGridSpec(
        num_scalar_prefetch=1,
        grid=(nt,),
        in_specs=[pl.BlockSpec((tt, D), lambda i, rs: (i, 0)),
                  pl.BlockSpec((N_EXPERTS, tt), lambda i, rs: (0, i)),
                  pl.BlockSpec((SUBLANES, N_MOD * D), lambda i, rs: (0, 0)),
                  pl.BlockSpec((1, D), lambda i, rs: (0, 0)),
                  pl.BlockSpec(memory_space=pl.ANY)],
        out_specs=pl.BlockSpec((tt, D), lambda i, rs: (i, 0)),
        scratch_shapes=[pltpu.VMEM((2, N_EXPERTS, win_rows, D), BF16),
                        pltpu.SemaphoreType.DMA((2,)),
                        pltpu.VMEM((tt, D), F32)])
    return pl.pallas_call(
        functools.partial(_combine_kernel, win_rows=win_rows, cap=cap, nt=nt, tt=tt, mrow=mrow, final=final),
        grid_spec=grid_spec,
        out_shape=jax.ShapeDtypeStruct((n, D), F32),
        compiler_params=_cparams(("arbitrary",)),
        name="moe_combine",
    )(rstart, x1, slot, mod, g_final, ye)


def _rope_tables(n_lat):
    rows = n_lat // GRID_W
    inv = ROPE_THETA ** (-jnp.arange(0, ROPE_AXIS, 2, dtype=F32) / ROPE_AXIS)
    ang_r = jnp.arange(rows).astype(F32)[:, None] * inv
    ang_c = jnp.arange(GRID_W).astype(F32)[:, None] * inv
    cr, sr, cc, sc = jnp.cos(ang_r), jnp.sin(ang_r), jnp.cos(ang_c), jnp.sin(ang_c)
    zr, zc = jnp.zeros_like(sr), jnp.zeros_like(sc)
    rep = lambda parts: jnp.concatenate(parts * (LANES // DA_DH), axis=1)

    def table(row_parts, col_parts):
        r = rep([row_parts[0], row_parts[1], zr, zr])
        c = rep([zc, zc, col_parts[0], col_parts[1]])
        return (r[:, None, :] + c[None, :, :]).reshape(n_lat, LANES)

    return (table((cr, cr), (cc, cc)), table((zr, sr), (zc, sc)), table((-sr, zr), (-sc, zc)))


def _largest_tile(n, unit, limit):
    best = unit
    for k in range(1, n // unit + 1):
        if n % (k * unit) == 0 and k * unit <= limit:
            best = k * unit
    return best


def _moe(x1, h2, aff_t, mod, mrow, w1, w3, w2, layer, g_final, final):
    n = x1.shape[0]
    cap = max(1, EC_CAPACITY * n // N_EXPERTS)
    tile = LANES * LANES if n <= LANES * LANES else n
    n_pad = max(n, tile)
    if n_pad > n:
        aff_t = jnp.concatenate([aff_t, jnp.full((N_EXPERTS, n_pad - n), -1.0, F32)], axis=1)
    idx, gate, slot, rank = _route(aff_t.reshape(N_EXPERTS, n_pad // LANES, LANES), cap)
    tt = min(256, n)
    slot = slot.reshape(N_EXPERTS, n_pad)[:, :n]
    rstart = jnp.concatenate([rank.reshape(N_EXPERTS, n_pad)[:, 0:n:tt],
                              jnp.full((N_EXPERTS, 1), cap, I32)], axis=1)
    tc = min(cap, 512)
    ye = _ffn(idx.reshape(N_EXPERTS, cap), gate, h2, w1, w3, w2, layer, tc)
    return _combine(rstart, x1, slot, mod, mrow, g_final, ye, tt, final)


def kernel(x, c, ctx, c_ctx, w_ada, b_ada, g_norm1, g_norm2, w_in, na_rpb, da_lam_q1, da_lam_k1, da_lam_q2,
           da_lam_k2, da_subln_g, gm_ln_g, gm_ln_b, gm_w_s, gm_b_s, w_branch, w_out, w_router, w_e1, w_e3,
           w_e2, g_final):
    depth = w_ada.shape[0]
    n_lat, n_ctx = x.shape[1], ctx.shape[1]
    rows = n_lat // GRID_W
    xs, xc = x[0], ctx[0]

    cc = jnp.concatenate([c.reshape(1, D), c_ctx.reshape(1, D), jnp.zeros((SUBLANES - 2, D), F32)], axis=0)
    mods = _ada(cc, w_ada, b_ada)

    tabs_lat = _rope_tables(n_lat)
    tabs_ctx = (jnp.ones((n_ctx, LANES), F32), jnp.zeros((n_ctx, LANES), F32), jnp.zeros((n_ctx, LANES), F32))
    tm_proj = _largest_tile(n_lat, 256, 1024)
    tk = _largest_tile(n_lat, 256, 2048)
    tq = 512

    for i in range(depth):
        last = i == depth - 1
        lam_init = 0.8 - 0.6 * math.exp(-0.3 * i)
        mod = mods[i]
        w_in_b = w_in[i].astype(BF16)
        g1 = g_norm1[i].reshape(1, D)
        g2 = g_norm2[i].reshape(1, D)
        gf = g_final.reshape(1, D)
        lam_vecs = [v[i].reshape(1, DA_DH).astype(F32) for v in (da_lam_q1, da_lam_k1, da_lam_q2, da_lam_k2)]
        g_col = da_subln_g[i].reshape(DA_VD, 1)
        merge_w = (w_branch[i].astype(BF16), w_out[i].astype(BF16), gm_ln_g[i].reshape(1, BW),
                   gm_ln_b[i].reshape(1, BW), gm_w_s[i].astype(BF16), gm_b_s[i].T, g2, w_router[i].T)

        p_lat, qt_lat, vt_lat = _proj(xs, g1, mod, w_in_b, tabs_lat, 0, tm_proj)
        p_ctx, qt_ctx, vt_ctx = _proj(xc, g1, mod, w_in_b, tabs_ctx, 1, n_ctx)

        ya = _na(p_lat, p_ctx, _na_bias(na_rpb[i], rows))
        yb = _diff(qt_lat, p_ctx, vt_ctx, (p_lat, vt_lat), lam_vecs, g_col, lam_init, tq, tk)
        x1, h2, aff_t = _merge(xs, ya, yb, p_lat, mod, 0, *merge_w, tm=_largest_tile(n_lat, 256, 512))
        xs = _moe(x1, h2, aff_t, mod, 0, w_e1, w_e3, w_e2, i, gf, last)

        if not last:
            yac = _ctx_dense(p_ctx)
            ybc = _diff(qt_ctx, p_ctx, vt_ctx, None, lam_vecs, g_col, lam_init, n_ctx, n_ctx)
            x1c, h2c, aff_tc = _merge(xc, yac, ybc, p_ctx, mod, 1, *merge_w, tm=n_ctx)
            xc = _moe(x1c, h2c, aff_tc, mod, 1, w_e1, w_e3, w_e2, i, gf, False)
    return xs[None]
```

```python
import functools
import math

import numpy as np
import jax
import jax.numpy as jnp
from jax import lax
from jax.experimental import pallas as pl
from jax.experimental.pallas import tpu as pltpu

F32 = jnp.float32
BF16 = jnp.bfloat16
I32 = jnp.int32

D = 1024
GRID_W = 64
BW = 512
N_COLBLK = 14
NA_HEADS = 8
NA_WIN_R = 8
NA_WIN_C = 16
NA_ROWS = 4
NA_KROWS = 12
DA_HEADS = 4
DA_DH = 64
DA_VD = 128
GM_GROUPS = 4
GM_CHUNK = 128
N_EXPERTS = 16
EC_CAPACITY = 2
ROPE_THETA = 10000.0
ROPE_AXIS = DA_DH // 2
N_MOD = 6
EPS = 1e-6
LANES = 128
SUBLANES = 8
BF16_ROWS = 16
NEG_BIG = -1e30
VT_ROWS = BW + BF16_ROWS
Q_SCALE_LOG2 = DA_DH ** -0.5 * math.log2(math.e)
VMEM_LIMIT = 60000 * 1024
STALE_MARGIN = 64.0

CB_KA, CB_VA, CB_KB, CB_VB, CB_QA, CB_QB, CB_U, CB_V = range(8)

_NT = (((1,), (1,)), ((), ()))
_TN = (((0,), (0,)), ((), ()))


def _cparams(sem):
    return pltpu.CompilerParams(dimension_semantics=sem, vmem_limit_bytes=VMEM_LIMIT)


def _gelu(x):
    return 0.5 * x * (1.0 + jnp.tanh(math.sqrt(2.0 / math.pi) * (x + 0.044715 * (x * x * x))))


def _ada_kernel(c_ref, w_ref, b_ref, o_ref):
    c = c_ref[...]
    s = c * jax.nn.sigmoid(c)
    o_ref[0] = jnp.dot(s, w_ref[0], preferred_element_type=F32,
                       precision=lax.Precision.HIGHEST) + b_ref[0]


def _ada(cc, w_ada, b_ada):
    depth = w_ada.shape[0]
    tn = D
    return pl.pallas_call(
        _ada_kernel,
        grid=(depth, N_MOD * D // tn),
        in_specs=[pl.BlockSpec((SUBLANES, D), lambda i, j: (0, 0)),
                  pl.BlockSpec((1, D, tn), lambda i, j: (i, 0, j)),
                  pl.BlockSpec((1, 1, tn), lambda i, j: (i, 0, j))],
        out_specs=pl.BlockSpec((1, SUBLANES, tn), lambda i, j: (i, 0, j)),
        out_shape=jax.ShapeDtypeStruct((depth, SUBLANES, N_MOD * D), F32),
        compiler_params=_cparams(("arbitrary", "arbitrary")),
        name="ada_mod",
    )(cc, w_ada, b_ada.reshape(depth, 1, N_MOD * D))


def _proj_kernel(x_ref, g_ref, mod_ref, w_ref, c_ref, sp_ref, sm_ref, p_ref, qt_ref, vt_ref, h_scr, *, mrow):
    j = pl.program_id(1)

    @pl.when(j == 0)
    def _():
        x = x_ref[...]
        ms = jnp.mean(x * x, axis=-1, keepdims=True)
        xn = x * lax.rsqrt(ms + EPS) * g_ref[...]
        sh = mod_ref[mrow:mrow + 1, 0:D]
        sc = mod_ref[mrow:mrow + 1, D:2 * D]
        h_scr[...] = (xn * (1.0 + sc) + sh).astype(BF16)

    z = jnp.dot(h_scr[...], w_ref[...], preferred_element_type=F32)

    def rope(z):
        reps = BW // LANES
        c = jnp.concatenate([c_ref[...]] * reps, axis=1)
        sp = jnp.concatenate([sp_ref[...]] * reps, axis=1)
        sm = jnp.concatenate([sm_ref[...]] * reps, axis=1)
        half = ROPE_AXIS // 2
        return z * c + pltpu.roll(z, half, 1) * sp + pltpu.roll(z, BW - half, 1) * sm

    @pl.when(j == CB_KB // 2)
    def _():
        p_ref[:, 0:BW] = rope(z[:, 0:BW]).astype(BF16)
        zv = z[:, BW:2 * BW]
        p_ref[:, BW:2 * BW] = zv.astype(BF16)
        vt_ref[0:BW, :] = zv.T.astype(BF16)
        vt_ref[BW:VT_ROWS, :] = jnp.ones((VT_ROWS - BW, z.shape[0]), BF16)

    @pl.when(j == CB_QB // 2)
    def _():
        p_ref[:, 0:BW] = z[:, 0:BW].astype(BF16)
        zr = rope(z[:, BW:2 * BW])
        p_ref[:, BW:2 * BW] = zr.astype(BF16)
        qt_ref[...] = (zr * Q_SCALE_LOG2).T.astype(BF16)

    @pl.when((j != CB_KB // 2) & (j != CB_QB // 2))
    def _():
        p_ref[...] = z.astype(BF16)


def _proj(x, g, mod, w, tabs, mrow, tm):
    n = x.shape[0]
    return pl.pallas_call(
        functools.partial(_proj_kernel, mrow=mrow),
        grid=(n // tm, N_COLBLK // 2),
        in_specs=[pl.BlockSpec((tm, D), lambda i, j: (i, 0)),
                  pl.BlockSpec((1, D), lambda i, j: (0, 0)),
                  pl.BlockSpec((SUBLANES, N_MOD * D), lambda i, j: (0, 0)),
                  pl.BlockSpec((D, 2 * BW), lambda i, j: (0, j)),
                  pl.BlockSpec((tm, LANES), lambda i, j: (i, 0)),
                  pl.BlockSpec((tm, LANES), lambda i, j: (i, 0)),
                  pl.BlockSpec((tm, LANES), lambda i, j: (i, 0))],
        out_specs=[pl.BlockSpec((tm, 2 * BW), lambda i, j: (i, j)),
                   pl.BlockSpec((BW, tm), lambda i, j: (0, i)),
                   pl.BlockSpec((VT_ROWS, tm), lambda i, j: (0, i))],
        out_shape=[jax.ShapeDtypeStruct((n, N_COLBLK * BW), BF16),
                   jax.ShapeDtypeStruct((BW, n), BF16),
                   jax.ShapeDtypeStruct((VT_ROWS, n), BF16)],
        scratch_shapes=[pltpu.VMEM((tm, D), BF16)],
        compiler_params=_cparams(("arbitrary", "arbitrary")),
        name="proj",
    )(x, g, mod, w, *tabs)


def _na_kernel(q_ref, k0_ref, k1_ref, k2_ref, v0_ref, v1_ref, v2_ref, kc_ref, vc_ref, bias_ref, o_ref,
               p0_scr, p1_scr):
    nq = q_ref.shape[0]
    lane = lax.broadcasted_iota(I32, (nq, LANES), 1)
    k_refs = (kc_ref, k0_ref, k1_ref, k2_ref)
    v_refs = (vc_ref, v0_ref, v1_ref, v2_ref)
    p_scr = (p0_scr, p1_scr)
    widths = [r.shape[0] for r in k_refs]
    offs = [sum(widths[:t]) for t in range(len(widths) + 1)]

    def head_lanes(h):
        return slice((h // 2) * LANES, (h // 2 + 1) * LANES)

    def scores_softmax(h):
        sl = head_lanes(h)
        qp = q_ref[:, sl] * 0.125
        keep = (lane < 64) if h % 2 == 0 else (lane >= 64)
        qz = jnp.where(keep, qp, jnp.zeros_like(qp))
        ss = [lax.dot_general(qz, kc_ref[:, sl], _NT, preferred_element_type=F32)]
        for t in range(3):
            s = lax.dot_general(qz, k_refs[1 + t][:, sl], _NT, preferred_element_type=F32)
            ss.append(s + bias_ref[0, h, :, t * nq:(t + 1) * nq])
        m = ss[0].max(axis=-1, keepdims=True)
        for s in ss[1:]:
            m = jnp.maximum(m, s.max(axis=-1, keepdims=True))
        l = jnp.zeros_like(m)
        for t, s in enumerate(ss):
            p = jnp.exp(s - m)
            l = l + p.sum(axis=-1, keepdims=True)
            p_scr[h % 2][:, offs[t]:offs[t + 1]] = p.astype(BF16)
        return l

    def values(h, l):
        sl = head_lanes(h)
        o = jnp.zeros((nq, LANES), F32)
        for t, v_ref in enumerate(v_refs):
            o = o + jnp.dot(p_scr[h % 2][:, offs[t]:offs[t + 1]], v_ref[:, sl], preferred_element_type=F32)
        return o / l

    denom, outs = {}, {}
    for r in range(NA_HEADS + 1):
        if r < NA_HEADS:
            denom[r] = scores_softmax(r)
        if r >= 1:
            h = r - 1
            outs[h] = values(h, denom[h])
            if h % 2 == 1:
                o_ref[:, head_lanes(h)] = jnp.where(lane < 64, outs[h - 1], outs[h]).astype(BF16)


def _na_bias(rpb, rows):
    nb = rows // NA_ROWS
    c = np.arange(GRID_W)[:, None]
    kc = np.arange(GRID_W)[None, :]
    c0 = np.clip(c - NA_WIN_C // 2, 0, GRID_W - NA_WIN_C)
    col_ok = (kc >= c0) & (kc < c0 + NA_WIN_C)
    dc = kc - c + (NA_WIN_C - 1)
    sel = np.stack([(dc == d) & col_ok for d in range(2 * NA_WIN_C - 1)]).astype(np.float32)
    toep = jnp.einsum("hrd,dck->hrck", rpb.astype(F32), sel, precision=lax.Precision.HIGHEST)
    toep = toep + np.where(col_ok, 0.0, NEG_BIG).astype(np.float32)
    n_dr = 2 * NA_WIN_R - 1
    toep = jnp.concatenate([toep, jnp.full((NA_HEADS, 1, GRID_W, GRID_W), NEG_BIG, F32)], axis=1)
    which = np.full((3, NA_ROWS, NA_KROWS), n_dr, np.int32)
    for v, b in enumerate((0, 1, nb - 1)):
        kb0 = min(max(b - 1, 0), nb - 3)
        for a in range(NA_ROWS):
            r = NA_ROWS * b + a
            r0 = min(max(r - NA_WIN_R // 2, 0), rows - NA_WIN_R)
            for i in range(NA_KROWS):
                kr = NA_ROWS * kb0 + i
                if r0 <= kr < r0 + NA_WIN_R:
                    which[v, a, i] = kr - r + NA_WIN_R - 1
    blocks = jnp.take(toep, which.reshape(-1), axis=1)
    blocks = blocks.reshape(NA_HEADS, 3, NA_ROWS, NA_KROWS, GRID_W, GRID_W)
    return blocks.transpose(1, 0, 2, 4, 3, 5).reshape(3, NA_HEADS, NA_ROWS * GRID_W, NA_KROWS * GRID_W)


def _na(p_lat, p_ctx, bias):
    nq = NA_ROWS * GRID_W
    n_lat, n_ctx = p_lat.shape[0], p_ctx.shape[0]
    nb = n_lat // nq

    def kmap(t, col):
        return lambda b: (jnp.clip(b - 1, 0, nb - 3) + t, col)

    def bmap(b):
        return (jnp.where(b == 0, 0, jnp.where(b == nb - 1, 2, 1)), 0, 0, 0)

    blk = lambda f: pl.BlockSpec((nq, BW), f)
    return pl.pallas_call(
        _na_kernel,
        grid=(nb,),
        in_specs=[blk(lambda b: (b, CB_QA)),
                  blk(kmap(0, CB_KA)), blk(kmap(1, CB_KA)), blk(kmap(2, CB_KA)),
                  blk(kmap(0, CB_VA)), blk(kmap(1, CB_VA)), blk(kmap(2, CB_VA)),
                  pl.BlockSpec((n_ctx, BW), lambda b: (0, CB_KA)),
                  pl.BlockSpec((n_ctx, BW), lambda b: (0, CB_VA)),
                  pl.BlockSpec((1, NA_HEADS, nq, NA_KROWS * GRID_W), bmap)],
        out_specs=pl.BlockSpec((nq, BW), lambda b: (b, 0)),
        out_shape=jax.ShapeDtypeStruct((n_lat, BW), BF16),
        scratch_shapes=[pltpu.VMEM((nq, n_ctx + NA_KROWS * GRID_W), BF16)] * 2,
        compiler_params=_cparams(("arbitrary",)),
        name="na_attn",
    )(p_lat, p_lat, p_lat, p_lat, p_lat, p_lat, p_lat, p_ctx, p_ctx, bias)


def _ctx_dense_kernel(q_ref, k_ref, v_ref, o_ref):
    nq = q_ref.shape[0]
    lane = lax.broadcasted_iota(I32, (nq, LANES), 1)
    for g in range(NA_HEADS // 2):
        sl = slice(g * LANES, (g + 1) * LANES)
        qp = q_ref[:, sl] * 0.125
        kp = k_ref[:, sl]
        vp = v_ref[:, sl]
        outs = []
        for sub in range(2):
            keep = (lane < 64) if sub == 0 else (lane >= 64)
            qz = jnp.where(keep, qp, jnp.zeros_like(qp))
            s = lax.dot_general(qz, kp, _NT, preferred_element_type=F32)
            m = s.max(axis=-1, keepdims=True)
            p = jnp.exp(s - m)
            l = p.sum(axis=-1, keepdims=True)
            outs.append(jnp.dot(p.astype(BF16), vp, preferred_element_type=F32) / l)
        o_ref[:, sl] = jnp.where(lane < 64, outs[0], outs[1]).astype(BF16)


def _ctx_dense(p_ctx):
    n_ctx = p_ctx.shape[0]
    blk = lambda col: pl.BlockSpec((n_ctx, BW), lambda i: (0, col))
    return pl.pallas_call(
        _ctx_dense_kernel,
        grid=(1,),
        in_specs=[blk(CB_QA), blk(CB_KA), blk(CB_VA)],
        out_specs=pl.BlockSpec((n_ctx, BW), lambda i: (0, 0)),
        out_shape=jax.ShapeDtypeStruct((n_ctx, BW), BF16),
        compiler_params=_cparams(("arbitrary",)),
        name="ctx_dense_attn",
    )(p_ctx, p_ctx, p_ctx)


def _diff_kernel(*refs, tq, nk, lam_init, with_lat):
    if with_lat:
        (qt_ref, kc_ref, vtc_ref, k_ref, vt_ref, lq1_ref, lk1_ref, lq2_ref, lk2_ref, g_ref, o_ref,
         qz_scr, m_scr, acc_scr, bak_scr, s0_scr, s1_scr, p0_scr, p1_scr) = refs
    else:
        (qt_ref, kc_ref, vtc_ref, lq1_ref, lk1_ref, lq2_ref, lk2_ref, g_ref, o_ref,
         qz_scr, m_scr, acc_scr, bak_scr, s0_scr, s1_scr, p0_scr, p1_scr) = refs
    s_scr = (s0_scr, s1_scr)
    p_scr = (p0_scr, p1_scr)
    j = pl.program_id(1)
    lanes2 = 2 * tq
    rc = max(BF16_ROWS, BF16_ROWS * 1024 // lanes2)
    ck = 256

    def attend_all(k_ref, vt_ref):
        ns = k_ref.shape[0] // ck

        def head_cols(h):
            return slice(h * DA_VD, (h + 1) * DA_VD)

        def run_round(h_qk, h_exp, h_pv, m_b):
            mx = None
            for i in range(ns):
                rows = slice(i * ck, (i + 1) * ck)
                if h_qk is not None:
                    s = jnp.dot(k_ref[rows, head_cols(h_qk)], qz_scr[h_qk], preferred_element_type=F32)
                    s_scr[h_qk % 2][rows, :] = s
                    part = s.reshape(ck // rc, rc, lanes2).max(axis=0)
                    mx = part if mx is None else jnp.maximum(mx, part)
                if h_exp is not None:
                    for c in range(i * ck // rc, (i + 1) * ck // rc):
                        rr = slice(c * rc, (c + 1) * rc)
                        p_scr[h_exp % 2][rr, :] = jnp.exp2(s_scr[h_exp % 2][rr, :] - m_b).astype(BF16)
                if h_pv is not None:
                    vt_ext = jnp.concatenate([vt_ref[head_cols(h_pv), rows], vt_ref[BW:VT_ROWS, rows]], axis=0)
                    acc_scr[h_pv] += jnp.dot(vt_ext, p_scr[h_pv % 2][rows, :], preferred_element_type=F32)
            return mx

        mx = None
        for r in range(DA_HEADS + 2):
            h_qk = r if r < DA_HEADS else None
            h_exp = r - 1 if 0 <= r - 1 < DA_HEADS else None
            h_pv = r - 2 if 0 <= r - 2 < DA_HEADS else None
            m_b = None
            if h_exp is not None:
                m_old = m_scr[h_exp]
                m_new = jnp.maximum(m_old, mx.max(axis=0, keepdims=True))
                m_scr[h_exp] = m_new
                acc_scr[h_exp] = jnp.exp2(m_old - m_new) * acc_scr[h_exp]
                m_b = jnp.broadcast_to(m_new, (rc, lanes2))
            mx = run_round(h_qk, h_exp, h_pv, m_b)

    @pl.when(j == 0)
    def _():
        row = lax.broadcasted_iota(I32, (DA_VD, tq), 0)
        for h in range(DA_HEADS):
            qh = qt_ref[h * DA_VD:(h + 1) * DA_VD, :]
            zero = jnp.zeros_like(qh)
            qz_scr[h] = jnp.concatenate([jnp.where(row < DA_DH, qh, zero),
                                         jnp.where(row >= DA_DH, qh, zero)], axis=1)
        m_scr[...] = jnp.full(m_scr.shape, -jnp.inf, F32)
        acc_scr[...] = jnp.zeros(acc_scr.shape, F32)
        attend_all(kc_ref, vtc_ref)

    def attend_stale(k_ref, vt_ref):
        ns = k_ref.shape[0] // ck
        maxima = []
        for r in range(DA_HEADS + 1):
            h_qk = r if r < DA_HEADS else None
            h_pv = r - 1 if r >= 1 else None
            if h_qk is not None:
                m_b = jnp.broadcast_to(m_scr[h_qk], (rc, lanes2))
            mx = None
            for i in range(ns):
                rows = slice(i * ck, (i + 1) * ck)
                if h_qk is not None:
                    cols = slice(h_qk * DA_VD, (h_qk + 1) * DA_VD)
                    s = jnp.dot(k_ref[rows, cols], qz_scr[h_qk], preferred_element_type=F32)
                    part = s.reshape(ck // rc, rc, lanes2).max(axis=0)
                    mx = part if mx is None else jnp.maximum(mx, part)
                    for c in range(ck // rc):
                        rr = slice(i * ck + c * rc, i * ck + (c + 1) * rc)
                        p_scr[h_qk % 2][rr, :] = jnp.exp2(s[c * rc:(c + 1) * rc, :] - m_b).astype(BF16)
                if h_pv is not None:
                    cols = slice(h_pv * DA_VD, (h_pv + 1) * DA_VD)
                    vt_ext = jnp.concatenate([vt_ref[cols, rows], vt_ref[BW:VT_ROWS, rows]], axis=0)
                    acc_scr[h_pv] += jnp.dot(vt_ext, p_scr[h_pv % 2][rows, :], preferred_element_type=F32)
            if h_qk is not None:
                maxima.append(mx.max(axis=0, keepdims=True))
        return maxima

    if with_lat:
        bak_scr[...] = acc_scr[...]
        maxima = attend_stale(k_ref, vt_ref)
        excess = maxima[0] - m_scr[0]
        for h in range(1, DA_HEADS):
            excess = jnp.maximum(excess, maxima[h] - m_scr[h])
        safe = jnp.max(excess) <= STALE_MARGIN

        @pl.when(safe)
        def _():
            for h in range(DA_HEADS):
                m_old = m_scr[h]
                m_new = jnp.maximum(m_old, maxima[h])
                m_scr[h] = m_new
                acc_scr[h] = jnp.exp2(m_old - m_new) * acc_scr[h]

        @pl.when(jnp.logical_not(safe))
        def _():
            acc_scr[...] = bak_scr[...]
            attend_all(k_ref, vt_ref)

    @pl.when(j == nk - 1)
    def _():
        lam = (jnp.exp(jnp.sum(lq1_ref[...] * lk1_ref[...], keepdims=True))
               - jnp.exp(jnp.sum(lq2_ref[...] * lk2_ref[...], keepdims=True)) + lam_init)
        for h in range(DA_HEADS):
            o = acc_scr[h, 0:DA_VD, :] / acc_scr[h, DA_VD:DA_VD + 1, :]
            od = o[:, :tq] - lam * o[:, tq:]
            ms = jnp.mean(od * od, axis=0, keepdims=True)
            y = od * lax.rsqrt(ms + EPS) * g_ref[...] * (1.0 - lam_init)
            o_ref[:, h * DA_VD:(h + 1) * DA_VD] = y.T.astype(BF16)


def _diff(qt, p_ctx, vt_ctx, lat, lam_vecs, g_col, lam_init, tq, tk):
    n_q, n_ctx = qt.shape[1], p_ctx.shape[0]
    with_lat = lat is not None
    nk = lat[0].shape[0] // tk if with_lat else 1
    vec = pl.BlockSpec((1, DA_DH), lambda i, j: (0, 0))
    in_specs = [pl.BlockSpec((BW, tq), lambda i, j: (0, i)),
                pl.BlockSpec((n_ctx, BW), lambda i, j: (0, CB_KB)),
                pl.BlockSpec((VT_ROWS, n_ctx), lambda i, j: (0, 0))]
    args = [qt, p_ctx, vt_ctx]
    if with_lat:
        in_specs += [pl.BlockSpec((tk, BW), lambda i, j: (j, CB_KB)),
                     pl.BlockSpec((VT_ROWS, tk), lambda i, j: (0, j))]
        args += list(lat)
    in_specs += [vec, vec, vec, vec, pl.BlockSpec((DA_VD, 1), lambda i, j: (0, 0))]
    return pl.pallas_call(
        functools.partial(_diff_kernel, tq=tq, nk=nk, lam_init=lam_init, with_lat=with_lat),
        grid=(n_q // tq, nk),
        in_specs=in_specs,
        out_specs=pl.BlockSpec((tq, BW), lambda i, j: (i, 0)),
        out_shape=jax.ShapeDtypeStruct((n_q, BW), BF16),
        scratch_shapes=[pltpu.VMEM((DA_HEADS, DA_VD, 2 * tq), BF16),
                        pltpu.VMEM((DA_HEADS, 1, 2 * tq), F32),
                        pltpu.VMEM((DA_HEADS, DA_VD + BF16_ROWS, 2 * tq), F32),
                        pltpu.VMEM((DA_HEADS, DA_VD + BF16_ROWS, 2 * tq), F32)]
        + [pltpu.VMEM((max(tk, n_ctx), 2 * tq), F32)] * 2
        + [pltpu.VMEM((max(tk, n_ctx), 2 * tq), BF16)] * 2,
        compiler_params=_cparams(("arbitrary", "arbitrary")),
        name="diff_attn",
    )(*args, *lam_vecs, g_col)


def _merge_kernel(x_ref, ya_ref, yb_ref, u_ref, v_ref, ga_ref, gb_ref, gc_ref, mod_ref, wb_ref, wo_ref,
                  lng_ref, lnb_ref, ws_ref, bst_ref, g2_ref, wrt_ref,
                  x1_ref, h2_ref, aff_ref, yc_scr, *, mrow, tm):
    ug = _gelu(u_ref[...].astype(F32))
    vg = _gelu(v_ref[...].astype(F32))
    mu = jnp.mean(vg, axis=-1, keepdims=True)
    var = jnp.mean(jnp.square(vg - mu), axis=-1, keepdims=True)
    vn = ((vg - mu) * lax.rsqrt(var + EPS) * lng_ref[...] + lnb_ref[...]).astype(BF16)
    for ci in range(tm // GM_CHUNK):
        rs = slice(ci * GM_CHUNK, (ci + 1) * GM_CHUNK)
        for g in range(GM_GROUPS):
            cs = slice(g * LANES, (g + 1) * LANES)
            mixed = jnp.dot(ws_ref[g], vn[rs, cs], preferred_element_type=F32) + bst_ref[:, g:g + 1]
            yc_scr[rs, cs] = (ug[rs, cs] * mixed).astype(BF16)

    sig = jax.nn.sigmoid
    m = sig(ga_ref[...].astype(F32)) * jnp.dot(ya_ref[...], wb_ref[0], preferred_element_type=F32)
    m = m + sig(gb_ref[...].astype(F32)) * jnp.dot(yb_ref[...], wb_ref[1], preferred_element_type=F32)
    m = m + sig(gc_ref[...].astype(F32)) * jnp.dot(yc_scr[...], wb_ref[2], preferred_element_type=F32)
    y = jnp.dot(m.astype(BF16), wo_ref[...], preferred_element_type=F32)
    gt1 = mod_ref[mrow:mrow + 1, 2 * D:3 * D]
    x1 = x_ref[...] + gt1 * y
    x1_ref[...] = x1

    ms = jnp.mean(x1 * x1, axis=-1, keepdims=True)
    sh2 = mod_ref[mrow:mrow + 1, 3 * D:4 * D]
    sc2 = mod_ref[mrow:mrow + 1, 4 * D:5 * D]
    h2 = x1 * lax.rsqrt(ms + EPS) * g2_ref[...] * (1.0 + sc2) + sh2
    for s in range(D // LANES):
        h2_ref[pl.ds(s, tm, stride=D // LANES), :] = h2[:, s * LANES:(s + 1) * LANES]

    hh = h2.astype(BF16)
    hl = (h2 - hh.astype(F32)).astype(BF16)
    w = wrt_ref[...]
    wh = w.astype(BF16)
    wl = (w - wh.astype(F32)).astype(BF16)
    lg = (lax.dot_general(wh, hh, _NT, preferred_element_type=F32)
          + lax.dot_general(wh, hl, _NT, preferred_element_type=F32)
          + lax.dot_general(wl, hh, _NT, preferred_element_type=F32))
    e = jnp.exp(lg - lg.max(axis=0, keepdims=True))
    aff_ref[...] = e / e.sum(axis=0, keepdims=True)


def _merge(x, ya, yb, p_all, mod, mrow, wb, wo, lng, lnb, ws, bst, g2, wrt, tm):
    n = x.shape[0]
    const = lambda shape: pl.BlockSpec(shape, lambda i: (0,) * len(shape))
    return pl.pallas_call(
        functools.partial(_merge_kernel, mrow=mrow, tm=tm),
        grid=(n // tm,),
        in_specs=[pl.BlockSpec((tm, D), lambda i: (i, 0)),
                  pl.BlockSpec((tm, BW), lambda i: (i, 0)),
                  pl.BlockSpec((tm, BW), lambda i: (i, 0)),
                  pl.BlockSpec((tm, BW), lambda i: (i, CB_U)),
                  pl.BlockSpec((tm, BW), lambda i: (i, CB_V)),
                  pl.BlockSpec((tm, D), lambda i: (i, 4)),
                  pl.BlockSpec((tm, D), lambda i: (i, 5)),
                  pl.BlockSpec((tm, D), lambda i: (i, 6)),
                  const((SUBLANES, N_MOD * D)),
                  const((3, BW, D)), const((D, D)),
                  const((1, BW)), const((1, BW)),
                  const((GM_GROUPS, GM_CHUNK, GM_CHUNK)), const((GM_CHUNK, GM_GROUPS)),
                  const((1, D)), const((N_EXPERTS, D))],
        out_specs=[pl.BlockSpec((tm, D), lambda i: (i, 0)),
                   pl.BlockSpec((tm * (D // LANES), LANES), lambda i: (i, 0)),
                   pl.BlockSpec((N_EXPERTS, tm), lambda i: (0, i))],
        out_shape=[jax.ShapeDtypeStruct((n, D), F32),
                   jax.ShapeDtypeStruct((n * (D // LANES), LANES), F32),
                   jax.ShapeDtypeStruct((N_EXPERTS, n), F32)],
        scratch_shapes=[pltpu.VMEM((tm, BW), BF16)],
        compiler_params=_cparams(("arbitrary",)),
        name="merge_prenorm",
    )(x, ya, yb, p_all, p_all, p_all, p_all, p_all, mod, wb, wo, lng, lnb, ws, bst, g2, wrt)


def _route_kernel(all_ref, a_ref, idx_ref, gate_ref, slot_ref, rank_ref, thr_scr, *, cap, nrow):
    e = pl.program_id(0)

    @pl.when(e == 0)
    def _():
        bits = pltpu.bitcast(all_ref[...], I32).reshape(N_EXPERTS, nrow, LANES)

        def bisect(k, prefix):
            cand = prefix | jnp.left_shift(jnp.int32(1), 30 - k)
            hit = jnp.where(bits >= cand, 1.0, 0.0)
            cnt = hit.sum(axis=1, keepdims=True).sum(axis=2, keepdims=True)
            return jnp.where(cnt >= cap, cand, prefix)

        thr_bits = lax.fori_loop(0, 31, bisect, jnp.zeros((N_EXPERTS, 1, 1), I32))
        tb = jnp.broadcast_to(thr_bits, (N_EXPERTS, SUBLANES, LANES)).reshape(N_EXPERTS * SUBLANES, LANES)
        thr_scr[...] = pltpu.bitcast(tb, F32).reshape(N_EXPERTS, SUBLANES, LANES)

    a = a_ref[0]
    thr = thr_scr[e][0:1, :]
    gt = a > thr
    eq = a == thr

    r_i = lax.broadcasted_iota(I32, (LANES, LANES), 0)
    c_i = lax.broadcasted_iota(I32, (LANES, LANES), 1)
    upper = (r_i <= c_i).astype(BF16)
    rr = lax.broadcasted_iota(I32, (nrow, nrow), 0)
    rc = lax.broadcasted_iota(I32, (nrow, nrow), 1)
    lower_strict = (rc < rr).astype(BF16)
    upper_strict = (rr < rc).astype(BF16)

    def prefix(mask):
        xf = jnp.where(mask, 1.0, 0.0)
        incl = jnp.dot(xf.astype(BF16), upper, preferred_element_type=F32)
        tot = jnp.broadcast_to(incl[:, LANES - 1:LANES], (nrow, LANES))
        base = jnp.dot(lower_strict, tot.astype(BF16), preferred_element_type=F32)
        return xf, incl, base + incl - xf

    _, _, eq_rank = prefix(eq)
    need = cap - jnp.sum(jnp.where(gt, 1.0, 0.0))
    sel = gt | (eq & (eq_rank < need))
    xf, incl, rank = prefix(sel)
    rank_ref[0] = rank.astype(I32)
    slot_ref[0] = jnp.where(sel, rank, -1.0).astype(I32)

    ones = jnp.ones((SUBLANES, LANES), BF16)
    tot_l = lax.dot_general(ones, xf.astype(BF16), _NT, preferred_element_type=F32)
    off_l = jnp.dot(tot_l.astype(BF16), upper_strict, preferred_element_type=F32)[0:1]
    tot_l = tot_l[0:1]

    jf = lax.broadcasted_iota(I32, (cap, nrow), 0).astype(F32)
    oh_row = (off_l <= jf) & (jf < off_l + tot_l)
    ohb = jnp.where(oh_row, 1.0, 0.0).astype(BF16)
    row_id = lax.broadcasted_iota(I32, (cap, nrow), 1).astype(F32)
    off_j = jnp.sum(jnp.where(oh_row, off_l, 0.0), axis=1, keepdims=True)
    row_j = jnp.sum(jnp.where(oh_row, row_id, 0.0), axis=1, keepdims=True)
    key = jnp.where(sel, incl, 0.0).astype(BF16)
    g = jnp.dot(ohb, key, preferred_element_type=F32)
    target = lax.broadcasted_iota(I32, (cap, 1), 0).astype(F32) - off_j + 1.0
    oh_lane = g == target
    lane_id = lax.broadcasted_iota(I32, (cap, LANES), 1).astype(F32)
    lane_j = jnp.sum(jnp.where(oh_lane, lane_id, 0.0), axis=1, keepdims=True)
    idx_ref[0] = (row_j * LANES + lane_j).astype(I32)

    a1 = a.astype(BF16)
    r1 = a - a1.astype(F32)
    a2 = r1.astype(BF16)
    a3 = (r1 - a2.astype(F32)).astype(BF16)
    arow = (jnp.dot(ohb, a1, preferred_element_type=F32) + jnp.dot(ohb, a2, preferred_element_type=F32)
            + jnp.dot(ohb, a3, preferred_element_type=F32))
    gate_ref[0] = jnp.sum(jnp.where(oh_lane, arow, 0.0), axis=1, keepdims=True)


def _route(aff3, cap):
    nrow = aff3.shape[1]
    return pl.pallas_call(
        functools.partial(_route_kernel, cap=cap, nrow=nrow),
        grid=(N_EXPERTS,),
        in_specs=[pl.BlockSpec((N_EXPERTS * nrow, LANES), lambda e: (0, 0)),
                  pl.BlockSpec((1, nrow, LANES), lambda e: (e, 0, 0))],
        out_specs=[pl.BlockSpec((1, cap, 1), lambda e: (e, 0, 0)),
                   pl.BlockSpec((1, cap, 1), lambda e: (e, 0, 0)),
                   pl.BlockSpec((1, nrow, LANES), lambda e: (e, 0, 0)),
                   pl.BlockSpec((1, nrow, LANES), lambda e: (e, 0, 0))],
        out_shape=[jax.ShapeDtypeStruct((N_EXPERTS, cap, 1), I32),
                   jax.ShapeDtypeStruct((N_EXPERTS, cap, 1), F32),
                   jax.ShapeDtypeStruct((N_EXPERTS, nrow, LANES), I32),
                   jax.ShapeDtypeStruct((N_EXPERTS, nrow, LANES), I32)],
        scratch_shapes=[pltpu.VMEM((N_EXPERTS, SUBLANES, LANES), F32)],
        compiler_params=_cparams(("arbitrary",)),
        name="ec_route",
    )(aff3.reshape(N_EXPERTS * nrow, LANES), aff3)


def _ffn_kernel(idx_ref, h2_hbm, gate_ref, w1_ref, w3_ref, w2_ref, ye_ref, buf, sem, wb_scr, *, tc, nc):
    sub = D // LANES

    @pl.when(pl.program_id(1) == 0)
    def _():
        wb_scr[0] = w1_ref[0, 0].astype(BF16)
        wb_scr[1] = w3_ref[0, 0].astype(BF16)
        wb_scr[2] = w2_ref[0, 0].astype(BF16)

    step = pl.program_id(0) * nc + pl.program_id(1)
    nsteps = N_EXPERTS * nc

    def row_copy(tok, r, slot):
        return pltpu.make_async_copy(h2_hbm.at[pl.ds(pl.multiple_of(tok * sub, sub), sub), :],
                                     buf.at[slot, pl.ds(pl.multiple_of(r * sub, sub), sub), :],
                                     sem.at[slot])

    def wait_slot(slot):
        pltpu.make_async_copy(h2_hbm.at[pl.ds(0, tc * sub), :], buf.at[slot], sem.at[slot]).wait()

    @pl.when(step == 0)
    def _():
        def body(r, carry):
            row_copy(idx_ref[0, r], r, 0).start()
            return carry

        lax.fori_loop(0, tc, body, 0, unroll=16)

    nxt = jnp.minimum(step + 1, nsteps - 1)
    nxt_e = nxt // nc
    nxt_base = (nxt % nc) * tc
    nslot = (step + 1) % 2
    parts = 4 if tc % 4 == 0 else 1

    def issue_part(q):
        for r in range(q * tc // parts, (q + 1) * tc // parts):
            row_copy(idx_ref[nxt_e, nxt_base + r], r, nslot).start(priority=r % 2)

    slot = step % 2
    wait_slot(slot)
    xin = jnp.concatenate([buf[slot, pl.ds(s, tc, stride=sub), :] for s in range(sub)], axis=1).astype(BF16)
    issue_part(0)
    a = jnp.dot(xin, wb_scr[0], preferred_element_type=F32)
    if parts == 4:
        issue_part(1)
    b = jnp.dot(xin, wb_scr[1], preferred_element_type=F32)
    hid = (a * jax.nn.sigmoid(a) * b).astype(BF16)
    if parts == 4:
        issue_part(2)
    y = jnp.dot(hid, wb_scr[2], preferred_element_type=F32)
    if parts == 4:
        issue_part(3)
    ye_ref[0] = (y * gate_ref[0]).astype(BF16)

    @pl.when(step == nsteps - 1)
    def _():
        wait_slot(nslot)


def _ffn(idx, gate, h2, w1, w3, w2, layer, tc):
    cap = idx.shape[1]
    nc = cap // tc
    sub = D // LANES
    wspec = pl.BlockSpec((1, 1, D, D), lambda e, c, idx: (layer, e, 0, 0))
    grid_spec = pltpu.PrefetchScalarGridSpec(
        num_scalar_prefetch=1,
        grid=(N_EXPERTS, nc),
        in_specs=[pl.BlockSpec(memory_space=pl.ANY),
                  pl.BlockSpec((1, tc, 1), lambda e, c, idx: (e, c, 0)),
                  wspec, wspec, wspec],
        out_specs=pl.BlockSpec((1, tc, D), lambda e, c, idx: (e, c, 0)),
        scratch_shapes=[pltpu.VMEM((2, tc * sub, LANES), F32),
                        pltpu.SemaphoreType.DMA((2,)),
                        pltpu.VMEM((3, D, D), BF16)])
    return pl.pallas_call(
        functools.partial(_ffn_kernel, tc=tc, nc=nc),
        grid_spec=grid_spec,
        out_shape=jax.ShapeDtypeStruct((N_EXPERTS, cap, D), BF16),
        compiler_params=_cparams(("arbitrary", "arbitrary")),
        name="expert_ffn",
    )(idx, h2, gate, w1, w3, w2)


def _combine_kernel(rs_ref, x_ref, slot_ref, mod_ref, gf_ref, ye_hbm, o_ref, win, sem, y_scr,
                    *, win_rows, cap, nt, tt, mrow, final):
    i = pl.program_id(0)

    def start_of(e, t):
        s0 = rs_ref[e, t]
        s_al = (s0 // BF16_ROWS) * BF16_ROWS
        return pl.multiple_of(jnp.minimum(s_al, cap - win_rows), BF16_ROWS)

    def copy(e, t, sl):
        return pltpu.make_async_copy(ye_hbm.at[e, pl.ds(start_of(e, t), win_rows), :],
                                     win.at[sl, e], sem.at[sl])

    @pl.when(i == 0)
    def _():
        for e in range(N_EXPERTS):
            copy(e, i, 0).start()

    @pl.when(i + 1 < nt)
    def _():
        for e in range(N_EXPERTS):
            copy(e, i + 1, (i + 1) % 2).start()

    sl = i % 2
    for e in range(N_EXPERTS):
        copy(e, i, sl).wait()

    def window_sum(r0, r1):
        acc = jnp.zeros((tt, D), F32)
        wi = lax.broadcasted_iota(I32, (r1 - r0, tt), 0) + r0
        for e in range(N_EXPERTS):
            rel = slot_ref[e:e + 1, :] - start_of(e, i)
            oh_t = jnp.where(wi == rel, 1.0, 0.0).astype(BF16)
            acc = acc + lax.dot_general(oh_t, win[sl, e, r0:r1, :], _TN, preferred_element_type=F32)
        return acc

    main_rows = min(win_rows, tt)
    gt2 = mod_ref[mrow:mrow + 1, 5 * D:6 * D]
    y_scr[...] = x_ref[...] + gt2 * window_sum(0, main_rows)
    if win_rows > main_rows:
        need_tail = False
        for e in range(N_EXPERTS):
            end = rs_ref[e, i + 1] - start_of(e, i)
            need_tail = jnp.logical_or(need_tail, end > main_rows)

        @pl.when(need_tail)
        def _():
            y_scr[...] += gt2 * window_sum(main_rows, win_rows)

    y = y_scr[...]
    if final:
        ms = jnp.mean(y * y, axis=-1, keepdims=True)
        y = y * lax.rsqrt(ms + EPS) * gf_ref[...]
    o_ref[...] = y


def _combine(rstart, x1, slot, mod, mrow, g_final, ye, tt, final):
    n = x1.shape[0]
    cap = ye.shape[1]
    nt = n // tt
    win_rows = min(cap, tt + BF16_ROWS)
    grid_spec = pltpu.PrefetchScalarGridSpec(
        num_scalar_prefetch=1,
        grid=(nt,),
        in_specs=[pl.BlockSpec((tt, D), lambda i, rs: (i, 0)),
                  pl.BlockSpec((N_EXPERTS, tt), lambda i, rs: (0, i)),
                  pl.BlockSpec((SUBLANES, N_MOD * D), lambda i, rs: (0, 0)),
                  pl.BlockSpec((1, D), lambda i, rs: (0, 0)),
                  pl.BlockSpec(memory_space=pl.ANY)],
        out_specs=pl.BlockSpec((tt, D), lambda i, rs: (i, 0)),
        scratch_shapes=[pltpu.VMEM((2, N_EXPERTS, win_rows, D), BF16),
                        pltpu.SemaphoreType.DMA((2,)),
                        pltpu.VMEM((tt, D), F32)])
    return pl.pallas_call(
        functools.partial(_combine_kernel, win_rows=win_rows, cap=cap, nt=nt, tt=tt, mrow=mrow, final=final),
        grid_spec=grid_spec,
        out_shape=jax.ShapeDtypeStruct((n, D), F32),
        compiler_params=_cparams(("arbitrary",)),
        name="moe_combine",
    )(rstart, x1, slot, mod, g_final, ye)


def _rope_tables(n_lat):
    rows = n_lat // GRID_W
    inv = ROPE_THETA ** (-jnp.arange(0, ROPE_AXIS, 2, dtype=F32) / ROPE_AXIS)
    ang_r = jnp.arange(rows).astype(F32)[:, None] * inv
    ang_c = jnp.arange(GRID_W).astype(F32)[:, None] * inv
    cr, sr, cc, sc = jnp.cos(ang_r), jnp.sin(ang_r), jnp.cos(ang_c), jnp.sin(ang_c)
    zr, zc = jnp.zeros_like(sr), jnp.zeros_like(sc)
    rep = lambda parts: jnp.concatenate(parts * (LANES // DA_DH), axis=1)

    def table(row_parts, col_parts):
        r = rep([row_parts[0], row_parts[1], zr, zr])
        c = rep([zc, zc, col_parts[0], col_parts[1]])
        return (r[:, None, :] + c[None, :, :]).reshape(n_lat, LANES)

    return (table((cr, cr), (cc, cc)), table((zr, sr), (zc, sc)), table((-sr, zr), (-sc, zc)))


def _largest_tile(n, unit, limit):
    best = unit
    for k in range(1, n // unit + 1):
        if n % (k * unit) == 0 and k * unit <= limit:
            best = k * unit
    return best


def _moe(x1, h2, aff_t, mod, mrow, w1, w3, w2, layer, g_final, final):
    n = x1.shape[0]
    cap = max(1, EC_CAPACITY * n // N_EXPERTS)
    tile = LANES * LANES if n <= LANES * LANES else n
    n_pad = max(n, tile)
    if n_pad > n:
        aff_t = jnp.concatenate([aff_t, jnp.full((N_EXPERTS, n_pad - n), -1.0, F32)], axis=1)
    idx, gate, slot, rank = _route(aff_t.reshape(N_EXPERTS, n_pad // LANES, LANES), cap)
    tt = min(256, n)
    slot = slot.reshape(N_EXPERTS, n_pad)[:, :n]
    rstart = jnp.concatenate([rank.reshape(N_EXPERTS, n_pad)[:, 0:n:tt],
                              jnp.full((N_EXPERTS, 1), cap, I32)], axis=1)
    tc = min(cap, 512)
    ye = _ffn(idx.reshape(N_EXPERTS, cap), gate, h2, w1, w3, w2, layer, tc)
    return _combine(rstart, x1, slot, mod, mrow, g_final, ye, tt, final)


def kernel(x, c, ctx, c_ctx, w_ada, b_ada, g_norm1, g_norm2, w_in, na_rpb, da_lam_q1, da_lam_k1, da_lam_q2,
           da_lam_k2, da_subln_g, gm_ln_g, gm_ln_b, gm_w_s, gm_b_s, w_branch, w_out, w_router, w_e1, w_e3,
           w_e2, g_final):
    depth = w_ada.shape[0]
    n_lat, n_ctx = x.shape[1], ctx.shape[1]
    rows = n_lat // GRID_W
    xs, xc = x[0], ctx[0]

    cc = jnp.concatenate([c.reshape(1, D), c_ctx.reshape(1, D), jnp.zeros((SUBLANES - 2, D), F32)], axis=0)
    mods = _ada(cc, w_ada, b_ada)

    tabs_lat = _rope_tables(n_lat)
    tabs_ctx = (jnp.ones((n_ctx, LANES), F32), jnp.zeros((n_ctx, LANES), F32), jnp.zeros((n_ctx, LANES), F32))
    tm_proj = _largest_tile(n_lat, 256, 1024)
    tk = _largest_tile(n_lat, 256, 2048)
    tq = 512

    for i in range(depth):
        last = i == depth - 1
        lam_init = 0.8 - 0.6 * math.exp(-0.3 * i)
        mod = mods[i]
        w_in_b = w_in[i].astype(BF16)
        g1 = g_norm1[i].reshape(1, D)
        g2 = g_norm2[i].reshape(1, D)
        gf = g_final.reshape(1, D)
        lam_vecs = [v[i].reshape(1, DA_DH).astype(F32) for v in (da_lam_q1, da_lam_k1, da_lam_q2, da_lam_k2)]
        g_col = da_subln_g[i].reshape(DA_VD, 1)
        merge_w = (w_branch[i].astype(BF16), w_out[i].astype(BF16), gm_ln_g[i].reshape(1, BW),
                   gm_ln_b[i].reshape(1, BW), gm_w_s[i].astype(BF16), gm_b_s[i].T, g2, w_router[i].T)

        p_lat, qt_lat, vt_lat = _proj(xs, g1, mod, w_in_b, tabs_lat, 0, tm_proj)
        p_ctx, qt_ctx, vt_ctx = _proj(xc, g1, mod, w_in_b, tabs_ctx, 1, n_ctx)

        ya = _na(p_lat, p_ctx, _na_bias(na_rpb[i], rows))
        yb = _diff(qt_lat, p_ctx, vt_ctx, (p_lat, vt_lat), lam_vecs, g_col, lam_init, tq, tk)
        x1, h2, aff_t = _merge(xs, ya, yb, p_lat, mod, 0, *merge_w, tm=_largest_tile(n_lat, 256, 512))
        xs = _moe(x1, h2, aff_t, mod, 0, w_e1, w_e3, w_e2, i, gf, last)

        if not last:
            yac = _ctx_dense(p_ctx)
            ybc = _diff(qt_ctx, p_ctx, vt_ctx, None, lam_vecs, g_col, lam_init, n_ctx, n_ctx)
            x1c, h2c, aff_tc = _merge(xc, yac, ybc, p_ctx, mod, 1, *merge_w, tm=n_ctx)
            xc = _moe(x1c, h2c, aff_tc, mod, 1, w_e1, w_e3, w_e2, i, gf, False)
    return xs[None]
```
